```python
import math
import jax, jax.numpy as jnp
from jax import lax
import numpy as np

D_MODEL = 1024
BATCH = 8
SEQ = 4096
DEPTH = 2

HEAD_DIM = 64
NSA_HEADS = 8
NSA_KV_HEADS = 2
NSA_GROUP = NSA_HEADS // NSA_KV_HEADS
CMP_LEN = 32
CMP_STRIDE = 16
CMP_HIDDEN = 256
SEL_BLOCK = 64
SEL_TOPN = 16
NSA_WINDOW = 512
SEL_QBLK = 64
FORCE_SCORE = 1e4
DIL_PAIRS = ((128, 1), (512, 4), (2048, 16))
N_DIL_GROUPS = 3
DIL_HEADS_PER_GROUP = 4
DIL_HEADS = N_DIL_GROUPS * DIL_HEADS_PER_GROUP
QBLK = 128
NUM_BUCKETS = 32
REL_MAX_DIST = 2048
N_BIAS_HEADS = NSA_HEADS + DIL_HEADS
N_EXPERT_GROUPS = 4
EXPERTS_PER_GROUP = 8
N_EXPERTS = N_EXPERT_GROUPS * EXPERTS_PER_GROUP
TOPK_IN_GROUP = 2
D_EXPERT = 512
MOE_ROW_BLOCK = 256
RMS_EPS = 1e-6
NSA_Q_W = NSA_HEADS * HEAD_DIM
NSA_KV_W = 3 * 2 * NSA_KV_HEADS * HEAD_DIM
NSA_GATE_W = 3 * NSA_HEADS
DIL_QKV_W = 3 * DIL_HEADS * HEAD_DIM
MERGE_W = 2 * D_MODEL
IN_W = NSA_Q_W + NSA_KV_W + NSA_GATE_W + DIL_QKV_W + MERGE_W
IN_SPLITS = (NSA_Q_W, NSA_Q_W + NSA_KV_W, NSA_Q_W + NSA_KV_W + NSA_GATE_W,
             NSA_Q_W + NSA_KV_W + NSA_GATE_W + DIL_QKV_W)
NSA_OUT_W = NSA_HEADS * HEAD_DIM
DIL_OUT_W = DIL_HEADS_PER_GROUP * HEAD_DIM

kernel_name = "hybrid_nsa_dilated_hmoe_adaln"


def rmsnorm(x, g):
    xf = x.astype(jnp.float32)
    y = xf * lax.rsqrt(jnp.mean(xf * xf, axis=-1, keepdims=True) + RMS_EPS)
    return (y * g.astype(jnp.float32)).astype(x.dtype)


def t5_bucket(dist):
    dist = jnp.maximum(dist, 0)
    max_exact = NUM_BUCKETS // 2
    df = jnp.maximum(dist, 1).astype(jnp.float32)
    large = max_exact + (jnp.log(df / max_exact) / math.log(REL_MAX_DIST / max_exact)
                         * (NUM_BUCKETS - max_exact)).astype(jnp.int32)
    large = jnp.minimum(large, NUM_BUCKETS - 1)
    return jnp.where(dist < max_exact, dist, large)


def banded_attention(q, k, v, window, tbl, dist_scale):
    B, L, Hk, G, dh = q.shape
    nb = -(-L // QBLK)
    Lp = nb * QBLK
    P = -(-window // QBLK) * QBLK
    q = jnp.pad(q, ((0, 0), (0, Lp - L), (0, 0), (0, 0), (0, 0)))
    k = jnp.pad(k, ((0, 0), (P, Lp - L), (0, 0), (0, 0)))
    v = jnp.pad(v, ((0, 0), (P, Lp - L), (0, 0), (0, 0)))
    scale = dh ** -0.5

    def step(i):
        qs = i * QBLK
        qb = lax.dynamic_slice_in_dim(q, qs, QBLK, axis=1)
        kb = lax.dynamic_slice_in_dim(k, qs, QBLK + P, axis=1)
        vb = lax.dynamic_slice_in_dim(v, qs, QBLK + P, axis=1)
        s = jnp.einsum('bqhgd,bkhd->bhgqk', qb, kb).astype(jnp.float32) * scale
        tq = qs + jnp.arange(QBLK)
        tk = qs - P + jnp.arange(QBLK + P)
        dist = tq[:, None] - tk[None, :]
        valid = (dist >= 0) & (dist <= window) & (tk[None, :] >= 0)
        bias = jnp.transpose(tbl[t5_bucket(dist * dist_scale)], (2, 3, 0, 1)).astype(jnp.float32)
        s = jnp.where(valid, s + bias, -jnp.inf)
        m = jnp.max(s, axis=-1, keepdims=True)
        p = jnp.exp(s - m)
        den = jnp.sum(p, axis=-1, keepdims=True)
        o = jnp.einsum('bhgqk,bkhd->bqhgd', (p / den).astype(vb.dtype), vb)
        return o, (m + jnp.log(den))[..., 0]

    o, lse = lax.map(step, jnp.arange(nb))
    o = jnp.moveaxis(o, 0, 1).reshape(B, Lp, Hk, G, dh)[:, :L]
    lse = jnp.transpose(lse, (1, 0, 4, 2, 3)).reshape(B, Lp, Hk, G)[:, :L]
    return o, lse


def compress_blocks(k, pos, w1, w2):
    B, S, Hk, dh = k.shape
    nc = (S - CMP_LEN) // CMP_STRIDE + 1
    idx = jnp.arange(nc)[:, None] * CMP_STRIDE + jnp.arange(CMP_LEN)[None, :]
    blk = k[:, idx] + pos[None, None, :, None, :]
    blk = blk.transpose(0, 1, 3, 2, 4).reshape(B, nc, Hk, CMP_LEN * dh)
    return jax.nn.gelu(blk @ w1) @ w2


def nsa_compressed_selected(q, kc, vc, ks, vs, tbl):
    B, S, Hk, G, dh = q.shape
    nc = kc.shape[1]
    n_sel = S // SEL_BLOCK
    n_top = min(SEL_TOPN, n_sel)
    scale = dh ** -0.5
    c_start = jnp.arange(nc) * CMP_STRIDE
    c_end = c_start + CMP_LEN - 1
    s_start = jnp.arange(n_sel) * SEL_BLOCK
    overlap = jnp.clip(jnp.minimum(c_start[:, None] + CMP_LEN, s_start[None, :] + SEL_BLOCK)
                       - jnp.maximum(c_start[:, None], s_start[None, :]), 0).astype(jnp.float32) / CMP_STRIDE
    ks_blk = ks.transpose(0, 2, 1, 3).reshape(B, Hk, n_sel, SEL_BLOCK, dh)
    vs_blk = vs.transpose(0, 2, 1, 3).reshape(B, Hk, n_sel, SEL_BLOCK, dh)
    tbl_h = jnp.transpose(tbl, (1, 0, 2))
    bi = jnp.arange(B)[:, None, None, None]
    hi = jnp.arange(Hk)[None, :, None, None]
    blk_ids = jnp.arange(n_sel)

    def step(i):
        qs = i * SEL_QBLK
        qb = lax.dynamic_slice_in_dim(q, qs, SEL_QBLK, axis=1)
        t = qs + jnp.arange(SEL_QBLK)
        sc = jnp.einsum('bqhgd,bchd->bhgqc', qb, kc).astype(jnp.float32) * scale
        dist_c = t[:, None] - c_end[None, :]
        bias_c = jnp.transpose(tbl[t5_bucket(dist_c)], (2, 3, 0, 1)).astype(jnp.float32)
        sc = jnp.where(dist_c >= 0, sc + bias_c, -jnp.inf)
        m = jnp.max(sc, axis=-1, keepdims=True)
        m = jnp.where(jnp.isfinite(m), m, 0.0)
        pc = jnp.exp(sc - m)
        den = jnp.sum(pc, axis=-1, keepdims=True)
        pc = pc / jnp.where(den > 0, den, 1.0)
        o_c = jnp.einsum('bhgqc,bchd->bqhgd', pc.astype(vc.dtype), vc)
        imp = jnp.einsum('bhgqc,cn->bhqn', pc, overlap)
        jq = t // SEL_BLOCK
        forced = (blk_ids[None] == 0) | (blk_ids[None] == jq[:, None]) | (blk_ids[None] == jq[:, None] - 1)
        future = blk_ids[None] > jq[:, None]
        score = jnp.where(forced, FORCE_SCORE, jnp.where(future, -1.0, imp))
        _, idx = lax.top_k(score, n_top)
        ksel = ks_blk[bi, hi, idx].reshape(B, Hk, SEL_QBLK, n_top * SEL_BLOCK, dh)
        vsel = vs_blk[bi, hi, idx].reshape(B, Hk, SEL_QBLK, n_top * SEL_BLOCK, dh)
        pos = (idx[..., None] * SEL_BLOCK + jnp.arange(SEL_BLOCK)).reshape(B, Hk, SEL_QBLK, n_top * SEL_BLOCK)
        dist = t[None, None, :, None] - pos
        bias_s = jnp.moveaxis(tbl_h[hi, t5_bucket(dist)], -1, 2).astype(jnp.float32)
        ss = jnp.einsum('bqhgd,bhqkd->bhgqk', qb, ksel).astype(jnp.float32) * scale
        ss = jnp.where((dist >= 0)[:, :, None], ss + bias_s, -jnp.inf)
        ps = jax.nn.softmax(ss, axis=-1)
        o_s = jnp.einsum('bhgqk,bhqkd->bqhgd', ps.astype(vsel.dtype), vsel)
        return o_c, o_s

    o_c, o_s = lax.map(step, jnp.arange(S // SEL_QBLK))
    o_c = jnp.moveaxis(o_c, 0, 1).reshape(B, S, Hk, G, dh)
    o_s = jnp.moveaxis(o_s, 0, 1).reshape(B, S, Hk, G, dh)
    return o_c, o_s


def dilated_attention(q, k, v, tbl):
    B, S = q.shape[0], q.shape[1]
    outs, lses = [], []
    for g, (window, d) in enumerate(DIL_PAIRS):
        def sub(a):
            return a.reshape(B, S // d, d, *a.shape[2:]).swapaxes(1, 2).reshape(B * d, S // d, *a.shape[2:])
        o, lse = banded_attention(sub(q[:, :, g])[:, :, :, None], sub(k[:, :, g]), sub(v[:, :, g]),
                                  window // d, tbl[:, g, :, None], d)
        H, dh = o.shape[2], o.shape[4]
        outs.append(o[:, :, :, 0].reshape(B, d, S // d, H, dh).swapaxes(1, 2).reshape(B, S, H, dh))
        lses.append(lse[..., 0].reshape(B, d, S // d, H).swapaxes(1, 2).reshape(B, S, H))
    w = jax.nn.softmax(jnp.stack(lses, axis=0), axis=0)
    return jnp.sum(w[..., None] * jnp.stack(outs, axis=0).astype(jnp.float32), axis=0)


def hybrid_mixer(h, w_in, cmp_pos, cmp_w1, cmp_w2, w_up_nsa, w_up_dil, w_o, rel_bias):
    B, S, D = h.shape
    proj = h @ w_in
    q_n, kv_n, gate_n, qkv_d, merge = jnp.split(proj, IN_SPLITS, axis=-1)
    q_n = q_n.reshape(B, S, NSA_KV_HEADS, NSA_GROUP, HEAD_DIM)
    kv_n = kv_n.reshape(B, S, 3, 2, NSA_KV_HEADS, HEAD_DIM)
    gate_n = jax.nn.sigmoid(gate_n.astype(jnp.float32)).reshape(B, S, 3, NSA_KV_HEADS, NSA_GROUP)
    tbl_n = rel_bias[:, :NSA_HEADS].reshape(NUM_BUCKETS, NSA_KV_HEADS, NSA_GROUP)
    tbl_d = rel_bias[:, NSA_HEADS:].reshape(NUM_BUCKETS, N_DIL_GROUPS, DIL_HEADS_PER_GROUP)
    kc = compress_blocks(kv_n[:, :, 0, 0], cmp_pos[0], cmp_w1[0], cmp_w2[0])
    vc = compress_blocks(kv_n[:, :, 0, 1], cmp_pos[1], cmp_w1[1], cmp_w2[1])
    o_c, o_s = nsa_compressed_selected(q_n, kc, vc, kv_n[:, :, 1, 0], kv_n[:, :, 1, 1], tbl_n)
    o_w, _ = banded_attention(q_n, kv_n[:, :, 2, 0], kv_n[:, :, 2, 1], NSA_WINDOW, tbl_n, 1)
    o_nsa = (gate_n[:, :, 0, ..., None] * o_c + gate_n[:, :, 1, ..., None] * o_s
             + gate_n[:, :, 2, ..., None] * o_w).reshape(B, S, NSA_OUT_W).astype(h.dtype)
    qkv_d = qkv_d.reshape(B, S, 3, N_DIL_GROUPS, DIL_HEADS_PER_GROUP, HEAD_DIM)
    o_dil = dilated_attention(qkv_d[:, :, 0], qkv_d[:, :, 1], qkv_d[:, :, 2], tbl_d)
    o_dil = o_dil.reshape(B, S, DIL_OUT_W).astype(h.dtype)
    g_m = jax.nn.sigmoid(merge.astype(jnp.float32)).reshape(B, S, 2, D)
    merged = g_m[:, :, 0] * (o_nsa @ w_up_nsa) + g_m[:, :, 1] * (o_dil @ w_up_dil)
    return merged.astype(h.dtype) @ w_o


def hierarchical_moe(h, wg, bg, we, be, w1, w3, w2):
    B, S, D = h.shape
    T = B * S
    hf = h.reshape(T, D)
    lg = (hf @ wg).astype(jnp.float32) + bg.astype(jnp.float32)
    grp = jnp.argmax(lg, axis=-1)
    p_grp = jnp.take_along_axis(jax.nn.softmax(lg, axis=-1), grp[:, None], axis=-1)
    le = ((hf @ we).astype(jnp.float32) + be.astype(jnp.float32)).reshape(T, N_EXPERT_GROUPS, EXPERTS_PER_GROUP)
    le = jnp.take_along_axis(le, grp[:, None, None], axis=1)[:, 0]
    top_v, top_i = lax.top_k(le, TOPK_IN_GROUP)
    wts = jax.nn.softmax(top_v, axis=-1) * p_grp
    eid = (grp[:, None] * EXPERTS_PER_GROUP + top_i).astype(jnp.int32)
    N = T * TOPK_IN_GROUP
    e_flat = eid.reshape(-1)
    tok = jnp.repeat(jnp.arange(T, dtype=jnp.int32), TOPK_IN_GROUP)
    w_flat = wts.reshape(-1)
    order = jnp.argsort(e_flat)
    e_s, tok_s, w_s = e_flat[order], tok[order], w_flat[order]
    counts = jnp.bincount(e_flat, length=N_EXPERTS)
    pcounts = (counts + MOE_ROW_BLOCK - 1) // MOE_ROW_BLOCK * MOE_ROW_BLOCK
    pend = jnp.cumsum(pcounts)
    pstart = pend - pcounts
    start = jnp.cumsum(counts) - counts
    dest = pstart[e_s] + (jnp.arange(N) - start[e_s])
    nb = -(-N // MOE_ROW_BLOCK) + N_EXPERTS
    n_pad = nb * MOE_ROW_BLOCK
    xs = jnp.zeros((n_pad, D), h.dtype).at[dest].set(hf[tok_s])
    blk_e = jnp.clip(jnp.searchsorted(pend, jnp.arange(nb) * MOE_ROW_BLOCK, side='right'), 0, N_EXPERTS - 1)

    def expert_block(args):
        xb, e = args
        return (jax.nn.silu(xb @ w1[e]) * (xb @ w3[e])) @ w2[e]

    ys = lax.map(expert_block, (xs.reshape(nb, MOE_ROW_BLOCK, D), blk_e)).reshape(n_pad, D)
    out = jnp.zeros((T, D), h.dtype).at[tok_s].add((w_s[:, None] * ys[dest]).astype(h.dtype))
    return out.reshape(B, S, D)


def setup_inputs(seed: int = 0) -> dict:
    key = jax.random.key(seed)
    ks = jax.random.split(key, 24)

    def nrm(k, shape, s):
        return jax.random.normal(k, shape, jnp.float32) * s

    return {
        "x": nrm(ks[0], (BATCH, SEQ, D_MODEL), 1.0),
        "c": nrm(ks[1], (BATCH, D_MODEL), 1.0),
        "rel_bias": nrm(ks[2], (NUM_BUCKETS, N_BIAS_HEADS), 0.2),
        "ada_w": nrm(ks[3], (DEPTH, D_MODEL, 6 * D_MODEL), D_MODEL ** -0.5),
        "ada_b": nrm(ks[4], (DEPTH, 6 * D_MODEL), 0.01),
        "norm1": 1.0 + nrm(ks[5], (DEPTH, D_MODEL), 0.02),
        "norm2": 1.0 + nrm(ks[6], (DEPTH, D_MODEL), 0.02),
        "w_in": nrm(ks[7], (DEPTH, D_MODEL, IN_W), D_MODEL ** -0.5),
        "cmp_pos": nrm(ks[8], (DEPTH, 2, CMP_LEN, HEAD_DIM), 0.1),
        "cmp_w1": nrm(ks[9], (DEPTH, 2, CMP_LEN * HEAD_DIM, CMP_HIDDEN), (CMP_LEN * HEAD_DIM) ** -0.5),
        "cmp_w2": nrm(ks[10], (DEPTH, 2, CMP_HIDDEN, HEAD_DIM), CMP_HIDDEN ** -0.5),
        "w_up_nsa": nrm(ks[11], (DEPTH, NSA_OUT_W, D_MODEL), NSA_OUT_W ** -0.5),
        "w_up_dil": nrm(ks[12], (DEPTH, DIL_OUT_W, D_MODEL), DIL_OUT_W ** -0.5),
        "w_o": nrm(ks[13], (DEPTH, D_MODEL, D_MODEL), D_MODEL ** -0.5),
        "router_wg": nrm(ks[14], (DEPTH, D_MODEL, N_EXPERT_GROUPS), D_MODEL ** -0.5),
        "router_bg": nrm(ks[15], (DEPTH, N_EXPERT_GROUPS), 0.01),
        "router_we": nrm(ks[16], (DEPTH, D_MODEL, N_EXPERTS), D_MODEL ** -0.5),
        "router_be": nrm(ks[17], (DEPTH, N_EXPERTS), 0.01),
        "exp_w1": nrm(ks[18], (DEPTH, N_EXPERTS, D_MODEL, D_EXPERT), D_MODEL ** -0.5),
        "exp_w3": nrm(ks[19], (DEPTH, N_EXPERTS, D_MODEL, D_EXPERT), D_MODEL ** -0.5),
        "exp_w2": nrm(ks[20], (DEPTH, N_EXPERTS, D_EXPERT, D_MODEL), D_EXPERT ** -0.5),
        "norm_f": 1.0 + nrm(ks[21], (D_MODEL,), 0.02),
    }


def reference(x, c, rel_bias, ada_w, ada_b, norm1, norm2, w_in, cmp_pos, cmp_w1, cmp_w2,
              w_up_nsa, w_up_dil, w_o, router_wg, router_bg, router_we, router_be,
              exp_w1, exp_w3, exp_w2, norm_f):
    cond = jax.nn.silu(c)
    for l in range(DEPTH):
        mod = (cond @ ada_w[l] + ada_b[l])[:, None, :]
        sh1, sc1, g1, sh2, sc2, g2 = jnp.split(mod, 6, axis=-1)
        h = (rmsnorm(x, norm1[l]) * (1 + sc1) + sh1).astype(x.dtype)
        x = x + g1 * hybrid_mixer(h, w_in[l], cmp_pos[l], cmp_w1[l], cmp_w2[l],
                                  w_up_nsa[l], w_up_dil[l], w_o[l], rel_bias)
        h = (rmsnorm(x, norm2[l]) * (1 + sc2) + sh2).astype(x.dtype)
        x = x + g2 * hierarchical_moe(h, router_wg[l], router_bg[l], router_we[l], router_be[l],
                                      exp_w1[l], exp_w3[l], exp_w2[l])
    return rmsnorm(x, norm_f)
```

```python
import functools
import math

import numpy as np
import jax
import jax.numpy as jnp
from jax import lax
from jax.experimental import pallas as pl
from jax.experimental.pallas import tpu as pltpu

F32 = jnp.float32
BF16 = jnp.bfloat16

HEAD_DIM = 64
NSA_HEADS = 8
NSA_KV_HEADS = 2
NSA_GROUP = NSA_HEADS // NSA_KV_HEADS
CMP_LEN = 32
CMP_STRIDE = 16
CMP_HIDDEN = 256
SEL_BLOCK = 64
SEL_TOPN = 16
NSA_WINDOW = 512
FORCE_SCORE = 1e4
DIL_PAIRS = ((128, 1), (512, 4), (2048, 16))
N_DIL_GROUPS = 3
DIL_HEADS_PER_GROUP = 4
NUM_BUCKETS = 32
REL_MAX_DIST = 2048
N_EXPERT_GROUPS = 4
EXPERTS_PER_GROUP = 8
N_EXPERTS = N_EXPERT_GROUPS * EXPERTS_PER_GROUP
D_EXPERT = 512
RMS_EPS = 1e-6

LANES = 128
SUBLANES = 8
MASK_NEG = -1e30
SEL_NEG = -1e9
TQ_NSA = 256
TQ_DIL = 128
TM_PROJ = 512
MOE_CHUNK = 2048
MOE_ROWS = 128
VMEM_LIMIT = 56 * 1024 * 1024


def _cparams(sem):
    return pltpu.CompilerParams(dimension_semantics=sem, vmem_limit_bytes=VMEM_LIMIT)


def _np_bucket(dist):
    dist = np.maximum(dist, 0)
    max_exact = NUM_BUCKETS // 2
    df = np.maximum(dist, 1).astype(np.float32)
    val = np.log(df / np.float32(max_exact)) / np.float32(math.log(REL_MAX_DIST / max_exact))
    large = max_exact + (val * np.float32(NUM_BUCKETS - max_exact)).astype(np.int32)
    large = np.minimum(large, NUM_BUCKETS - 1)
    return np.where(dist < max_exact, dist, large).astype(np.int32)


def _toeplitz_buckets(n_delta, tile, window, dist_scale):
    dd = np.arange(n_delta)[:, None, None]
    r = np.arange(tile)[None, :, None]
    c = np.arange(tile)[None, None, :]
    dist = dd * tile + c - r
    valid = (dist >= 0) & (dist <= window)
    return np.where(valid, _np_bucket(dist * dist_scale), -1).astype(np.int32)


def _cmp_buckets(seq, n_cmp):
    c_end = np.arange(n_cmp)[:, None] * CMP_STRIDE + CMP_LEN - 1
    t = np.arange(seq)[None, :]
    dist = t - c_end
    return np.where(dist >= 0, _np_bucket(dist), -1).astype(np.int32)


def _overlap_t(n_cmp, n_sel):
    c_start = np.arange(n_cmp)[None, :] * CMP_STRIDE
    s_start = np.arange(n_sel)[:, None] * SEL_BLOCK
    ov = np.clip(np.minimum(c_start + CMP_LEN, s_start + SEL_BLOCK) - np.maximum(c_start, s_start), 0, None)
    out = np.zeros((LANES, n_cmp), np.float32)
    out[64:64 + n_sel] = ov.astype(np.float32) / CMP_STRIDE
    return out


def _block_onehot(seq):
    oh = np.zeros((seq, LANES), np.float32)
    blk = np.arange(seq) // SEL_BLOCK
    oh[np.arange(seq), blk] = 1.0
    oh[np.arange(seq), 64 + blk] = 1.0
    return oh


def _gate_expand():
    e = np.zeros((LANES, 3 * NSA_HEADS * HEAD_DIM), np.float32)
    for br in range(3):
        for g in range(NSA_GROUP):
            for ln in range(LANES):
                kv = ln // HEAD_DIM
                e[br * NSA_HEADS + kv * NSA_GROUP + g, br * 512 + g * LANES + ln] = 1.0
    return e


def _mod_kernel(c_ref, w_ref, b_ref, o_ref):
    c = c_ref[...]
    cond = c * (1.0 / (1.0 + jnp.exp(-c)))
    o_ref[...] = jnp.dot(cond, w_ref[...], preferred_element_type=F32,
                         precision=lax.Precision.HIGHEST) + b_ref[...]


def _modulation(c, ada_w, ada_b):
    depth, d, n = ada_w.shape
    b = c.shape[0]
    tn = 1536
    return pl.pallas_call(
        _mod_kernel,
        grid=(depth, n // tn),
        in_specs=[pl.BlockSpec((b, d), lambda l, j: (0, 0)),
                  pl.BlockSpec((None, d, tn), lambda l, j: (l, 0, j)),
                  pl.BlockSpec((None, 1, tn), lambda l, j: (l, 0, j))],
        out_specs=pl.BlockSpec((None, b, tn), lambda l, j: (l, 0, j)),
        out_shape=jax.ShapeDtypeStruct((depth, b, n), F32),
        compiler_params=_cparams(("arbitrary", "arbitrary")),
        name="adaln_modulation",
    )(c, ada_w, ada_b.reshape(depth, 1, n))


def _bias_kernel(tbl_ref, bkt_ref, o_ref, *, head_base):
    h = pl.program_id(0) + head_base
    bkt = bkt_ref[...]
    acc = jnp.full(bkt.shape, MASK_NEG, F32)
    for b in range(NUM_BUCKETS):
        acc = jnp.where(bkt == b, tbl_ref[b, h], acc)
    o_ref[...] = acc


def _expand_bias(rel_bias, buckets, head_base, n_heads, row_tile):
    rows, cols = buckets.shape
    return pl.pallas_call(
        functools.partial(_bias_kernel, head_base=head_base),
        grid=(n_heads, rows // row_tile),
        in_specs=[pl.BlockSpec(memory_space=pltpu.SMEM),
                  pl.BlockSpec((row_tile, cols), lambda h, i: (i, 0))],
        out_specs=pl.BlockSpec((None, row_tile, cols), lambda h, i: (h, i, 0)),
        out_shape=jax.ShapeDtypeStruct((n_heads, rows, cols), F32),
        compiler_params=_cparams(("arbitrary", "arbitrary")),
        name="bias_expand",
    )(rel_bias, jnp.asarray(buckets))


def _norm_mod(x, g, sc, sh):
    ms = jnp.mean(x * x, axis=-1, keepdims=True)
    y = x * lax.rsqrt(ms + RMS_EPS) * g
    return y * (1.0 + sc) + sh


def _inproj_kernel(x_ref, sc_ref, sh_ref, g_ref, wq_ref, wkv_ref, wd0_ref, wd1_ref, wd2_ref, wm_ref, wg_ref,
                   q_ref, kc_ref, vc_ref, ks_ref, vs_ref, kw_ref, vw_ref, d0_ref, d1_ref, d2_ref, m_ref, gl_ref):
    h = _norm_mod(x_ref[...], g_ref[...], sc_ref[...], sh_ref[...]).astype(BF16)

    def proj(w_ref):
        return jnp.dot(h, w_ref[...], preferred_element_type=F32)

    q_ref[...] = proj(wq_ref).astype(BF16)
    kv = proj(wkv_ref).astype(BF16)
    for k, ref in enumerate((kc_ref, vc_ref, ks_ref, vs_ref, kw_ref, vw_ref)):
        ref[...] = kv[:, k * LANES:(k + 1) * LANES]
    d0_ref[...] = proj(wd0_ref).astype(BF16)
    d1_ref[...] = proj(wd1_ref).astype(BF16)
    d2_ref[...] = proj(wd2_ref).astype(BF16)
    m_ref[...] = proj(wm_ref).astype(BF16)
    gl_ref[...] = proj(wg_ref)


def _inproj(x2, sc, sh, g, weights, seq):
    t, d = x2.shape
    tm = TM_PROJ
    per_b = seq // tm
    widths = (512, 128, 128, 128, 128, 128, 128, 768, 768, 768, 2 * d, LANES)
    dtypes = (BF16,) * 11 + (F32,)
    row = lambda i: (i, 0)
    const = lambda i: (0, 0)
    in_specs = [pl.BlockSpec((tm, d), row),
                pl.BlockSpec((None, 1, d), lambda i: (i // per_b, 0, 0)),
                pl.BlockSpec((None, 1, d), lambda i: (i // per_b, 0, 0)),
                pl.BlockSpec((1, d), const)]
    in_specs += [pl.BlockSpec(w.shape, const) for w in weights]
    return pl.pallas_call(
        _inproj_kernel,
        grid=(t // tm,),
        in_specs=in_specs,
        out_specs=[pl.BlockSpec((tm, w), row) for w in widths],
        out_shape=[jax.ShapeDtypeStruct((t, w), dt) for w, dt in zip(widths, dtypes)],
        compiler_params=_cparams(("arbitrary",)),
        name="norm_inproj",
    )(x2, sc, sh, g, *weights)


def _gelu_tanh(x):
    return 0.5 * x * (1.0 + jnp.tanh(math.sqrt(2.0 / math.pi) * (x + 0.044715 * (x * x * x))))


def _compress_kernel(r_ref, pt_ref, pb_ref, wt_ref, wb_ref, w2_ref, o_ref, *, transpose_out):
    r = r_ref[...].astype(F32)
    top = jnp.dot((r + pt_ref[...]).astype(BF16), wt_ref[...], preferred_element_type=F32)
    bot = jnp.dot((r + pb_ref[...]).astype(BF16), wb_ref[...], preferred_element_type=F32)
    n = bot.shape[0]
    hid = top + pltpu.roll(bot, n - 1, 0)
    act = _gelu_tanh(hid).astype(BF16)
    if transpose_out:
        o_ref[...] = lax.dot_general(w2_ref[...], act, (((1,), (1,)), ((), ())),
                                     preferred_element_type=F32).astype(BF16)
    else:
        o_ref[...] = jnp.dot(act, w2_ref[...], preferred_element_type=F32).astype(BF16)


def _compress(tok, pos, w1, w2, batch, seq, transpose_out):
    nc = seq // CMP_STRIDE
    half = CMP_LEN // 2
    eye = jnp.eye(NSA_KV_HEADS, dtype=F32)
    w1r = w1.reshape(CMP_LEN, HEAD_DIM, CMP_HIDDEN)
    blk = lambda w: jnp.einsum('ldn,hg->lhdgn', w, eye).reshape(half * LANES, 2 * CMP_HIDDEN).astype(BF16)
    wt, wb = blk(w1r[:half]), blk(w1r[half:])
    posr = lambda p: jnp.broadcast_to(p[:, None, :], (half, NSA_KV_HEADS, HEAD_DIM)).reshape(1, half * LANES)
    pt, pb = posr(pos[:half]), posr(pos[half:])
    w2b = jnp.einsum('nd,hg->hngd', w2, eye).reshape(2 * CMP_HIDDEN, LANES)
    if transpose_out:
        w2b = w2b.T
        out_block, out_shape = (None, LANES, nc), (batch, LANES, nc)
    else:
        out_block, out_shape = (None, nc, LANES), (batch, nc, LANES)
    w2b = w2b.astype(BF16)
    const = lambda b: (0, 0)
    return pl.pallas_call(
        functools.partial(_compress_kernel, transpose_out=transpose_out),
        grid=(batch,),
        in_specs=[pl.BlockSpec((None, nc, half * LANES), lambda b: (b, 0, 0)),
                  pl.BlockSpec(pt.shape, const), pl.BlockSpec(pb.shape, const),
                  pl.BlockSpec(wt.shape, const), pl.BlockSpec(wb.shape, const),
                  pl.BlockSpec(w2b.shape, const)],
        out_specs=pl.BlockSpec(out_block, lambda b: (b, 0, 0)),
        out_shape=jax.ShapeDtypeStruct(out_shape, BF16),
        compiler_params=_cparams(("arbitrary",)),
        name="nsa_compress",
    )(tok.reshape(batch, nc, half * LANES), pt, pb, wt, wb, w2b)


def _cmp_select_kernel(q_ref, kc_ref, vct_ref, bias_ref, ovl_ref, o_ref, selb_ref, *, n_sel, n_top):
    tq = q_ref.shape[0]
    nc = kc_ref.shape[0]
    qs = pl.program_id(1) * tq
    kc = kc_ref[...]
    vct = vct_ref[...]
    lane_q = lax.broadcasted_iota(jnp.int32, (tq, LANES), 1)
    row_o = lax.broadcasted_iota(jnp.int32, (LANES, tq), 0)
    pc_sum = [jnp.zeros((nc, tq), F32) for _ in range(NSA_KV_HEADS)]
    for g in range(NSA_GROUP):
        qt = q_ref[:, g * LANES:(g + 1) * LANES]
        outs = []
        for kv in range(NSA_KV_HEADS):
            mine = (lane_q < HEAD_DIM) if kv == 0 else (lane_q >= HEAD_DIM)
            qm = jnp.where(mine, qt, jnp.zeros_like(qt))
            s = lax.dot_general(kc, qm, (((1,), (1,)), ((), ())), preferred_element_type=F32)
            s = s + bias_ref[kv * NSA_GROUP + g]
            m = jnp.max(s, axis=0, keepdims=True)
            m = jnp.where(m < 0.5 * MASK_NEG, 0.0, m)
            p = jnp.exp(s - m)
            den = jnp.sum(p, axis=0, keepdims=True)
            pc = p * (1.0 / jnp.where(den > 0.0, den, 1.0))
            pc_sum[kv] = pc_sum[kv] + pc
            outs.append(jnp.dot(vct, pc.astype(BF16), preferred_element_type=F32))
        o_t = jnp.where(row_o < HEAD_DIM, outs[0], outs[1])
        o_ref[:, g * LANES:(g + 1) * LANES] = o_t.T.astype(BF16)

    rowj = lax.broadcasted_iota(jnp.int32, (HEAD_DIM, tq), 0)
    t = qs + lax.broadcasted_iota(jnp.int32, (HEAD_DIM, tq), 1)
    jq = jnp.right_shift(t, SEL_BLOCK.bit_length() - 1)
    forced = (rowj == 0) | (rowj == jq) | (rowj == jq - 1)
    future = rowj > jq
    ovl = ovl_ref[...]
    for kv in range(NSA_KV_HEADS):
        hi = pc_sum[kv].astype(BF16)
        lo = (pc_sum[kv] - hi.astype(F32)).astype(BF16)
        imp = (jnp.dot(ovl, hi, preferred_element_type=F32)
               + jnp.dot(ovl, lo, preferred_element_type=F32))[HEAD_DIM:]
        score = jnp.where(forced, FORCE_SCORE, jnp.where(future, -1.0, imp))
        rem = jnp.where(rowj < n_sel, score, -3e38)
        sel = jnp.zeros((HEAD_DIM, tq), F32)
        for _ in range(n_top):
            m = jnp.max(rem, axis=0, keepdims=True)
            idx = jnp.min(jnp.where(rem == m, rowj, HEAD_DIM), axis=0, keepdims=True)
            pick = rowj == idx
            sel = jnp.where(pick, 1.0, sel)
            rem = jnp.where(pick, -3e38, rem)
        sb = jnp.where(sel > 0.5, 0.0, SEL_NEG)
        zero = jnp.zeros_like(sb)
        full = jnp.concatenate([zero, sb], axis=0) if kv == 0 else jnp.concatenate([sb, zero], axis=0)
        selb_ref[:, kv * LANES:(kv + 1) * LANES] = full.T.astype(BF16)


def _cmp_select(q, kc, vct, bias_c, ovl_t, batch, seq):
    t = q.shape[0]
    tq = TQ_NSA
    nq = seq // tq
    nc = seq // CMP_STRIDE
    n_sel = seq // SEL_BLOCK
    n_top = min(SEL_TOPN, n_sel)
    return pl.pallas_call(
        functools.partial(_cmp_select_kernel, n_sel=n_sel, n_top=n_top),
        grid=(batch, nq),
        in_specs=[pl.BlockSpec((tq, 512), lambda b, i: (b * nq + i, 0)),
                  pl.BlockSpec((None, nc, LANES), lambda b, i: (b, 0, 0)),
                  pl.BlockSpec((None, LANES, nc), lambda b, i: (b, 0, 0)),
                  pl.BlockSpec((NSA_HEADS, nc, tq), lambda b, i: (0, 0, i)),
                  pl.BlockSpec((LANES, nc), lambda b, i: (0, 0))],
        out_specs=[pl.BlockSpec((tq, 512), lambda b, i: (b * nq + i, 0)),
                   pl.BlockSpec((tq, 2 * LANES), lambda b, i: (b * nq + i, 0))],
        out_shape=[jax.ShapeDtypeStruct((t, 512), BF16),
                   jax.ShapeDtypeStruct((t, 2 * LANES), BF16)],
        compiler_params=_cparams(("arbitrary", "arbitrary")),
        name="nsa_cmp_select",
    )(q, kc, vct, bias_c, ovl_t)


def _flash_kernel(*refs, tile, n_back, with_sel, with_lse):
    it = iter(refs)
    q_ref, k_ref, vt_ref, b0_ref, b1_ref = (next(it) for _ in range(5))
    selb_ref = next(it) if with_sel else None
    oh_ref = next(it) if with_sel else None
    o_ref = next(it)
    lse_ref = next(it) if with_lse else None
    acc0_ref, acc1_ref = next(it), next(it)

    i = pl.program_id(2)
    lane_q = lax.broadcasted_iota(jnp.int32, (tile, LANES), 1)
    lo = lane_q < HEAD_DIM
    q = q_ref[...]
    if with_sel:
        qa0 = jnp.where(lo, q, selb_ref[:, :LANES])
        qa1 = jnp.where(lo, selb_ref[:, LANES:], q)
    else:
        zero = jnp.zeros_like(q)
        qa0 = jnp.where(lo, q, zero)
        qa1 = jnp.where(lo, zero, q)
    acc0_ref[...] = jnp.zeros_like(acc0_ref)
    acc1_ref[...] = jnp.zeros_like(acc1_ref)
    n_steps = i + 1 if n_back is None else jnp.minimum(i, n_back) + 1
    nt = (((1,), (1,)), ((), ()))

    def one_head(ka, qa, bias, vt, m, l, acc_ref):
        s = lax.dot_general(ka, qa, nt, preferred_element_type=F32) + bias
        m_new = jnp.maximum(m, jnp.max(s, axis=0, keepdims=True))
        alpha = jnp.exp(m - m_new)
        p = jnp.exp(s - m_new)
        l_new = alpha * l + jnp.sum(p, axis=0, keepdims=True)
        acc_ref[...] = alpha * acc_ref[...] + jnp.dot(vt, p.astype(BF16), preferred_element_type=F32)
        return m_new, l_new

    def body(step, carry):
        m0, l0, m1, l1 = carry
        j = i - step
        ks = pl.multiple_of(j * tile, tile)
        kt = k_ref[pl.ds(ks, tile), :]
        if with_sel:
            oh = oh_ref[pl.ds(ks, tile), :]
            ka0 = jnp.where(lo, kt, oh)
            ka1 = jnp.where(lo, oh, kt)
        else:
            ka0 = ka1 = kt
        vt = vt_ref[j]
        m0, l0 = one_head(ka0, qa0, b0_ref[step], vt, m0, l0, acc0_ref)
        m1, l1 = one_head(ka1, qa1, b1_ref[step], vt, m1, l1, acc1_ref)
        return m0, l0, m1, l1

    init = (jnp.full((1, tile), MASK_NEG, F32), jnp.zeros((1, tile), F32),
            jnp.full((1, tile), MASK_NEG, F32), jnp.zeros((1, tile), F32))
    m0, l0, m1, l1 = lax.fori_loop(0, n_steps, body, init)
    row_o = lax.broadcasted_iota(jnp.int32, (LANES, tile), 0)
    top = row_o < HEAD_DIM
    o_t = jnp.where(top, acc0_ref[...] * (1.0 / l0), acc1_ref[...] * (1.0 / l1))
    o_ref[...] = o_t.T.astype(o_ref.dtype)
    if with_lse:
        lse_t = jnp.where(top, m0 + jnp.log(l0), m1 + jnp.log(l1))
        lse_ref[...] = lse_t.T


def _flash(q_arr, k_arr, vt_arr, bias, *, batch, length, dil, tile, n_back, n_pairs, q_off, k_off,
           q_blocks, k_blocks, per_pair_kv, head0, head1, out_blocks, selb=None, onehot=None, with_lse=False):
    nq = length // tile
    n_kvp = n_pairs if per_pair_kv else 1
    with_sel = selb is not None
    n_delta = bias.shape[1]

    def bdec(br):
        return br // dil, br % dil

    def q_map(p, br, i):
        b, r = bdec(br)
        return (b, i, r * q_blocks + q_off + p)

    def k_map(p, br, i):
        b, r = bdec(br)
        return (b, 0, r * k_blocks + k_off + (p if per_pair_kv else 0))

    def vt_map(p, br, i):
        return (br * n_kvp + (p if per_pair_kv else 0), 0, 0, 0)

    def o_map(p, br, i):
        b, r = bdec(br)
        return (b, i, r * out_blocks + p)

    in_specs = [pl.BlockSpec((None, tile, LANES), q_map),
                pl.BlockSpec((None, length, LANES), k_map),
                pl.BlockSpec((None, nq, LANES, tile), vt_map),
                pl.BlockSpec((None, n_delta, tile, tile), lambda p, br, i: (head0(p), 0, 0, 0)),
                pl.BlockSpec((None, n_delta, tile, tile), lambda p, br, i: (head1(p), 0, 0, 0))]
    args = [q_arr, k_arr, vt_arr, bias, bias]
    if with_sel:
        in_specs += [pl.BlockSpec((None, tile, 2 * LANES), lambda p, br, i: (br, i, 0)),
                     pl.BlockSpec(onehot.shape, lambda p, br, i: (0, 0))]
        args += [selb, onehot]
    out_cols = dil * out_blocks * LANES
    out_specs = [pl.BlockSpec((None, tile, LANES), o_map)]
    out_shape = [jax.ShapeDtypeStruct((batch, length, out_cols), BF16)]
    if with_lse:
        out_specs.append(pl.BlockSpec((None, tile, LANES), o_map))
        out_shape.append(jax.ShapeDtypeStruct((batch, length, out_cols), F32))
    res = pl.pallas_call(
        functools.partial(_flash_kernel, tile=tile, n_back=n_back, with_sel=with_sel, with_lse=with_lse),
        grid=(n_pairs, batch * dil, nq),
        in_specs=in_specs,
        out_specs=out_specs,
        out_shape=out_shape,
        scratch_shapes=[pltpu.VMEM((LANES, tile), F32), pltpu.VMEM((LANES, tile), F32)],
        compiler_params=_cparams(("arbitrary", "arbitrary", "arbitrary")),
        name="flash_sel" if with_sel else ("flash_dil" if with_lse else "flash_win"),
    )(*args)
    return res


def _transpose_values(v, batch, length, dil, n_pairs, tile):
    v6 = v.reshape(batch, length // tile, tile, dil, n_pairs, LANES)
    return v6.transpose(0, 3, 4, 1, 5, 2).reshape(batch * dil * n_pairs, length // tile, LANES, tile)


def _mix_kernel(x_ref, g1_ref, oc_ref, os_ref, ow_ref, gl_ref, ge_ref,
                od0_ref, od1_ref, od2_ref, l0_ref, l1_ref, l2_ref, ml_ref,
                wn_ref, wd_ref, wo_ref, o_ref):
    d = x_ref.shape[1]
    sig = 1.0 / (1.0 + jnp.exp(-gl_ref[...]))
    hi = sig.astype(BF16)
    lo = (sig - hi.astype(F32)).astype(BF16)
    ge = ge_ref[...]
    gates = jnp.dot(hi, ge, preferred_element_type=F32) + jnp.dot(lo, ge, preferred_element_type=F32)
    o_nsa = (gates[:, 0:512] * oc_ref[...].astype(F32)
             + gates[:, 512:1024] * os_ref[...].astype(F32)
             + gates[:, 1024:1536] * ow_ref[...].astype(F32))
    u_nsa = jnp.dot(o_nsa.astype(BF16), wn_ref[...], preferred_element_type=F32)

    lses = (l0_ref[...], l1_ref[...], l2_ref[...])
    mx = jnp.maximum(jnp.maximum(lses[0], lses[1]), lses[2])
    es = [jnp.exp(l - mx) for l in lses]
    inv = 1.0 / (es[0] + es[1] + es[2])
    o_dil = (es[0] * od0_ref[...].astype(F32) + es[1] * od1_ref[...].astype(F32)
             + es[2] * od2_ref[...].astype(F32)) * inv
    u_dil = jnp.dot(o_dil.astype(BF16), wd_ref[...], preferred_element_type=F32)

    gm = 1.0 / (1.0 + jnp.exp(-ml_ref[...].astype(F32)))
    merged = gm[:, :d] * u_nsa + gm[:, d:] * u_dil
    y = jnp.dot(merged.astype(BF16), wo_ref[...], preferred_element_type=F32)
    o_ref[...] = x_ref[...] + g1_ref[...] * y


def _mix(x2, g1, oc, osel, ow, gl, ge, od, lse, ml, wn, wd, wo, seq):
    t, d = x2.shape
    tm = TM_PROJ
    per_b = seq // tm
    row = lambda i: (i, 0)
    const = lambda i: (0, 0)
    rows = lambda a: pl.BlockSpec((tm, a.shape[1]), row)
    full = lambda a: pl.BlockSpec(a.shape, const)
    return pl.pallas_call(
        _mix_kernel,
        grid=(t // tm,),
        in_specs=[rows(x2), pl.BlockSpec((None, 1, d), lambda i: (i // per_b, 0, 0)),
                  rows(oc), rows(osel), rows(ow), rows(gl), full(ge),
                  rows(od[0]), rows(od[1]), rows(od[2]), rows(lse[0]), rows(lse[1]), rows(lse[2]), rows(ml),
                  full(wn), full(wd), full(wo)],
        out_specs=pl.BlockSpec((tm, d), row),
        out_shape=jax.ShapeDtypeStruct((t, d), F32),
        compiler_params=_cparams(("arbitrary",)),
        name="mix_outproj",
    )(x2, g1, oc, osel, ow, gl, ge, *od, *lse, ml, wn, wd, wo)


def _router_kernel(x_ref, sc_ref, sh_ref, g_ref, wr_ref, br_ref, h_ref, eid_ref, wts_ref):
    h = _norm_mod(x_ref[...], g_ref[...], sc_ref[...], sh_ref[...])
    h_ref[...] = h
    logit = lax.dot_general(wr_ref[...], h, (((1,), (1,)), ((), ())), preferred_element_type=F32,
                            precision=lax.Precision.HIGHEST) + br_ref[...]
    grp = jnp.zeros((1, h.shape[0]), jnp.int32)
    best = logit[0:1]
    for k in range(1, N_EXPERT_GROUPS):
        better = logit[k:k + 1] > best
        grp = jnp.where(better, k, grp)
        best = jnp.where(better, logit[k:k + 1], best)
    den = jnp.zeros_like(best)
    for k in range(N_EXPERT_GROUPS):
        den = den + jnp.exp(logit[k:k + 1] - best)
    p_grp = 1.0 / den
    le = logit[SUBLANES:SUBLANES + EXPERTS_PER_GROUP]
    for k in range(1, N_EXPERT_GROUPS):
        lo = SUBLANES + k * EXPERTS_PER_GROUP
        le = jnp.where(grp == k, logit[lo:lo + EXPERTS_PER_GROUP], le)
    rowi = lax.broadcasted_iota(jnp.int32, le.shape, 0)
    v1 = jnp.max(le, axis=0, keepdims=True)
    i1 = jnp.min(jnp.where(le == v1, rowi, EXPERTS_PER_GROUP), axis=0, keepdims=True)
    rest = jnp.where(rowi == i1, -3e38, le)
    v2 = jnp.max(rest, axis=0, keepdims=True)
    i2 = jnp.min(jnp.where(rest == v2, rowi, EXPERTS_PER_GROUP), axis=0, keepdims=True)
    e2 = jnp.exp(v2 - v1)
    inv = p_grp / (1.0 + e2)
    eid_ref[...] = jnp.concatenate([grp * EXPERTS_PER_GROUP + i1, grp * EXPERTS_PER_GROUP + i2], axis=0)
    wts_ref[...] = jnp.concatenate([inv, e2 * inv], axis=0)


def _router(x2, sc, sh, g, wr_t, br, seq):
    t, d = x2.shape
    tm = TM_PROJ
    per_b = seq // tm
    row = lambda i: (i, 0)
    const = lambda i: (0, 0)
    return pl.pallas_call(
        _router_kernel,
        grid=(t // tm,),
        in_specs=[pl.BlockSpec((tm, d), row),
                  pl.BlockSpec((None, 1, d), lambda i: (i // per_b, 0, 0)),
                  pl.BlockSpec((None, 1, d), lambda i: (i // per_b, 0, 0)),
                  pl.BlockSpec((1, d), const),
                  pl.BlockSpec(wr_t.shape, const),
                  pl.BlockSpec(br.shape, const)],
        out_specs=[pl.BlockSpec((tm, d), row),
                   pl.BlockSpec((2, tm), lambda i: (0, i)),
                   pl.BlockSpec((2, tm), lambda i: (0, i))],
        out_shape=[jax.ShapeDtypeStruct((t, d), F32),
                   jax.ShapeDtypeStruct((2, t), jnp.int32),
                   jax.ShapeDtypeStruct((2, t), F32)],
        compiler_params=_cparams(("arbitrary",)),
        name="norm_router",
    )(x2, sc, sh, g, wr_t, br)


def _moe_kernel(cnt_ref, off_ref, tok_ref, wt_ref, h_ref, w1_ref, w3_ref, w2_ref, o_ref,
                xs_ref, xb_ref, y3_ref):
    c = pl.program_id(0)
    e = pl.program_id(1)
    rb = xs_ref.shape[0]
    n_sub = xs_ref.shape[1]

    @pl.when((c == 0) & (e == 0))
    def _():
        xs_ref[...] = jnp.zeros_like(xs_ref)

    @pl.when(e == 0)
    def _():
        o_ref[...] = jnp.zeros_like(o_ref)

    n = cnt_ref[c, e]
    off = off_ref[c, e]

    def block(bi, _):
        base = off + bi * rb
        rows = jnp.minimum(rb, n - bi * rb)

        def gather(r, _):
            xs_ref[r] = h_ref[tok_ref[0, base + r]]
            return 0

        lax.fori_loop(0, rows, gather, 0)
        for j in range(n_sub):
            xb_ref[:, j * LANES:(j + 1) * LANES] = xs_ref[:, j, :].astype(BF16)
        xb = xb_ref[...]
        a = jnp.dot(xb, w1_ref[...], preferred_element_type=F32)
        b = jnp.dot(xb, w3_ref[...], preferred_element_type=F32)
        mid = (a * (1.0 / (1.0 + jnp.exp(-a))) * b).astype(BF16)
        y = jnp.dot(mid, w2_ref[...], preferred_element_type=F32)
        for j in range(n_sub):
            y3_ref[:, j, :] = y[:, j * LANES:(j + 1) * LANES]

        def scatter(r, _):
            tkn = tok_ref[0, base + r]
            o_ref[tkn] = o_ref[tkn] + wt_ref[0, base + r] * y3_ref[r]
            return 0

        lax.fori_loop(0, rows, scatter, 0)
        return 0

    lax.fori_loop(0, (n + rb - 1) // rb, block, 0)


def _moe(h2, eid, wts, w1, w3, w2):
    t, d = h2.shape
    tc = min(MOE_CHUNK, t)
    n_chunks = t // tc
    n_sub = d // LANES
    slots = 2 * tc
    tok = jnp.arange(t, dtype=jnp.int32)
    key = ((tok // tc)[None, :] * N_EXPERTS + eid).reshape(-1)
    order = jnp.argsort(key)
    tok_sorted = (jnp.tile(tok % tc, 2)[order]).reshape(n_chunks, 1, slots)
    w_sorted = wts.reshape(-1)[order].reshape(n_chunks, 1, slots)
    counts = jnp.zeros((n_chunks * N_EXPERTS,), jnp.int32).at[key].add(1).reshape(n_chunks, N_EXPERTS)
    offs = jnp.cumsum(counts, axis=1) - counts

    grid_spec = pltpu.PrefetchScalarGridSpec(
        num_scalar_prefetch=2,
        grid=(n_chunks, N_EXPERTS),
        in_specs=[pl.BlockSpec((None, 1, slots), lambda c, e, *_: (c, 0, 0), memory_space=pltpu.SMEM),
                  pl.BlockSpec((None, 1, slots), lambda c, e, *_: (c, 0, 0), memory_space=pltpu.SMEM),
                  pl.BlockSpec((tc, n_sub, LANES), lambda c, e, *_: (c, 0, 0)),
                  pl.BlockSpec((None, d, D_EXPERT), lambda c, e, *_: (e, 0, 0)),
                  pl.BlockSpec((None, d, D_EXPERT), lambda c, e, *_: (e, 0, 0)),
                  pl.BlockSpec((None, D_EXPERT, d), lambda c, e, *_: (e, 0, 0))],
        out_specs=pl.BlockSpec((tc, n_sub, LANES), lambda c, e, *_: (c, 0, 0)),
        scratch_shapes=[pltpu.VMEM((MOE_ROWS, n_sub, LANES), F32),
                        pltpu.VMEM((MOE_ROWS, d), BF16),
                        pltpu.VMEM((MOE_ROWS, n_sub, LANES), F32)],
    )
    out = pl.pallas_call(
        _moe_kernel,
        grid_spec=grid_spec,
        out_shape=jax.ShapeDtypeStruct((t, n_sub, LANES), F32),
        compiler_params=_cparams(("arbitrary", "arbitrary")),
        name="moe_experts",
    )(counts, offs, tok_sorted, w_sorted, h2.reshape(t, n_sub, LANES), w1, w3, w2)
    return out.reshape(t, d)


def _resid_kernel(x_ref, y_ref, g_ref, nf_ref, o_ref, *, final):
    x = x_ref[...] + g_ref[...] * y_ref[...]
    if final:
        ms = jnp.mean(x * x, axis=-1, keepdims=True)
        x = x * lax.rsqrt(ms + RMS_EPS) * nf_ref[...]
    o_ref[...] = x


def _residual(x2, y2, g2, norm_f, seq, final):
    t, d = x2.shape
    tm = TM_PROJ
    per_b = seq // tm
    row = lambda i: (i, 0)
    return pl.pallas_call(
        functools.partial(_resid_kernel, final=final),
        grid=(t // tm,),
        in_specs=[pl.BlockSpec((tm, d), row), pl.BlockSpec((tm, d), row),
                  pl.BlockSpec((None, 1, d), lambda i: (i // per_b, 0, 0)),
                  pl.BlockSpec((1, d), lambda i: (0, 0))],
        out_specs=pl.BlockSpec((tm, d), row),
        out_shape=jax.ShapeDtypeStruct((t, d), F32),
        compiler_params=_cparams(("arbitrary",)),
        name="residual_final" if final else "residual",
    )(x2, y2, g2, norm_f)


def _split_w_in(w_in, d):
    scale = HEAD_DIM ** -0.5
    nq = NSA_HEADS * HEAD_DIM
    nkv = 3 * 2 * NSA_KV_HEADS * HEAD_DIM
    ngate = 3 * NSA_HEADS
    ndil = 3 * N_DIL_GROUPS * DIL_HEADS_PER_GROUP * HEAD_DIM
    o1, o2, o3 = nq, nq + nkv, nq + nkv + ngate
    o4 = o3 + ndil
    wq = (w_in[:, :o1] * scale).reshape(d, NSA_KV_HEADS, NSA_GROUP, HEAD_DIM)
    wq = wq.transpose(0, 2, 1, 3).reshape(d, nq)
    wkv = w_in[:, o1:o2]
    wg = jnp.pad(w_in[:, o2:o3], ((0, 0), (0, LANES - ngate)))
    gw = DIL_HEADS_PER_GROUP * HEAD_DIM
    per_which = N_DIL_GROUPS * gw
    wds = []
    for grp in range(N_DIL_GROUPS):
        parts = [w_in[:, o3 + which * per_which + grp * gw: o3 + which * per_which + (grp + 1) * gw]
                 for which in range(3)]
        parts[0] = parts[0] * scale
        wds.append(jnp.concatenate(parts, axis=1))
    wm = w_in[:, o4:]
    cast = lambda w: w.astype(BF16)
    return [cast(wq), cast(wkv), cast(wds[0]), cast(wds[1]), cast(wds[2]), cast(wm), cast(wg)]


def kernel(x, c, rel_bias, ada_w, ada_b, norm1, norm2, w_in, cmp_pos, cmp_w1, cmp_w2, w_up_nsa, w_up_dil, w_o,
           router_wg, router_bg, router_we, router_be, exp_w1, exp_w3, exp_w2, norm_f):
    batch, seq, d = x.shape
    depth = ada_w.shape[0]
    t = batch * seq
    n_cmp = seq // CMP_STRIDE
    n_sel = seq // SEL_BLOCK
    assert seq % TQ_NSA == 0 and n_sel <= HEAD_DIM and n_sel >= SEL_TOPN
    assert all(seq % (dil * TQ_DIL) == 0 for _, dil in DIL_PAIRS)

    mod = _modulation(c, ada_w, ada_b)

    nq_nsa = seq // TQ_NSA
    bias_cmp = _expand_bias(rel_bias, _cmp_buckets(seq, n_cmp), 0, NSA_HEADS, 8 * SUBLANES)
    sel_b = _toeplitz_buckets(nq_nsa, TQ_NSA, seq, 1).reshape(nq_nsa * TQ_NSA, TQ_NSA)
    bias_sel = _expand_bias(rel_bias, sel_b, 0, NSA_HEADS, TQ_NSA).reshape(NSA_HEADS, nq_nsa, TQ_NSA, TQ_NSA)
    nb_win = min(-(-NSA_WINDOW // TQ_NSA), nq_nsa - 1)
    win_b = _toeplitz_buckets(nb_win + 1, TQ_NSA, NSA_WINDOW, 1).reshape(-1, TQ_NSA)
    bias_win = _expand_bias(rel_bias, win_b, 0, NSA_HEADS, TQ_NSA).reshape(NSA_HEADS, nb_win + 1, TQ_NSA, TQ_NSA)
    bias_dil, nb_dil = [], []
    for grp, (window, dil) in enumerate(DIL_PAIRS):
        nb = min(-(-(window // dil) // TQ_DIL), seq // dil // TQ_DIL - 1)
        bk = _toeplitz_buckets(nb + 1, TQ_DIL, window // dil, dil).reshape(-1, TQ_DIL)
        hb = NSA_HEADS + grp * DIL_HEADS_PER_GROUP
        bias_dil.append(_expand_bias(rel_bias, bk, hb, DIL_HEADS_PER_GROUP, TQ_DIL)
                        .reshape(DIL_HEADS_PER_GROUP, nb + 1, TQ_DIL, TQ_DIL))
        nb_dil.append(nb)
    ovl_t = jnp.asarray(_overlap_t(n_cmp, n_sel), BF16)
    onehot = jnp.asarray(_block_onehot(seq), BF16)
    gate_e = jnp.asarray(_gate_expand(), BF16)

    x2 = x.reshape(t, d)
    for l in range(depth):
        sh1, sc1, g1, sh2, sc2, g2 = [m.reshape(batch, 1, d) for m in jnp.split(mod[l], 6, axis=-1)]
        weights = _split_w_in(w_in[l], d)
        (q_n, kc_in, vc_in, k_sel, v_sel, k_win, v_win, qkv_d0, qkv_d1, qkv_d2, merge_l, gate_l) = _inproj(
            x2, sc1, sh1, norm1[l].reshape(1, d), weights, seq)

        kc = _compress(kc_in, cmp_pos[l, 0], cmp_w1[l, 0], cmp_w2[l, 0], batch, seq, transpose_out=False)
        vct = _compress(vc_in, cmp_pos[l, 1], cmp_w1[l, 1], cmp_w2[l, 1], batch, seq, transpose_out=True)
        o_c, selb = _cmp_select(q_n, kc, vct, bias_cmp, ovl_t, batch, seq)
        nsa_common = dict(batch=batch, length=seq, dil=1, tile=TQ_NSA, n_pairs=NSA_GROUP, q_off=0, k_off=0,
                          q_blocks=NSA_GROUP, k_blocks=1, per_pair_kv=False,
                          head0=lambda p: p, head1=lambda p: NSA_GROUP + p, out_blocks=NSA_GROUP)
        q3 = q_n.reshape(batch, seq, 512)
        (o_s,) = _flash(q3, k_sel.reshape(batch, seq, LANES), _transpose_values(v_sel, batch, seq, 1, 1, TQ_NSA),
                        bias_sel, n_back=None, selb=selb.reshape(batch, seq, 2 * LANES), onehot=onehot,
                        **nsa_common)
        (o_w,) = _flash(q3, k_win.reshape(batch, seq, LANES), _transpose_values(v_win, batch, seq, 1, 1, TQ_NSA),
                        bias_win, n_back=nb_win, **nsa_common)

        o_d, lse_d = [], []
        for grp, ((window, dil), qkv) in enumerate(zip(DIL_PAIRS, (qkv_d0, qkv_d1, qkv_d2))):
            length = seq // dil
            view = qkv.reshape(batch, length, dil * 768)
            vt = _transpose_values(qkv[:, 512:], batch, length, dil, 2, TQ_DIL)
            o_g, lse_g = _flash(view, view, vt, bias_dil[grp], batch=batch, length=length, dil=dil, tile=TQ_DIL,
                                n_back=nb_dil[grp], n_pairs=2, q_off=0, k_off=2, q_blocks=6, k_blocks=6,
                                per_pair_kv=True, head0=lambda p: 2 * p, head1=lambda p: 2 * p + 1,
                                out_blocks=2, with_lse=True)
            o_d.append(o_g.reshape(t, 256))
            lse_d.append(lse_g.reshape(t, 256))

        wn = w_up_nsa[l].reshape(NSA_KV_HEADS, NSA_GROUP, HEAD_DIM, d).transpose(1, 0, 2, 3).reshape(512, d)
        x2 = _mix(x2, g1, o_c, o_s.reshape(t, 512), o_w.reshape(t, 512), gate_l, gate_e, o_d, lse_d, merge_l,
                  wn.astype(BF16), w_up_dil[l].astype(BF16), w_o[l].astype(BF16), seq)

        wr_t = jnp.concatenate([jnp.pad(router_wg[l], ((0, 0), (0, SUBLANES - N_EXPERT_GROUPS))),
                                router_we[l]], axis=1).T
        br = jnp.concatenate([jnp.pad(router_bg[l], (0, SUBLANES - N_EXPERT_GROUPS)),
                              router_be[l]]).reshape(-1, 1)
        h2, eid, wts = _router(x2, sc2, sh2, norm2[l].reshape(1, d), wr_t, br, seq)
        y = _moe(h2, eid, wts, exp_w1[l].astype(BF16), exp_w3[l].astype(BF16), exp_w2[l].astype(BF16))
        x2 = _residual(x2, y, g2, norm_f.reshape(1, d), seq, final=(l == depth - 1))
    return x2.reshape(batch, seq, d)
```

```python
import functools
import math

import numpy as np
import jax
import jax.numpy as jnp
from jax import lax
from jax.experimental import pallas as pl
from jax.experimental.pallas import tpu as pltpu

F32 = jnp.float32
BF16 = jnp.bfloat16

HEAD_DIM = 64
NSA_HEADS = 8
NSA_KV_HEADS = 2
NSA_GROUP = NSA_HEADS // NSA_KV_HEADS
CMP_LEN = 32
CMP_STRIDE = 16
CMP_HIDDEN = 256
SEL_BLOCK = 64
SEL_TOPN = 16
NSA_WINDOW = 512
FORCE_SCORE = 1e4
DIL_PAIRS = ((128, 1), (512, 4), (2048, 16))
N_DIL_GROUPS = 3
DIL_HEADS_PER_GROUP = 4
NUM_BUCKETS = 32
REL_MAX_DIST = 2048
N_EXPERT_GROUPS = 4
EXPERTS_PER_GROUP = 8
N_EXPERTS = N_EXPERT_GROUPS * EXPERTS_PER_GROUP
D_EXPERT = 512
RMS_EPS = 1e-6

LOG2E = math.log2(math.e)
LANES = 128
SUBLANES = 8
MASK_NEG = -1e30
SEL_NEG = -1e9
TQ_NSA = 256
TQ_DIL = 128
TM_PROJ = 512
MOE_CHUNK = 2048
MOE_ROWS = 128
MOE_UNROLL = 8
VMEM_LIMIT = 56 * 1024 * 1024


def _cparams(sem):
    return pltpu.CompilerParams(dimension_semantics=sem, vmem_limit_bytes=VMEM_LIMIT)


def _np_bucket(dist):
    dist = np.maximum(dist, 0)
    max_exact = NUM_BUCKETS // 2
    df = np.maximum(dist, 1).astype(np.float32)
    val = np.log(df / np.float32(max_exact)) / np.float32(math.log(REL_MAX_DIST / max_exact))
    large = max_exact + (val * np.float32(NUM_BUCKETS - max_exact)).astype(np.int32)
    large = np.minimum(large, NUM_BUCKETS - 1)
    return np.where(dist < max_exact, dist, large).astype(np.int32)


def _toeplitz_buckets(n_delta, tile, window, dist_scale):
    dd = np.arange(n_delta)[:, None, None]
    r = np.arange(tile)[None, :, None]
    c = np.arange(tile)[None, None, :]
    dist = dd * tile + c - r
    valid = (dist >= 0) & (dist <= window)
    return np.where(valid, _np_bucket(dist * dist_scale), -1).astype(np.int32)


def _cmp_buckets(seq, n_cmp):
    c_end = np.arange(n_cmp)[:, None] * CMP_STRIDE + CMP_LEN - 1
    t = np.arange(seq)[None, :]
    dist = t - c_end
    return np.where(dist >= 0, _np_bucket(dist), -1).astype(np.int32)


def _overlap_t(n_cmp, n_sel):
    c_start = np.arange(n_cmp)[None, :] * CMP_STRIDE
    s_start = np.arange(n_sel)[:, None] * SEL_BLOCK
    ov = np.clip(np.minimum(c_start + CMP_LEN, s_start + SEL_BLOCK) - np.maximum(c_start, s_start), 0, None)
    out = np.zeros((LANES, n_cmp), np.float32)
    out[64:64 + n_sel] = ov.astype(np.float32) / CMP_STRIDE
    return out


def _block_onehot(seq):
    oh = np.zeros((seq, LANES), np.float32)
    blk = np.arange(seq) // SEL_BLOCK
    oh[np.arange(seq), blk] = 1.0
    oh[np.arange(seq), 64 + blk] = 1.0
    return oh


def _gate_expand():
    e = np.zeros((LANES, 3 * NSA_HEADS * HEAD_DIM), np.float32)
    for br in range(3):
        for g in range(NSA_GROUP):
            for ln in range(LANES):
                kv = ln // HEAD_DIM
                e[br * NSA_HEADS + kv * NSA_GROUP + g, br * 512 + g * LANES + ln] = 1.0
    return e


def _mod_kernel(c_ref, w_ref, b_ref, o_ref):
    c = c_ref[...]
    cond = c * (1.0 / (1.0 + jnp.exp(-c)))
    o_ref[...] = jnp.dot(cond, w_ref[...], preferred_element_type=F32,
                         precision=lax.Precision.HIGHEST) + b_ref[...]


def _modulation(c, ada_w, ada_b):
    depth, d, n = ada_w.shape
    b = c.shape[0]
    tn = 1536
    return pl.pallas_call(
        _mod_kernel,
        grid=(depth, n // tn),
        in_specs=[pl.BlockSpec((b, d), lambda l, j: (0, 0)),
                  pl.BlockSpec((None, d, tn), lambda l, j: (l, 0, j)),
                  pl.BlockSpec((None, 1, tn), lambda l, j: (l, 0, j))],
        out_specs=pl.BlockSpec((None, b, tn), lambda l, j: (l, 0, j)),
        out_shape=jax.ShapeDtypeStruct((depth, b, n), F32),
        compiler_params=_cparams(("arbitrary", "arbitrary")),
        name="adaln_modulation",
    )(c, ada_w, ada_b.reshape(depth, 1, n))


def _bias_kernel(tbl_ref, bkt_ref, o_ref, *, head_base):
    h = pl.program_id(0) + head_base
    bkt = bkt_ref[...]
    acc = jnp.full(bkt.shape, MASK_NEG, F32)
    for b in range(NUM_BUCKETS):
        acc = jnp.where(bkt == b, tbl_ref[b, h] * LOG2E, acc)
    o_ref[...] = acc


def _expand_bias(rel_bias, buckets, head_base, n_heads, row_tile):
    rows, cols = buckets.shape
    return pl.pallas_call(
        functools.partial(_bias_kernel, head_base=head_base),
        grid=(n_heads, rows // row_tile),
        in_specs=[pl.BlockSpec(memory_space=pltpu.SMEM),
                  pl.BlockSpec((row_tile, cols), lambda h, i: (i, 0))],
        out_specs=pl.BlockSpec((None, row_tile, cols), lambda h, i: (h, i, 0)),
        out_shape=jax.ShapeDtypeStruct((n_heads, rows, cols), F32),
        compiler_params=_cparams(("arbitrary", "arbitrary")),
        name="bias_expand",
    )(rel_bias, jnp.asarray(buckets))


def _norm_mod(x, g, sc, sh):
    ms = jnp.mean(x * x, axis=-1, keepdims=True)
    y = x * lax.rsqrt(ms + RMS_EPS) * g
    return y * (1.0 + sc) + sh


def _inproj_kernel(x_ref, sc_ref, sh_ref, g_ref, oh_ref, wq_ref, wkv_ref, wd0_ref, wd1_ref, wd2_ref, wm_ref, wg_ref,
                   q_ref, kc_ref, vc_ref, ks_ref, vs_ref, kw_ref, vw_ref, d0_ref, d1_ref, d2_ref, m_ref, gl_ref):
    h = _norm_mod(x_ref[...], g_ref[...], sc_ref[...], sh_ref[...]).astype(BF16)

    def proj(w_ref):
        return jnp.dot(h, w_ref[...], preferred_element_type=F32)

    q_ref[...] = proj(wq_ref).astype(BF16)
    kv = proj(wkv_ref).astype(BF16)
    for k, ref in ((0, kc_ref), (1, vc_ref), (3, vs_ref), (4, kw_ref), (5, vw_ref)):
        ref[...] = kv[:, k * LANES:(k + 1) * LANES]
    k_sel = kv[:, 2 * LANES:3 * LANES]
    oh = oh_ref[...]
    lo = lax.broadcasted_iota(jnp.int32, k_sel.shape, 1) < HEAD_DIM
    ks_ref[:, :LANES] = jnp.where(lo, k_sel, oh)
    ks_ref[:, LANES:] = jnp.where(lo, oh, k_sel)
    d0_ref[...] = proj(wd0_ref).astype(BF16)
    d1_ref[...] = proj(wd1_ref).astype(BF16)
    d2_ref[...] = proj(wd2_ref).astype(BF16)
    m_ref[...] = proj(wm_ref).astype(BF16)
    gl_ref[...] = proj(wg_ref)


def _inproj(x2, sc, sh, g, onehot, weights, seq):
    t, d = x2.shape
    tm = TM_PROJ
    per_b = seq // tm
    widths = (512, 128, 128, 256, 128, 128, 128, 768, 768, 768, 2 * d, LANES)
    dtypes = (BF16,) * 11 + (F32,)
    row = lambda i: (i, 0)
    const = lambda i: (0, 0)
    in_specs = [pl.BlockSpec((tm, d), row),
                pl.BlockSpec((None, 1, d), lambda i: (i // per_b, 0, 0)),
                pl.BlockSpec((None, 1, d), lambda i: (i // per_b, 0, 0)),
                pl.BlockSpec((1, d), const),
                pl.BlockSpec((tm, LANES), lambda i: (i % per_b, 0))]
    in_specs += [pl.BlockSpec(w.shape, const) for w in weights]
    return pl.pallas_call(
        _inproj_kernel,
        grid=(t // tm,),
        in_specs=in_specs,
        out_specs=[pl.BlockSpec((tm, w), row) for w in widths],
        out_shape=[jax.ShapeDtypeStruct((t, w), dt) for w, dt in zip(widths, dtypes)],
        compiler_params=_cparams(("arbitrary",)),
        name="norm_inproj",
    )(x2, sc, sh, g, onehot, *weights)


def _gelu_tanh(x):
    return 0.5 * x * (1.0 + jnp.tanh(math.sqrt(2.0 / math.pi) * (x + 0.044715 * (x * x * x))))


def _compress_kernel(r_ref, pt_ref, pb_ref, wt_ref, wb_ref, w2_ref, o_ref, *, transpose_out):
    r = r_ref[...].astype(F32)
    top = jnp.dot((r + pt_ref[...]).astype(BF16), wt_ref[...], preferred_element_type=F32)
    bot = jnp.dot((r + pb_ref[...]).astype(BF16), wb_ref[...], preferred_element_type=F32)
    n = bot.shape[0]
    hid = top + pltpu.roll(bot, n - 1, 0)
    act = _gelu_tanh(hid).astype(BF16)
    if transpose_out:
        o_ref[...] = lax.dot_general(w2_ref[...], act, (((1,), (1,)), ((), ())),
                                     preferred_element_type=F32).astype(BF16)
    else:
        o_ref[...] = jnp.dot(act, w2_ref[...], preferred_element_type=F32).astype(BF16)


def _compress(tok, pos, w1, w2, batch, seq, transpose_out):
    nc = seq // CMP_STRIDE
    half = CMP_LEN // 2
    eye = jnp.eye(NSA_KV_HEADS, dtype=F32)
    w1r = w1.reshape(CMP_LEN, HEAD_DIM, CMP_HIDDEN)
    blk = lambda w: jnp.einsum('ldn,hg->lhdgn', w, eye).reshape(half * LANES, 2 * CMP_HIDDEN).astype(BF16)
    wt, wb = blk(w1r[:half]), blk(w1r[half:])
    posr = lambda p: jnp.broadcast_to(p[:, None, :], (half, NSA_KV_HEADS, HEAD_DIM)).reshape(1, half * LANES)
    pt, pb = posr(pos[:half]), posr(pos[half:])
    w2b = jnp.einsum('nd,hg->hngd', w2, eye).reshape(2 * CMP_HIDDEN, LANES)
    if transpose_out:
        w2b = w2b.T
        out_block, out_shape = (None, LANES, nc), (batch, LANES, nc)
    else:
        out_block, out_shape = (None, nc, LANES), (batch, nc, LANES)
    w2b = w2b.astype(BF16)
    const = lambda b: (0, 0)
    return pl.pallas_call(
        functools.partial(_compress_kernel, transpose_out=transpose_out),
        grid=(batch,),
        in_specs=[pl.BlockSpec((None, nc, half * LANES), lambda b: (b, 0, 0)),
                  pl.BlockSpec(pt.shape, const), pl.BlockSpec(pb.shape, const),
                  pl.BlockSpec(wt.shape, const), pl.BlockSpec(wb.shape, const),
                  pl.BlockSpec(w2b.shape, const)],
        out_specs=pl.BlockSpec(out_block, lambda b: (b, 0, 0)),
        out_shape=jax.ShapeDtypeStruct(out_shape, BF16),
        compiler_params=_cparams(("arbitrary",)),
        name="nsa_compress",
    )(tok.reshape(batch, nc, half * LANES), pt, pb, wt, wb, w2b)


def _cmp_select_kernel(q_ref, kc_ref, vct_ref, bias_ref, ovl_ref, o_ref, selb_ref, *, n_sel, n_top):
    tq = q_ref.shape[0]
    nc = kc_ref.shape[0]
    qs = pl.program_id(1) * tq
    kc = kc_ref[...]
    vct = vct_ref[...]
    lane_q = lax.broadcasted_iota(jnp.int32, (tq, LANES), 1)
    row_o = lax.broadcasted_iota(jnp.int32, (LANES, tq), 0)
    pc_sum = [jnp.zeros((nc, tq), F32) for _ in range(NSA_KV_HEADS)]
    for g in range(NSA_GROUP):
        qt = q_ref[:, g * LANES:(g + 1) * LANES]
        outs = []
        for kv in range(NSA_KV_HEADS):
            mine = (lane_q < HEAD_DIM) if kv == 0 else (lane_q >= HEAD_DIM)
            qm = jnp.where(mine, qt, jnp.zeros_like(qt))
            s = lax.dot_general(kc, qm, (((1,), (1,)), ((), ())), preferred_element_type=F32)
            s = s + bias_ref[kv * NSA_GROUP + g]
            m = jnp.max(s, axis=0, keepdims=True)
            m = jnp.where(m < 0.5 * MASK_NEG, 0.0, m)
            p = jnp.exp2(s - m)
            den = jnp.sum(p, axis=0, keepdims=True)
            pc = p * (1.0 / jnp.where(den > 0.0, den, 1.0))
            pc_sum[kv] = pc_sum[kv] + pc
            outs.append(jnp.dot(vct, pc.astype(BF16), preferred_element_type=F32))
        o_t = jnp.where(row_o < HEAD_DIM, outs[0], outs[1])
        o_ref[:, g * LANES:(g + 1) * LANES] = o_t.T.astype(BF16)

    rowj = lax.broadcasted_iota(jnp.int32, (HEAD_DIM, tq), 0)
    t = qs + lax.broadcasted_iota(jnp.int32, (HEAD_DIM, tq), 1)
    jq = jnp.right_shift(t, SEL_BLOCK.bit_length() - 1)
    forced = (rowj == 0) | (rowj == jq) | (rowj == jq - 1)
    future = rowj > jq
    ovl = ovl_ref[...]
    for kv in range(NSA_KV_HEADS):
        hi = pc_sum[kv].astype(BF16)
        lo = (pc_sum[kv] - hi.astype(F32)).astype(BF16)
        imp = (jnp.dot(ovl, hi, preferred_element_type=F32)
               + jnp.dot(ovl, lo, preferred_element_type=F32))[HEAD_DIM:]
        score = jnp.where(forced, FORCE_SCORE, jnp.where(future, -1.0, imp))
        rem = jnp.where(rowj < n_sel, score, -3e38)
        sel = jnp.zeros((HEAD_DIM, tq), F32)
        for _ in range(n_top):
            m = jnp.max(rem, axis=0, keepdims=True)
            idx = jnp.min(jnp.where(rem == m, rowj, HEAD_DIM), axis=0, keepdims=True)
            pick = rowj == idx
            sel = jnp.where(pick, 1.0, sel)
            rem = jnp.where(pick, -3e38, rem)
        sb = jnp.where(sel > 0.5, 0.0, SEL_NEG)
        zero = jnp.zeros_like(sb)
        full = jnp.concatenate([zero, sb], axis=0) if kv == 0 else jnp.concatenate([sb, zero], axis=0)
        selb_ref[:, kv * LANES:(kv + 1) * LANES] = full.T.astype(BF16)


def _cmp_select(q, kc, vct, bias_c, ovl_t, batch, seq):
    t = q.shape[0]
    tq = TQ_NSA
    nq = seq // tq
    nc = seq // CMP_STRIDE
    n_sel = seq // SEL_BLOCK
    n_top = min(SEL_TOPN, n_sel)
    return pl.pallas_call(
        functools.partial(_cmp_select_kernel, n_sel=n_sel, n_top=n_top),
        grid=(batch, nq),
        in_specs=[pl.BlockSpec((tq, 512), lambda b, i: (b * nq + i, 0)),
                  pl.BlockSpec((None, nc, LANES), lambda b, i: (b, 0, 0)),
                  pl.BlockSpec((None, LANES, nc), lambda b, i: (b, 0, 0)),
                  pl.BlockSpec((NSA_HEADS, nc, tq), lambda b, i: (0, 0, i)),
                  pl.BlockSpec((LANES, nc), lambda b, i: (0, 0))],
        out_specs=[pl.BlockSpec((tq, 512), lambda b, i: (b * nq + i, 0)),
                   pl.BlockSpec((tq, 2 * LANES), lambda b, i: (b * nq + i, 0))],
        out_shape=[jax.ShapeDtypeStruct((t, 512), BF16),
                   jax.ShapeDtypeStruct((t, 2 * LANES), BF16)],
        compiler_params=_cparams(("arbitrary", "arbitrary")),
        name="nsa_cmp_select",
    )(q, kc, vct, bias_c, ovl_t)


def _flash_kernel(*refs, tile, n_pairs, heads, per_pair_kv, n_back, n_bias, with_sel, with_lse):
    it = iter(refs)
    q_ref, k_ref, vt_ref, b_ref = (next(it) for _ in range(4))
    selb_ref = next(it) if with_sel else None
    o_ref = next(it)
    lse_ref = next(it) if with_lse else None
    qa_ref, acc_ref = next(it), next(it)

    i = pl.program_id(1)
    n_heads = 2 * n_pairs
    lo = lax.broadcasted_iota(jnp.int32, (tile, LANES), 1) < HEAD_DIM
    for p in range(n_pairs):
        q = q_ref[:, p * LANES:(p + 1) * LANES]
        if with_sel:
            qa_ref[2 * p] = jnp.where(lo, q, selb_ref[:, :LANES])
            qa_ref[2 * p + 1] = jnp.where(lo, selb_ref[:, LANES:], q)
        else:
            zero = jnp.zeros_like(q)
            qa_ref[2 * p] = jnp.where(lo, q, zero)
            qa_ref[2 * p + 1] = jnp.where(lo, zero, q)
    acc_ref[...] = jnp.zeros_like(acc_ref)
    n_steps = i + 1 if n_back is None else jnp.minimum(i, n_back) + 1
    nt = (((1,), (1,)), ((), ()))

    def body(step, carry):
        j = i - step
        ks = pl.multiple_of(j * tile, tile)
        bidx = jnp.minimum(step, n_bias - 1)

        def scores(hd):
            p, half = divmod(hd, 2)
            kc = half if with_sel else (p if per_pair_kv else 0)
            kt = k_ref[pl.ds(ks, tile), kc * LANES:(kc + 1) * LANES]
            s = lax.dot_general(kt, qa_ref[hd], nt, preferred_element_type=F32)
            return s + b_ref[heads[p][half], bidx]

        out = []
        s_next = scores(0)
        for hd in range(n_heads):
            s = s_next
            if hd + 1 < n_heads:
                s_next = scores(hd + 1)
            m, l = carry[2 * hd], carry[2 * hd + 1]
            m_new = jnp.maximum(m, jnp.max(s, axis=0, keepdims=True))
            alpha = jnp.exp2(m - m_new)
            pr = jnp.exp2(s - m_new)
            l_new = alpha * l + jnp.sum(pr, axis=0, keepdims=True)
            vt = vt_ref[(hd // 2) if per_pair_kv else 0, j]
            acc_ref[hd] = alpha * acc_ref[hd] + jnp.dot(vt, pr.astype(BF16), preferred_element_type=F32)
            out += [m_new, l_new]
        return tuple(out)

    init = (jnp.full((1, tile), MASK_NEG, F32), jnp.zeros((1, tile), F32)) * n_heads
    fin = lax.fori_loop(0, n_steps, body, init)
    top = lax.broadcasted_iota(jnp.int32, (LANES, tile), 0) < HEAD_DIM
    for p in range(n_pairs):
        m0, l0, m1, l1 = fin[4 * p:4 * p + 4]
        o_t = jnp.where(top, acc_ref[2 * p] * (1.0 / l0), acc_ref[2 * p + 1] * (1.0 / l1))
        o_ref[:, p * LANES:(p + 1) * LANES] = o_t.T.astype(o_ref.dtype)
        if with_lse:
            lse_t = jnp.where(top, m0 + jnp.log2(l0), m1 + jnp.log2(l1))
            lse_ref[:, p * LANES:(p + 1) * LANES] = lse_t.T


def _flash(q_arr, k_arr, vt_arr, bias, *, batch, length, dil, tile, n_back, n_pairs, q_blk, k_blk, k_width,
           per_pair_kv, heads, selb=None, with_lse=False):
    nq = length // tile
    n_kvp = n_pairs if per_pair_kv else 1
    with_sel = selb is not None
    width = n_pairs * LANES
    vt5 = vt_arr.reshape(batch * dil, n_kvp, nq, LANES, tile)

    in_specs = [pl.BlockSpec((None, tile, width), lambda br, i: (br // dil, i, q_blk(br % dil))),
                pl.BlockSpec((None, length, k_width), lambda br, i: (br // dil, 0, k_blk(br % dil))),
                pl.BlockSpec((None, n_kvp, nq, LANES, tile), lambda br, i: (br, 0, 0, 0, 0)),
                pl.BlockSpec(bias.shape, lambda br, i: (0, 0, 0, 0))]
    args = [q_arr, k_arr, vt5, bias]
    if with_sel:
        in_specs.append(pl.BlockSpec((None, tile, 2 * LANES), lambda br, i: (br, i, 0)))
        args.append(selb)
    o_map = lambda br, i: (br // dil, i, br % dil)
    out_specs = [pl.BlockSpec((None, tile, width), o_map)]
    out_shape = [jax.ShapeDtypeStruct((batch, length, dil * width), BF16)]
    if with_lse:
        out_specs.append(pl.BlockSpec((None, tile, width), o_map))
        out_shape.append(jax.ShapeDtypeStruct((batch, length, dil * width), F32))
    return pl.pallas_call(
        functools.partial(_flash_kernel, tile=tile, n_pairs=n_pairs, heads=heads, per_pair_kv=per_pair_kv,
                          n_back=n_back, n_bias=bias.shape[1], with_sel=with_sel, with_lse=with_lse),
        grid=(batch * dil, nq),
        in_specs=in_specs,
        out_specs=out_specs,
        out_shape=out_shape,
        scratch_shapes=[pltpu.VMEM((2 * n_pairs, tile, LANES), BF16),
                        pltpu.VMEM((2 * n_pairs, LANES, tile), F32)],
        compiler_params=_cparams(("arbitrary", "arbitrary")),
        name="flash_sel" if with_sel else ("flash_dil" if with_lse else "flash_win"),
    )(*args)


def _transpose_values(v, batch, length, dil, n_pairs, tile):
    v6 = v.reshape(batch, length // tile, tile, dil, n_pairs, LANES)
    return v6.transpose(0, 3, 4, 1, 5, 2).reshape(batch * dil * n_pairs, length // tile, LANES, tile)


def _mix_kernel(x_ref, g1_ref, oc_ref, os_ref, ow_ref, gl_ref, ge_ref,
                od0_ref, od1_ref, od2_ref, l0_ref, l1_ref, l2_ref, ml_ref,
                wn_ref, wd_ref, wo_ref, o_ref):
    d = x_ref.shape[1]
    sig = 1.0 / (1.0 + jnp.exp(-gl_ref[...]))
    hi = sig.astype(BF16)
    lo = (sig - hi.astype(F32)).astype(BF16)
    ge = ge_ref[...]
    gates = jnp.dot(hi, ge, preferred_element_type=F32) + jnp.dot(lo, ge, preferred_element_type=F32)
    o_nsa = (gates[:, 0:512] * oc_ref[...].astype(F32)
             + gates[:, 512:1024] * os_ref[...].astype(F32)
             + gates[:, 1024:1536] * ow_ref[...].astype(F32))
    u_nsa = jnp.dot(o_nsa.astype(BF16), wn_ref[...], preferred_element_type=F32)

    lses = (l0_ref[...], l1_ref[...], l2_ref[...])
    mx = jnp.maximum(jnp.maximum(lses[0], lses[1]), lses[2])
    es = [jnp.exp2(l - mx) for l in lses]
    inv = 1.0 / (es[0] + es[1] + es[2])
    o_dil = (es[0] * od0_ref[...].astype(F32) + es[1] * od1_ref[...].astype(F32)
             + es[2] * od2_ref[...].astype(F32)) * inv
    u_dil = jnp.dot(o_dil.astype(BF16), wd_ref[...], preferred_element_type=F32)

    gm = 1.0 / (1.0 + jnp.exp(-ml_ref[...].astype(F32)))
    merged = gm[:, :d] * u_nsa + gm[:, d:] * u_dil
    y = jnp.dot(merged.astype(BF16), wo_ref[...], preferred_element_type=F32)
    o_ref[...] = x_ref[...] + g1_ref[...] * y


def _mix(x2, g1, oc, osel, ow, gl, ge, od, lse, ml, wn, wd, wo, seq):
    t, d = x2.shape
    tm = TM_PROJ
    per_b = seq // tm
    row = lambda i: (i, 0)
    const = lambda i: (0, 0)
    rows = lambda a: pl.BlockSpec((tm, a.shape[1]), row)
    full = lambda a: pl.BlockSpec(a.shape, const)
    return pl.pallas_call(
        _mix_kernel,
        grid=(t // tm,),
        in_specs=[rows(x2), pl.BlockSpec((None, 1, d), lambda i: (i // per_b, 0, 0)),
                  rows(oc), rows(osel), rows(ow), rows(gl), full(ge),
                  rows(od[0]), rows(od[1]), rows(od[2]), rows(lse[0]), rows(lse[1]), rows(lse[2]), rows(ml),
                  full(wn), full(wd), full(wo)],
        out_specs=pl.BlockSpec((tm, d), row),
        out_shape=jax.ShapeDtypeStruct((t, d), F32),
        compiler_params=_cparams(("arbitrary",)),
        name="mix_outproj",
    )(x2, g1, oc, osel, ow, gl, ge, *od, *lse, ml, wn, wd, wo)


def _router_kernel(x_ref, sc_ref, sh_ref, g_ref, wr_ref, br_ref, h_ref, eid_ref, wts_ref):
    h = _norm_mod(x_ref[...], g_ref[...], sc_ref[...], sh_ref[...])
    h_ref[...] = h
    logit = lax.dot_general(wr_ref[...], h, (((1,), (1,)), ((), ())), preferred_element_type=F32,
                            precision=lax.Precision.HIGHEST) + br_ref[...]
    grp = jnp.zeros((1, h.shape[0]), jnp.int32)
    best = logit[0:1]
    for k in range(1, N_EXPERT_GROUPS):
        better = logit[k:k + 1] > best
        grp = jnp.where(better, k, grp)
        best = jnp.where(better, logit[k:k + 1], best)
    den = jnp.zeros_like(best)
    for k in range(N_EXPERT_GROUPS):
        den = den + jnp.exp(logit[k:k + 1] - best)
    p_grp = 1.0 / den
    le = logit[SUBLANES:SUBLANES + EXPERTS_PER_GROUP]
    for k in range(1, N_EXPERT_GROUPS):
        lo = SUBLANES + k * EXPERTS_PER_GROUP
        le = jnp.where(grp == k, logit[lo:lo + EXPERTS_PER_GROUP], le)
    rowi = lax.broadcasted_iota(jnp.int32, le.shape, 0)
    v1 = jnp.max(le, axis=0, keepdims=True)
    i1 = jnp.min(jnp.where(le == v1, rowi, EXPERTS_PER_GROUP), axis=0, keepdims=True)
    rest = jnp.where(rowi == i1, -3e38, le)
    v2 = jnp.max(rest, axis=0, keepdims=True)
    i2 = jnp.min(jnp.where(rest == v2, rowi, EXPERTS_PER_GROUP), axis=0, keepdims=True)
    e2 = jnp.exp(v2 - v1)
    inv = p_grp / (1.0 + e2)
    eid_ref[...] = jnp.concatenate([grp * EXPERTS_PER_GROUP + i1, grp * EXPERTS_PER_GROUP + i2], axis=0)
    wts_ref[...] = jnp.concatenate([inv, e2 * inv], axis=0)


def _router(x2, sc, sh, g, wr_t, br, seq):
    t, d = x2.shape
    tm = TM_PROJ
    per_b = seq // tm
    row = lambda i: (i, 0)
    const = lambda i: (0, 0)
    return pl.pallas_call(
        _router_kernel,
        grid=(t // tm,),
        in_specs=[pl.BlockSpec((tm, d), row),
                  pl.BlockSpec((None, 1, d), lambda i: (i // per_b, 0, 0)),
                  pl.BlockSpec((None, 1, d), lambda i: (i // per_b, 0, 0)),
                  pl.BlockSpec((1, d), const),
                  pl.BlockSpec(wr_t.shape, const),
                  pl.BlockSpec(br.shape, const)],
        out_specs=[pl.BlockSpec((tm, d), row),
                   pl.BlockSpec((2, tm), lambda i: (0, i)),
                   pl.BlockSpec((2, tm), lambda i: (0, i))],
        out_shape=[jax.ShapeDtypeStruct((t, d), F32),
                   jax.ShapeDtypeStruct((2, t), jnp.int32),
                   jax.ShapeDtypeStruct((2, t), F32)],
        compiler_params=_cparams(("arbitrary",)),
        name="norm_router",
    )(x2, sc, sh, g, wr_t, br)


def _moe_kernel(cnt_ref, off_ref, tok_ref, wt_ref, h_ref, w1_ref, w3_ref, w2_ref, o_ref,
                xs_ref, os_ref, xb_ref, y3_ref):
    c = pl.program_id(0)
    e = pl.program_id(1)
    n_sub = h_ref.shape[1]
    rb = xs_ref.shape[0] // n_sub

    @pl.when((c == 0) & (e == 0))
    def _():
        xs_ref[...] = jnp.zeros_like(xs_ref)

    @pl.when(e == 0)
    def _():
        o_ref[...] = jnp.zeros_like(o_ref)

    n = cnt_ref[c, e]
    off = off_ref[c, e]

    def unrolled(rows, fn):
        groups = rows // MOE_UNROLL

        def grp(g, _):
            for u in range(MOE_UNROLL):
                fn(g * MOE_UNROLL + u)
            return 0

        lax.fori_loop(0, groups, grp, 0)
        lax.fori_loop(groups * MOE_UNROLL, rows, lambda r, _: (fn(r), 0)[1], 0)

    def block(bi, _):
        base = off + bi * rb
        rows = jnp.minimum(rb, n - bi * rb)

        def gather(r):
            tkn = tok_ref[0, base + r]
            dst = pl.ds(pl.multiple_of(r * n_sub, n_sub), n_sub)
            xs_ref[dst, :] = h_ref[tkn]
            os_ref[dst, :] = o_ref[tkn]

        unrolled(rows, gather)
        for j in range(n_sub):
            xb_ref[:, j * LANES:(j + 1) * LANES] = xs_ref[pl.ds(j, rb, stride=n_sub), :].astype(BF16)
        xb = xb_ref[...]
        a = jnp.dot(xb, w1_ref[...], preferred_element_type=F32)
        b = jnp.dot(xb, w3_ref[...], preferred_element_type=F32)
        mid = (a * (1.0 / (1.0 + jnp.exp(-a))) * b).astype(BF16)
        y = jnp.dot(mid, w2_ref[...], preferred_element_type=F32)
        for j in range(n_sub):
            y3_ref[pl.ds(j, rb, stride=n_sub), :] = y[:, j * LANES:(j + 1) * LANES]

        def scatter(r):
            tkn = tok_ref[0, base + r]
            src = pl.ds(pl.multiple_of(r * n_sub, n_sub), n_sub)
            o_ref[tkn] = os_ref[src, :] + wt_ref[0, base + r] * y3_ref[src, :]

        unrolled(rows, scatter)
        return 0

    lax.fori_loop(0, (n + rb - 1) // rb, block, 0)


def _moe(h2, eid, wts, w1, w3, w2):
    t, d = h2.shape
    tc = min(MOE_CHUNK, t)
    n_chunks = t // tc
    n_sub = d // LANES
    slots = 2 * tc
    tok = jnp.arange(t, dtype=jnp.int32)
    key = ((tok // tc)[None, :] * N_EXPERTS + eid).reshape(-1)
    order = jnp.argsort(key)
    tok_sorted = (jnp.tile(tok % tc, 2)[order]).reshape(n_chunks, 1, slots)
    w_sorted = wts.reshape(-1)[order].reshape(n_chunks, 1, slots)
    counts = jnp.zeros((n_chunks * N_EXPERTS,), jnp.int32).at[key].add(1).reshape(n_chunks, N_EXPERTS)
    offs = jnp.cumsum(counts, axis=1) - counts

    grid_spec = pltpu.PrefetchScalarGridSpec(
        num_scalar_prefetch=2,
        grid=(n_chunks, N_EXPERTS),
        in_specs=[pl.BlockSpec((None, 1, slots), lambda c, e, *_: (c, 0, 0), memory_space=pltpu.SMEM),
                  pl.BlockSpec((None, 1, slots), lambda c, e, *_: (c, 0, 0), memory_space=pltpu.SMEM),
                  pl.BlockSpec((tc, n_sub, LANES), lambda c, e, *_: (c, 0, 0)),
                  pl.BlockSpec((None, d, D_EXPERT), lambda c, e, *_: (e, 0, 0)),
                  pl.BlockSpec((None, d, D_EXPERT), lambda c, e, *_: (e, 0, 0)),
                  pl.BlockSpec((None, D_EXPERT, d), lambda c, e, *_: (e, 0, 0))],
        out_specs=pl.BlockSpec((tc, n_sub, LANES), lambda c, e, *_: (c, 0, 0)),
        scratch_shapes=[pltpu.VMEM((MOE_ROWS * n_sub, LANES), F32),
                        pltpu.VMEM((MOE_ROWS * n_sub, LANES), F32),
                        pltpu.VMEM((MOE_ROWS, d), BF16),
                        pltpu.VMEM((MOE_ROWS * n_sub, LANES), F32)],
    )
    out = pl.pallas_call(
        _moe_kernel,
        grid_spec=grid_spec,
        out_shape=jax.ShapeDtypeStruct((t, n_sub, LANES), F32),
        compiler_params=_cparams(("arbitrary", "arbitrary")),
        name="moe_experts",
    )(counts, offs, tok_sorted, w_sorted, h2.reshape(t, n_sub, LANES), w1, w3, w2)
    return out.reshape(t, d)


def _resid_kernel(x_ref, y_ref, g_ref, nf_ref, o_ref, *, final):
    x = x_ref[...] + g_ref[...] * y_ref[...]
    if final:
        ms = jnp.mean(x * x, axis=-1, keepdims=True)
        x = x * lax.rsqrt(ms + RMS_EPS) * nf_ref[...]
    o_ref[...] = x


def _residual(x2, y2, g2, norm_f, seq, final):
    t, d = x2.shape
    tm = TM_PROJ
    per_b = seq // tm
    row = lambda i: (i, 0)
    return pl.pallas_call(
        functools.partial(_resid_kernel, final=final),
        grid=(t // tm,),
        in_specs=[pl.BlockSpec((tm, d), row), pl.BlockSpec((tm, d), row),
                  pl.BlockSpec((None, 1, d), lambda i: (i // per_b, 0, 0)),
                  pl.BlockSpec((1, d), lambda i: (0, 0))],
        out_specs=pl.BlockSpec((tm, d), row),
        out_shape=jax.ShapeDtypeStruct((t, d), F32),
        compiler_params=_cparams(("arbitrary",)),
        name="residual_final" if final else "residual",
    )(x2, y2, g2, norm_f)


def _split_w_in(w_in, d):
    scale = HEAD_DIM ** -0.5 * LOG2E
    nq = NSA_HEADS * HEAD_DIM
    nkv = 3 * 2 * NSA_KV_HEADS * HEAD_DIM
    ngate = 3 * NSA_HEADS
    ndil = 3 * N_DIL_GROUPS * DIL_HEADS_PER_GROUP * HEAD_DIM
    o1, o2, o3 = nq, nq + nkv, nq + nkv + ngate
    o4 = o3 + ndil
    wq = (w_in[:, :o1] * scale).reshape(d, NSA_KV_HEADS, NSA_GROUP, HEAD_DIM)
    wq = wq.transpose(0, 2, 1, 3).reshape(d, nq)
    wkv = w_in[:, o1:o2]
    wg = jnp.pad(w_in[:, o2:o3], ((0, 0), (0, LANES - ngate)))
    gw = DIL_HEADS_PER_GROUP * HEAD_DIM
    per_which = N_DIL_GROUPS * gw
    wds = []
    for grp in range(N_DIL_GROUPS):
        parts = [w_in[:, o3 + which * per_which + grp * gw: o3 + which * per_which + (grp + 1) * gw]
                 for which in range(3)]
        parts[0] = parts[0] * scale
        wds.append(jnp.concatenate(parts, axis=1))
    wm = w_in[:, o4:]
    cast = lambda w: w.astype(BF16)
    return [cast(wq), cast(wkv), cast(wds[0]), cast(wds[1]), cast(wds[2]), cast(wm), cast(wg)]


def kernel(x, c, rel_bias, ada_w, ada_b, norm1, norm2, w_in, cmp_pos, cmp_w1, cmp_w2, w_up_nsa, w_up_dil, w_o,
           router_wg, router_bg, router_we, router_be, exp_w1, exp_w3, exp_w2, norm_f):
    batch, seq, d = x.shape
    depth = ada_w.shape[0]
    t = batch * seq
    n_cmp = seq // CMP_STRIDE
    n_sel = seq // SEL_BLOCK
    assert seq % TQ_NSA == 0 and n_sel <= HEAD_DIM and n_sel >= SEL_TOPN
    assert all(seq % (dil * TQ_DIL) == 0 for _, dil in DIL_PAIRS)

    mod = _modulation(c, ada_w, ada_b)

    nq_nsa = seq // TQ_NSA
    bias_cmp = _expand_bias(rel_bias, _cmp_buckets(seq, n_cmp), 0, NSA_HEADS, 8 * SUBLANES)
    sel_b = _toeplitz_buckets(nq_nsa, TQ_NSA, seq, 1)
    n_sel_bias = nq_nsa
    while n_sel_bias > 1 and (sel_b[n_sel_bias - 2:] == sel_b[n_sel_bias - 1, 0, 0]).all():
        n_sel_bias -= 1
    sel_b = sel_b[:n_sel_bias].reshape(n_sel_bias * TQ_NSA, TQ_NSA)
    bias_sel = _expand_bias(rel_bias, sel_b, 0, NSA_HEADS, TQ_NSA).reshape(NSA_HEADS, n_sel_bias, TQ_NSA, TQ_NSA)
    nb_win = min(-(-NSA_WINDOW // TQ_NSA), nq_nsa - 1)
    win_b = _toeplitz_buckets(nb_win + 1, TQ_NSA, NSA_WINDOW, 1).reshape(-1, TQ_NSA)
    bias_win = _expand_bias(rel_bias, win_b, 0, NSA_HEADS, TQ_NSA).reshape(NSA_HEADS, nb_win + 1, TQ_NSA, TQ_NSA)
    bias_dil, nb_dil = [], []
    for grp, (window, dil) in enumerate(DIL_PAIRS):
        nb = min(-(-(window // dil) // TQ_DIL), seq // dil // TQ_DIL - 1)
        bk = _toeplitz_buckets(nb + 1, TQ_DIL, window // dil, dil).reshape(-1, TQ_DIL)
        hb = NSA_HEADS + grp * DIL_HEADS_PER_GROUP
        bias_dil.append(_expand_bias(rel_bias, bk, hb, DIL_HEADS_PER_GROUP, TQ_DIL)
                        .reshape(DIL_HEADS_PER_GROUP, nb + 1, TQ_DIL, TQ_DIL))
        nb_dil.append(nb)
    ovl_t = jnp.asarray(_overlap_t(n_cmp, n_sel), BF16)
    onehot = jnp.asarray(_block_onehot(seq), BF16)
    gate_e = jnp.asarray(_gate_expand(), BF16)

    x2 = x.reshape(t, d)
    for l in range(depth):
        sh1, sc1, g1, sh2, sc2, g2 = [m.reshape(batch, 1, d) for m in jnp.split(mod[l], 6, axis=-1)]
        weights = _split_w_in(w_in[l], d)
        (q_n, kc_in, vc_in, k_sel, v_sel, k_win, v_win, qkv_d0, qkv_d1, qkv_d2, merge_l, gate_l) = _inproj(
            x2, sc1, sh1, norm1[l].reshape(1, d), onehot, weights, seq)

        kc = _compress(kc_in, cmp_pos[l, 0], cmp_w1[l, 0], cmp_w2[l, 0], batch, seq, transpose_out=False)
        vct = _compress(vc_in, cmp_pos[l, 1], cmp_w1[l, 1], cmp_w2[l, 1], batch, seq, transpose_out=True)
        o_c, selb = _cmp_select(q_n, kc, vct, bias_cmp, ovl_t, batch, seq)
        nsa_common = dict(batch=batch, length=seq, dil=1, tile=TQ_NSA, n_pairs=NSA_GROUP, q_blk=lambda r: 0,
                          k_blk=lambda r: 0, per_pair_kv=False,
                          heads=tuple((g, NSA_GROUP + g) for g in range(NSA_GROUP)))
        q3 = q_n.reshape(batch, seq, 512)
        (o_s,) = _flash(q3, k_sel.reshape(batch, seq, 2 * LANES),
                        _transpose_values(v_sel, batch, seq, 1, 1, TQ_NSA), bias_sel, n_back=None,
                        k_width=2 * LANES, selb=selb.reshape(batch, seq, 2 * LANES), **nsa_common)
        (o_w,) = _flash(q3, k_win.reshape(batch, seq, LANES), _transpose_values(v_win, batch, seq, 1, 1, TQ_NSA),
                        bias_win, n_back=nb_win, k_width=LANES, **nsa_common)

        o_d, lse_d = [], []
        for grp, ((window, dil), qkv) in enumerate(zip(DIL_PAIRS, (qkv_d0, qkv_d1, qkv_d2))):
            length = seq // dil
            view = qkv.reshape(batch, length, dil * 768)
            vt = _transpose_values(qkv[:, 512:], batch, length, dil, 2, TQ_DIL)
            o_g, lse_g = _flash(view, view, vt, bias_dil[grp], batch=batch, length=length, dil=dil, tile=TQ_DIL,
                                n_back=nb_dil[grp], n_pairs=2, q_blk=lambda r: 3 * r, k_blk=lambda r: 3 * r + 1,
                                k_width=2 * LANES, per_pair_kv=True, heads=((0, 1), (2, 3)), with_lse=True)
            o_d.append(o_g.reshape(t, 256))
            lse_d.append(lse_g.reshape(t, 256))

        wn = w_up_nsa[l].reshape(NSA_KV_HEADS, NSA_GROUP, HEAD_DIM, d).transpose(1, 0, 2, 3).reshape(512, d)
        x2 = _mix(x2, g1, o_c, o_s.reshape(t, 512), o_w.reshape(t, 512), gate_l, gate_e, o_d, lse_d, merge_l,
                  wn.astype(BF16), w_up_dil[l].astype(BF16), w_o[l].astype(BF16), seq)

        wr_t = jnp.concatenate([jnp.pad(router_wg[l], ((0, 0), (0, SUBLANES - N_EXPERT_GROUPS))),
                                router_we[l]], axis=1).T
        br = jnp.concatenate([jnp.pad(router_bg[l], (0, SUBLANES - N_EXPERT_GROUPS)),
                              router_be[l]]).reshape(-1, 1)
        h2, eid, wts = _router(x2, sc2, sh2, norm2[l].reshape(1, d), wr_t, br, seq)
        y = _moe(h2, eid, wts, exp_w1[l].astype(BF16), exp_w3[l].astype(BF16), exp_w2[l].astype(BF16))
        x2 = _residual(x2, y, g2, norm_f.reshape(1, d), seq, final=(l == depth - 1))
    return x2.reshape(batch, seq, d)
```

```python
import functools
import math

import numpy as np
import jax
import jax.numpy as jnp
from jax import lax
from jax.experimental import pallas as pl
from jax.experimental.pallas import tpu as pltpu

F32 = jnp.float32
BF16 = jnp.bfloat16

HEAD_DIM = 64
NSA_HEADS = 8
NSA_KV_HEADS = 2
NSA_GROUP = NSA_HEADS // NSA_KV_HEADS
CMP_LEN = 32
CMP_STRIDE = 16
CMP_HIDDEN = 256
SEL_BLOCK = 64
SEL_TOPN = 16
NSA_WINDOW = 512
FORCE_SCORE = 1e4
DIL_PAIRS = ((128, 1), (512, 4), (2048, 16))
N_DIL_GROUPS = 3
DIL_HEADS_PER_GROUP = 4
NUM_BUCKETS = 32
REL_MAX_DIST = 2048
N_EXPERT_GROUPS = 4
EXPERTS_PER_GROUP = 8
N_EXPERTS = N_EXPERT_GROUPS * EXPERTS_PER_GROUP
D_EXPERT = 512
RMS_EPS = 1e-6

LOG2E = math.log2(math.e)
LANES = 128
SUBLANES = 8
MASK_NEG = -1e30
SEL_NEG = -1e9
TQ_NSA = 256
TQ_DIL = 128
TM_PROJ = 512
FLASH_LOOKAHEAD = 4
MOE_CHUNK = 2048
MOE_ROWS = 128
MOE_UNROLL = 8
VMEM_LIMIT = 56 * 1024 * 1024


def _cparams(sem):
    return pltpu.CompilerParams(dimension_semantics=sem, vmem_limit_bytes=VMEM_LIMIT)


def _np_bucket(dist):
    dist = np.maximum(dist, 0)
    max_exact = NUM_BUCKETS // 2
    df = np.maximum(dist, 1).astype(np.float32)
    val = np.log(df / np.float32(max_exact)) / np.float32(math.log(REL_MAX_DIST / max_exact))
    large = max_exact + (val * np.float32(NUM_BUCKETS - max_exact)).astype(np.int32)
    large = np.minimum(large, NUM_BUCKETS - 1)
    return np.where(dist < max_exact, dist, large).astype(np.int32)


def _toeplitz_buckets(n_delta, tile, window, dist_scale, masked_tail=False):
    dd = np.arange(n_delta)[:, None, None]
    r = np.arange(tile)[None, :, None]
    c = np.arange(tile)[None, None, :]
    dist = dd * tile + c - r
    valid = (dist >= 0) & (dist <= window)
    out = np.where(valid, _np_bucket(dist * dist_scale), -1).astype(np.int32)
    if masked_tail:
        out = np.concatenate([out, np.full((1, tile, tile), -1, np.int32)], axis=0)
    return out


def _cmp_buckets(seq, n_cmp):
    c_end = np.arange(n_cmp)[:, None] * CMP_STRIDE + CMP_LEN - 1
    t = np.arange(seq)[None, :]
    dist = t - c_end
    return np.where(dist >= 0, _np_bucket(dist), -1).astype(np.int32)


def _overlap_t(n_cmp, n_sel):
    c_start = np.arange(n_cmp)[None, :] * CMP_STRIDE
    s_start = np.arange(n_sel)[:, None] * SEL_BLOCK
    ov = np.clip(np.minimum(c_start + CMP_LEN, s_start + SEL_BLOCK) - np.maximum(c_start, s_start), 0, None)
    out = np.zeros((LANES, n_cmp), np.float32)
    out[64:64 + n_sel] = ov.astype(np.float32) / CMP_STRIDE
    return out


def _block_onehot(seq):
    oh = np.zeros((seq, LANES), np.float32)
    blk = np.arange(seq) // SEL_BLOCK
    oh[np.arange(seq), blk] = 1.0
    oh[np.arange(seq), 64 + blk] = 1.0
    return oh


def _gate_expand():
    e = np.zeros((LANES, 3 * NSA_HEADS * HEAD_DIM), np.float32)
    for br in range(3):
        for g in range(NSA_GROUP):
            for ln in range(LANES):
                kv = ln // HEAD_DIM
                e[br * NSA_HEADS + kv * NSA_GROUP + g, br * 512 + g * LANES + ln] = 1.0
    return e


def _mod_kernel(c_ref, w_ref, b_ref, o_ref):
    c = c_ref[...]
    cond = c * (1.0 / (1.0 + jnp.exp(-c)))
    o_ref[...] = jnp.dot(cond, w_ref[...], preferred_element_type=F32,
                         precision=lax.Precision.HIGHEST) + b_ref[...]


def _modulation(c, ada_w, ada_b):
    depth, d, n = ada_w.shape
    b = c.shape[0]
    tn = 1536
    return pl.pallas_call(
        _mod_kernel,
        grid=(depth, n // tn),
        in_specs=[pl.BlockSpec((b, d), lambda l, j: (0, 0)),
                  pl.BlockSpec((None, d, tn), lambda l, j: (l, 0, j)),
                  pl.BlockSpec((None, 1, tn), lambda l, j: (l, 0, j))],
        out_specs=pl.BlockSpec((None, b, tn), lambda l, j: (l, 0, j)),
        out_shape=jax.ShapeDtypeStruct((depth, b, n), F32),
        compiler_params=_cparams(("arbitrary", "arbitrary")),
        name="adaln_modulation",
    )(c, ada_w, ada_b.reshape(depth, 1, n))


def _bias_kernel(tbl_ref, bkt_ref, o_ref, *, head_base):
    h = pl.program_id(0) + head_base
    bkt = bkt_ref[...]
    acc = jnp.full(bkt.shape, MASK_NEG, F32)
    for b in range(NUM_BUCKETS):
        acc = jnp.where(bkt == b, tbl_ref[b, h] * LOG2E, acc)
    o_ref[...] = acc


def _expand_bias(rel_bias, buckets, head_base, n_heads, row_tile):
    rows, cols = buckets.shape
    return pl.pallas_call(
        functools.partial(_bias_kernel, head_base=head_base),
        grid=(n_heads, rows // row_tile),
        in_specs=[pl.BlockSpec(memory_space=pltpu.SMEM),
                  pl.BlockSpec((row_tile, cols), lambda h, i: (i, 0))],
        out_specs=pl.BlockSpec((None, row_tile, cols), lambda h, i: (h, i, 0)),
        out_shape=jax.ShapeDtypeStruct((n_heads, rows, cols), F32),
        compiler_params=_cparams(("arbitrary", "arbitrary")),
        name="bias_expand",
    )(rel_bias, jnp.asarray(buckets))


def _norm_mod(x, g, sc, sh):
    ms = jnp.mean(x * x, axis=-1, keepdims=True)
    y = x * lax.rsqrt(ms + RMS_EPS) * g
    return y * (1.0 + sc) + sh


def _inproj_kernel(x_ref, sc_ref, sh_ref, g_ref, oh_ref, wq_ref, wkv_ref, wd0_ref, wd1_ref, wd2_ref, wm_ref, wg_ref,
                   q_ref, kc_ref, vc_ref, ks_ref, vs_ref, kw_ref, vw_ref, d0_ref, d1_ref, d2_ref, m_ref, gl_ref):
    h = _norm_mod(x_ref[...], g_ref[...], sc_ref[...], sh_ref[...]).astype(BF16)

    def proj(w_ref):
        return jnp.dot(h, w_ref[...], preferred_element_type=F32)

    q_ref[...] = proj(wq_ref).astype(BF16)
    kv = proj(wkv_ref).astype(BF16)
    for k, ref in ((0, kc_ref), (1, vc_ref), (3, vs_ref), (4, kw_ref), (5, vw_ref)):
        ref[...] = kv[:, k * LANES:(k + 1) * LANES]
    k_sel = kv[:, 2 * LANES:3 * LANES]
    oh = oh_ref[...]
    lo = lax.broadcasted_iota(jnp.int32, k_sel.shape, 1) < HEAD_DIM
    ks_ref[:, :LANES] = jnp.where(lo, k_sel, oh)
    ks_ref[:, LANES:] = jnp.where(lo, oh, k_sel)
    d0_ref[...] = proj(wd0_ref).astype(BF16)
    d1_ref[...] = proj(wd1_ref).astype(BF16)
    d2_ref[...] = proj(wd2_ref).astype(BF16)
    m_ref[...] = proj(wm_ref).astype(BF16)
    gl_ref[...] = proj(wg_ref)


def _inproj(x2, sc, sh, g, onehot, weights, seq):
    t, d = x2.shape
    tm = TM_PROJ
    per_b = seq // tm
    widths = (512, 128, 128, 256, 128, 128, 128, 768, 768, 768, 2 * d, LANES)
    dtypes = (BF16,) * 11 + (F32,)
    row = lambda i: (i, 0)
    const = lambda i: (0, 0)
    in_specs = [pl.BlockSpec((tm, d), row),
                pl.BlockSpec((None, 1, d), lambda i: (i // per_b, 0, 0)),
                pl.BlockSpec((None, 1, d), lambda i: (i // per_b, 0, 0)),
                pl.BlockSpec((1, d), const),
                pl.BlockSpec((tm, LANES), lambda i: (i % per_b, 0))]
    in_specs += [pl.BlockSpec(w.shape, const) for w in weights]
    return pl.pallas_call(
        _inproj_kernel,
        grid=(t // tm,),
        in_specs=in_specs,
        out_specs=[pl.BlockSpec((tm, w), row) for w in widths],
        out_shape=[jax.ShapeDtypeStruct((t, w), dt) for w, dt in zip(widths, dtypes)],
        compiler_params=_cparams(("arbitrary",)),
        name="norm_inproj",
    )(x2, sc, sh, g, onehot, *weights)


def _gelu_tanh(x):
    return 0.5 * x * (1.0 + jnp.tanh(math.sqrt(2.0 / math.pi) * (x + 0.044715 * (x * x * x))))


def _compress_kernel(r_ref, pt_ref, pb_ref, wt_ref, wb_ref, w2_ref, o_ref, *, transpose_out):
    r = r_ref[...].astype(F32)
    top = jnp.dot((r + pt_ref[...]).astype(BF16), wt_ref[...], preferred_element_type=F32)
    bot = jnp.dot((r + pb_ref[...]).astype(BF16), wb_ref[...], preferred_element_type=F32)
    n = bot.shape[0]
    hid = top + pltpu.roll(bot, n - 1, 0)
    act = _gelu_tanh(hid).astype(BF16)
    if transpose_out:
        o_ref[...] = lax.dot_general(w2_ref[...], act, (((1,), (1,)), ((), ())),
                                     preferred_element_type=F32).astype(BF16)
    else:
        o_ref[...] = jnp.dot(act, w2_ref[...], preferred_element_type=F32).astype(BF16)


def _compress(tok, pos, w1, w2, batch, seq, transpose_out):
    nc = seq // CMP_STRIDE
    half = CMP_LEN // 2
    eye = jnp.eye(NSA_KV_HEADS, dtype=F32)
    w1r = w1.reshape(CMP_LEN, HEAD_DIM, CMP_HIDDEN)
    blk = lambda w: jnp.einsum('ldn,hg->lhdgn', w, eye).reshape(half * LANES, 2 * CMP_HIDDEN).astype(BF16)
    wt, wb = blk(w1r[:half]), blk(w1r[half:])
    posr = lambda p: jnp.broadcast_to(p[:, None, :], (half, NSA_KV_HEADS, HEAD_DIM)).reshape(1, half * LANES)
    pt, pb = posr(pos[:half]), posr(pos[half:])
    w2b = jnp.einsum('nd,hg->hngd', w2, eye).reshape(2 * CMP_HIDDEN, LANES)
    if transpose_out:
        w2b = w2b.T
        out_block, out_shape = (None, LANES, nc), (batch, LANES, nc)
    else:
        out_block, out_shape = (None, nc, LANES), (batch, nc, LANES)
    w2b = w2b.astype(BF16)
    const = lambda b: (0, 0)
    return pl.pallas_call(
        functools.partial(_compress_kernel, transpose_out=transpose_out),
        grid=(batch,),
        in_specs=[pl.BlockSpec((None, nc, half * LANES), lambda b: (b, 0, 0)),
                  pl.BlockSpec(pt.shape, const), pl.BlockSpec(pb.shape, const),
                  pl.BlockSpec(wt.shape, const), pl.BlockSpec(wb.shape, const),
                  pl.BlockSpec(w2b.shape, const)],
        out_specs=pl.BlockSpec(out_block, lambda b: (b, 0, 0)),
        out_shape=jax.ShapeDtypeStruct(out_shape, BF16),
        compiler_params=_cparams(("arbitrary",)),
        name="nsa_compress",
    )(tok.reshape(batch, nc, half * LANES), pt, pb, wt, wb, w2b)


def _cmp_select_kernel(q_ref, kc_ref, vct_ref, bias_ref, ovl_ref, o_ref, selb_ref, *, n_sel, n_top):
    tq = q_ref.shape[0]
    nc = kc_ref.shape[0]
    qs = pl.program_id(1) * tq
    kc = kc_ref[...]
    vct = vct_ref[...]
    lane_q = lax.broadcasted_iota(jnp.int32, (tq, LANES), 1)
    row_o = lax.broadcasted_iota(jnp.int32, (LANES, tq), 0)
    pc_sum = [jnp.zeros((nc, tq), F32) for _ in range(NSA_KV_HEADS)]
    for g in range(NSA_GROUP):
        qt = q_ref[:, g * LANES:(g + 1) * LANES]
        outs = []
        for kv in range(NSA_KV_HEADS):
            mine = (lane_q < HEAD_DIM) if kv == 0 else (lane_q >= HEAD_DIM)
            qm = jnp.where(mine, qt, jnp.zeros_like(qt))
            s = lax.dot_general(kc, qm, (((1,), (1,)), ((), ())), preferred_element_type=F32)
            s = s + bias_ref[kv * NSA_GROUP + g]
            m = jnp.max(s, axis=0, keepdims=True)
            m = jnp.where(m < 0.5 * MASK_NEG, 0.0, m)
            p = jnp.exp2(s - m)
            den = jnp.sum(p, axis=0, keepdims=True)
            pc = p * (1.0 / jnp.where(den > 0.0, den, 1.0))
            pc_sum[kv] = pc_sum[kv] + pc
            outs.append(jnp.dot(vct, pc.astype(BF16), preferred_element_type=F32))
        o_t = jnp.where(row_o < HEAD_DIM, outs[0], outs[1])
        o_ref[:, g * LANES:(g + 1) * LANES] = o_t.T.astype(BF16)

    rowj = lax.broadcasted_iota(jnp.int32, (HEAD_DIM, tq), 0)
    t = qs + lax.broadcasted_iota(jnp.int32, (HEAD_DIM, tq), 1)
    jq = jnp.right_shift(t, SEL_BLOCK.bit_length() - 1)
    forced = (rowj == 0) | (rowj == jq) | (rowj == jq - 1)
    future = rowj > jq
    ovl = ovl_ref[...]
    for kv in range(NSA_KV_HEADS):
        hi = pc_sum[kv].astype(BF16)
        lo = (pc_sum[kv] - hi.astype(F32)).astype(BF16)
        imp = (jnp.dot(ovl, hi, preferred_element_type=F32)
               + jnp.dot(ovl, lo, preferred_element_type=F32))[HEAD_DIM:]
        score = jnp.where(forced, FORCE_SCORE, jnp.where(future, -1.0, imp))
        rem = jnp.where(rowj < n_sel, score, -3e38)
        sel = jnp.zeros((HEAD_DIM, tq), F32)
        for _ in range(n_top):
            m = jnp.max(rem, axis=0, keepdims=True)
            idx = jnp.min(jnp.where(rem == m, rowj, HEAD_DIM), axis=0, keepdims=True)
            pick = rowj == idx
            sel = jnp.where(pick, 1.0, sel)
            rem = jnp.where(pick, -3e38, rem)
        sb = jnp.where(sel > 0.5, 0.0, SEL_NEG)
        zero = jnp.zeros_like(sb)
        full = jnp.concatenate([zero, sb], axis=0) if kv == 0 else jnp.concatenate([sb, zero], axis=0)
        selb_ref[:, kv * LANES:(kv + 1) * LANES] = full.T.astype(BF16)


def _cmp_select(q, kc, vct, bias_c, ovl_t, batch, seq):
    t = q.shape[0]
    tq = TQ_NSA
    nq = seq // tq
    nc = seq // CMP_STRIDE
    n_sel = seq // SEL_BLOCK
    n_top = min(SEL_TOPN, n_sel)
    return pl.pallas_call(
        functools.partial(_cmp_select_kernel, n_sel=n_sel, n_top=n_top),
        grid=(batch, nq),
        in_specs=[pl.BlockSpec((tq, 512), lambda b, i: (b * nq + i, 0)),
                  pl.BlockSpec((None, nc, LANES), lambda b, i: (b, 0, 0)),
                  pl.BlockSpec((None, LANES, nc), lambda b, i: (b, 0, 0)),
                  pl.BlockSpec((NSA_HEADS, nc, tq), lambda b, i: (0, 0, i)),
                  pl.BlockSpec((LANES, nc), lambda b, i: (0, 0))],
        out_specs=[pl.BlockSpec((tq, 512), lambda b, i: (b * nq + i, 0)),
                   pl.BlockSpec((tq, 2 * LANES), lambda b, i: (b * nq + i, 0))],
        out_shape=[jax.ShapeDtypeStruct((t, 512), BF16),
                   jax.ShapeDtypeStruct((t, 2 * LANES), BF16)],
        compiler_params=_cparams(("arbitrary", "arbitrary")),
        name="nsa_cmp_select",
    )(q, kc, vct, bias_c, ovl_t)


def _flash_kernel(*refs, tile, n_pairs, heads, per_pair_kv, n_back, n_bias, with_sel, with_lse):
    it = iter(refs)
    q_ref, k_ref, v_ref, b_ref = (next(it) for _ in range(4))
    selb_ref = next(it) if with_sel else None
    o_ref = next(it)
    lse_ref = next(it) if with_lse else None
    qa_ref = next(it)
    acc_ref = next(it) if n_back is None else None

    i = pl.program_id(1)
    n_heads = 2 * n_pairs
    lo = lax.broadcasted_iota(jnp.int32, (tile, LANES), 1) < HEAD_DIM
    for p in range(n_pairs):
        q = q_ref[:, p * LANES:(p + 1) * LANES]
        if with_sel:
            qa_ref[2 * p] = jnp.where(lo, q, selb_ref[:, :LANES])
            qa_ref[2 * p + 1] = jnp.where(lo, selb_ref[:, LANES:], q)
        else:
            zero = jnp.zeros_like(q)
            qa_ref[2 * p] = jnp.where(lo, q, zero)
            qa_ref[2 * p + 1] = jnp.where(lo, zero, q)
    nt = (((1,), (1,)), ((), ()))
    tn = (((0,), (0,)), ((), ()))
    top = lax.broadcasted_iota(jnp.int32, (LANES, tile), 0) < HEAD_DIM

    def finish(p, acc0, l0, m0, acc1, l1, m1):
        o_t = jnp.where(top, acc0 * (1.0 / l0), acc1 * (1.0 / l1))
        o_ref[:, p * LANES:(p + 1) * LANES] = o_t.T.astype(o_ref.dtype)
        if with_lse:
            lse_t = jnp.where(top, m0 + jnp.log2(l0), m1 + jnp.log2(l1))
            lse_ref[:, p * LANES:(p + 1) * LANES] = lse_t.T

    def scores(hd, tiles):
        p, half = divmod(hd, 2)
        kc = half if with_sel else (p if per_pair_kv else 0)
        parts = []
        for j, bidx in tiles:
            kt = k_ref[pl.ds(pl.multiple_of(j * tile, tile), tile), kc * LANES:(kc + 1) * LANES]
            s = lax.dot_general(kt, qa_ref[hd], nt, preferred_element_type=F32)
            parts.append(s + b_ref[heads[p][half], bidx])
        return parts

    def group(tiles, state):
        res = []
        ahead = [scores(hd, tiles) for hd in range(min(FLASH_LOOKAHEAD, n_heads))]
        for hd in range(n_heads):
            parts = ahead.pop(0)
            if hd + FLASH_LOOKAHEAD < n_heads:
                ahead.append(scores(hd + FLASH_LOOKAHEAD, tiles))
            m_new = functools.reduce(jnp.maximum, [jnp.max(s, axis=0, keepdims=True) for s in parts])
            if state is not None:
                m, l = state[hd]
                m_new = jnp.maximum(m, m_new)
                alpha = jnp.exp2(m - m_new)
            prs = [jnp.exp2(s - m_new) for s in parts]
            l_new = functools.reduce(jnp.add, [jnp.sum(pr, axis=0, keepdims=True) for pr in prs])
            vc = (hd // 2) if per_pair_kv else 0
            pv = None
            for (j, _), pr in zip(tiles, prs):
                vt = v_ref[pl.ds(pl.multiple_of(j * tile, tile), tile), vc * LANES:(vc + 1) * LANES]
                d = lax.dot_general(vt, pr.astype(BF16), tn, preferred_element_type=F32)
                pv = d if pv is None else pv + d
            if state is not None:
                l_new = alpha * l + l_new
                pv = alpha * acc_ref[hd] + pv
            res.append((m_new, l_new, pv))
        return res

    if n_back is not None:
        tiles = [(jnp.maximum(i - k, 0), jnp.where(i >= k, k, n_bias - 1)) for k in range(n_back + 1)]
        res = group(tiles, None)
        for p in range(n_pairs):
            (m0, l0, a0), (m1, l1, a1) = res[2 * p], res[2 * p + 1]
            finish(p, a0, l0, m0, a1, l1, m1)
        return

    def store(res):
        for hd, (_, _, acc) in enumerate(res):
            acc_ref[hd] = acc
        return tuple(x for m, l, _ in res for x in (m, l))

    def unpack(carry):
        return [(carry[2 * hd], carry[2 * hd + 1]) for hd in range(n_heads)]

    bias_of = lambda back: jnp.minimum(back, n_bias - 1)
    carry = store(group([(i, 0)], None))

    def pair_body(step, carry):
        back = 2 * step + 1
        return store(group([(i - back, bias_of(back)), (i - back - 1, bias_of(back + 1))], unpack(carry)))

    carry = lax.fori_loop(0, i // 2, pair_body, carry)

    def last_body(_, carry):
        return store(group([(0, bias_of(i))], unpack(carry)))

    carry = lax.fori_loop(0, i % 2, last_body, carry)
    fin = unpack(carry)
    for p in range(n_pairs):
        finish(p, acc_ref[2 * p], fin[2 * p][1], fin[2 * p][0], acc_ref[2 * p + 1], fin[2 * p + 1][1], fin[2 * p + 1][0])


def _flash(q_arr, k_arr, v_arr, bias, *, batch, length, dil, tile, n_back, n_pairs, q_blk, k_blk, k_width,
           v_blk, v_width, per_pair_kv, heads, selb=None, with_lse=False):
    nq = length // tile
    with_sel = selb is not None
    width = n_pairs * LANES

    in_specs = [pl.BlockSpec((None, tile, width), lambda br, i: (br // dil, i, q_blk(br % dil))),
                pl.BlockSpec((None, length, k_width), lambda br, i: (br // dil, 0, k_blk(br % dil))),
                pl.BlockSpec((None, length, v_width), lambda br, i: (br // dil, 0, v_blk(br % dil))),
                pl.BlockSpec(bias.shape, lambda br, i: (0, 0, 0, 0))]
    args = [q_arr, k_arr, v_arr, bias]
    if with_sel:
        in_specs.append(pl.BlockSpec((None, tile, 2 * LANES), lambda br, i: (br, i, 0)))
        args.append(selb)
    o_map = lambda br, i: (br // dil, i, br % dil)
    out_specs = [pl.BlockSpec((None, tile, width), o_map)]
    out_shape = [jax.ShapeDtypeStruct((batch, length, dil * width), BF16)]
    if with_lse:
        out_specs.append(pl.BlockSpec((None, tile, width), o_map))
        out_shape.append(jax.ShapeDtypeStruct((batch, length, dil * width), F32))
    return pl.pallas_call(
        functools.partial(_flash_kernel, tile=tile, n_pairs=n_pairs, heads=heads, per_pair_kv=per_pair_kv,
                          n_back=n_back, n_bias=bias.shape[1], with_sel=with_sel, with_lse=with_lse),
        grid=(batch * dil, nq),
        in_specs=in_specs,
        out_specs=out_specs,
        out_shape=out_shape,
        scratch_shapes=[pltpu.VMEM((2 * n_pairs, tile, LANES), BF16)]
        + ([pltpu.VMEM((2 * n_pairs, LANES, tile), F32)] if n_back is None else []),
        compiler_params=_cparams(("arbitrary", "arbitrary")),
        name="flash_sel" if with_sel else ("flash_dil" if with_lse else "flash_win"),
    )(*args)


def _mix_kernel(x_ref, g1_ref, oc_ref, os_ref, ow_ref, gl_ref, ge_ref,
                od0_ref, od1_ref, od2_ref, l0_ref, l1_ref, l2_ref, ml_ref,
                wn_ref, wd_ref, wo_ref, o_ref):
    d = x_ref.shape[1]
    sig = 1.0 / (1.0 + jnp.exp(-gl_ref[...]))
    hi = sig.astype(BF16)
    lo = (sig - hi.astype(F32)).astype(BF16)
    ge = ge_ref[...]
    gates = jnp.dot(hi, ge, preferred_element_type=F32) + jnp.dot(lo, ge, preferred_element_type=F32)
    o_nsa = (gates[:, 0:512] * oc_ref[...].astype(F32)
             + gates[:, 512:1024] * os_ref[...].astype(F32)
             + gates[:, 1024:1536] * ow_ref[...].astype(F32))
    u_nsa = jnp.dot(o_nsa.astype(BF16), wn_ref[...], preferred_element_type=F32)

    lses = (l0_ref[...], l1_ref[...], l2_ref[...])
    mx = jnp.maximum(jnp.maximum(lses[0], lses[1]), lses[2])
    es = [jnp.exp2(l - mx) for l in lses]
    inv = 1.0 / (es[0] + es[1] + es[2])
    o_dil = (es[0] * od0_ref[...].astype(F32) + es[1] * od1_ref[...].astype(F32)
             + es[2] * od2_ref[...].astype(F32)) * inv
    u_dil = jnp.dot(o_dil.astype(BF16), wd_ref[...], preferred_element_type=F32)

    gm = 1.0 / (1.0 + jnp.exp(-ml_ref[...].astype(F32)))
    merged = gm[:, :d] * u_nsa + gm[:, d:] * u_dil
    y = jnp.dot(merged.astype(BF16), wo_ref[...], preferred_element_type=F32)
    o_ref[...] = x_ref[...] + g1_ref[...] * y


def _mix(x2, g1, oc, osel, ow, gl, ge, od, lse, ml, wn, wd, wo, seq):
    t, d = x2.shape
    tm = TM_PROJ
    per_b = seq // tm
    row = lambda i: (i, 0)
    const = lambda i: (0, 0)
    rows = lambda a: pl.BlockSpec((tm, a.shape[1]), row)
    full = lambda a: pl.BlockSpec(a.shape, const)
    return pl.pallas_call(
        _mix_kernel,
        grid=(t // tm,),
        in_specs=[rows(x2), pl.BlockSpec((None, 1, d), lambda i: (i // per_b, 0, 0)),
                  rows(oc), rows(osel), rows(ow), rows(gl), full(ge),
                  rows(od[0]), rows(od[1]), rows(od[2]), rows(lse[0]), rows(lse[1]), rows(lse[2]), rows(ml),
                  full(wn), full(wd), full(wo)],
        out_specs=pl.BlockSpec((tm, d), row),
        out_shape=jax.ShapeDtypeStruct((t, d), F32),
        compiler_params=_cparams(("arbitrary",)),
        name="mix_outproj",
    )(x2, g1, oc, osel, ow, gl, ge, *od, *lse, ml, wn, wd, wo)


def _router_kernel(x_ref, sc_ref, sh_ref, g_ref, wr_ref, br_ref, h_ref, eid_ref, wts_ref):
    h = _norm_mod(x_ref[...], g_ref[...], sc_ref[...], sh_ref[...])
    n_sub = h.shape[1] // LANES
    for j in range(n_sub):
        h_ref[pl.ds(j, h.shape[0], stride=n_sub), :] = h[:, j * LANES:(j + 1) * LANES]
    logit = lax.dot_general(wr_ref[...], h, (((1,), (1,)), ((), ())), preferred_element_type=F32,
                            precision=lax.Precision.HIGHEST) + br_ref[...]
    grp = jnp.zeros((1, h.shape[0]), jnp.int32)
    best = logit[0:1]
    for k in range(1, N_EXPERT_GROUPS):
        better = logit[k:k + 1] > best
        grp = jnp.where(better, k, grp)
        best = jnp.where(better, logit[k:k + 1], best)
    den = jnp.zeros_like(best)
    for k in range(N_EXPERT_GROUPS):
        den = den + jnp.exp(logit[k:k + 1] - best)
    p_grp = 1.0 / den
    le = logit[SUBLANES:SUBLANES + EXPERTS_PER_GROUP]
    for k in range(1, N_EXPERT_GROUPS):
        lo = SUBLANES + k * EXPERTS_PER_GROUP
        le = jnp.where(grp == k, logit[lo:lo + EXPERTS_PER_GROUP], le)
    rowi = lax.broadcasted_iota(jnp.int32, le.shape, 0)
    v1 = jnp.max(le, axis=0, keepdims=True)
    i1 = jnp.min(jnp.where(le == v1, rowi, EXPERTS_PER_GROUP), axis=0, keepdims=True)
    rest = jnp.where(rowi == i1, -3e38, le)
    v2 = jnp.max(rest, axis=0, keepdims=True)
    i2 = jnp.min(jnp.where(rest == v2, rowi, EXPERTS_PER_GROUP), axis=0, keepdims=True)
    e2 = jnp.exp(v2 - v1)
    inv = p_grp / (1.0 + e2)
    eid_ref[...] = jnp.concatenate([grp * EXPERTS_PER_GROUP + i1, grp * EXPERTS_PER_GROUP + i2], axis=0)
    wts_ref[...] = jnp.concatenate([inv, e2 * inv], axis=0)


def _router(x2, sc, sh, g, wr_t, br, seq):
    t, d = x2.shape
    tm = TM_PROJ
    per_b = seq // tm
    row = lambda i: (i, 0)
    const = lambda i: (0, 0)
    return pl.pallas_call(
        _router_kernel,
        grid=(t // tm,),
        in_specs=[pl.BlockSpec((tm, d), row),
                  pl.BlockSpec((None, 1, d), lambda i: (i // per_b, 0, 0)),
                  pl.BlockSpec((None, 1, d), lambda i: (i // per_b, 0, 0)),
                  pl.BlockSpec((1, d), const),
                  pl.BlockSpec(wr_t.shape, const),
                  pl.BlockSpec(br.shape, const)],
        out_specs=[pl.BlockSpec((tm * (d // LANES), LANES), row),
                   pl.BlockSpec((2, tm), lambda i: (0, i)),
                   pl.BlockSpec((2, tm), lambda i: (0, i))],
        out_shape=[jax.ShapeDtypeStruct((t * (d // LANES), LANES), F32),
                   jax.ShapeDtypeStruct((2, t), jnp.int32),
                   jax.ShapeDtypeStruct((2, t), F32)],
        compiler_params=_cparams(("arbitrary",)),
        name="norm_router",
    )(x2, sc, sh, g, wr_t, br)


def _moe_kernel(cnt_ref, off_ref, tok_ref, wt_ref, h_ref, w1_ref, w3_ref, w2_ref, o_ref,
                xs_ref, os_ref, xb_ref, y3_ref):
    c = pl.program_id(0)
    e = pl.program_id(1)
    n_sub = h_ref.shape[1]
    rb = xs_ref.shape[0] // n_sub

    @pl.when((c == 0) & (e == 0))
    def _():
        xs_ref[...] = jnp.zeros_like(xs_ref)

    @pl.when(e == 0)
    def _():
        o_ref[...] = jnp.zeros_like(o_ref)

    n = cnt_ref[c, e]
    off = off_ref[c, e]

    def unrolled(rows, fn):
        groups = rows // MOE_UNROLL

        def grp(g, _):
            for u in range(MOE_UNROLL):
                fn(g * MOE_UNROLL + u)
            return 0

        lax.fori_loop(0, groups, grp, 0)
        lax.fori_loop(groups * MOE_UNROLL, rows, lambda r, _: (fn(r), 0)[1], 0)

    def block(bi, _):
        base = off + bi * rb
        rows = jnp.minimum(rb, n - bi * rb)

        def gather(r):
            tkn = tok_ref[0, base + r]
            dst = pl.ds(pl.multiple_of(r * n_sub, n_sub), n_sub)
            xs_ref[dst, :] = h_ref[tkn]
            os_ref[dst, :] = o_ref[tkn]

        unrolled(rows, gather)
        for j in range(n_sub):
            xb_ref[:, j * LANES:(j + 1) * LANES] = xs_ref[pl.ds(j, rb, stride=n_sub), :].astype(BF16)
        xb = xb_ref[...]
        a = jnp.dot(xb, w1_ref[...], preferred_element_type=F32)
        b = jnp.dot(xb, w3_ref[...], preferred_element_type=F32)
        mid = (a * (1.0 / (1.0 + jnp.exp(-a))) * b).astype(BF16)
        y = jnp.dot(mid, w2_ref[...], preferred_element_type=F32)
        for j in range(n_sub):
            y3_ref[pl.ds(j, rb, stride=n_sub), :] = y[:, j * LANES:(j + 1) * LANES]

        def scatter(r):
            tkn = tok_ref[0, base + r]
            src = pl.ds(pl.multiple_of(r * n_sub, n_sub), n_sub)
            o_ref[tkn] = os_ref[src, :] + wt_ref[0, base + r] * y3_ref[src, :]

        unrolled(rows, scatter)
        return 0

    lax.fori_loop(0, (n + rb - 1) // rb, block, 0)


def _moe(h2, eid, wts, w1, w3, w2):
    n_sub = w1.shape[1] // LANES
    t, d = h2.shape[0] // n_sub, w1.shape[1]
    tc = min(MOE_CHUNK, t)
    n_chunks = t // tc
    slots = 2 * tc
    tok = jnp.arange(t, dtype=jnp.int32)
    key = ((tok // tc)[None, :] * N_EXPERTS + eid).reshape(-1)
    order = jnp.argsort(key)
    tok_sorted = (jnp.tile(tok % tc, 2)[order]).reshape(n_chunks, 1, slots)
    w_sorted = wts.reshape(-1)[order].reshape(n_chunks, 1, slots)
    counts = jnp.zeros((n_chunks * N_EXPERTS,), jnp.int32).at[key].add(1).reshape(n_chunks, N_EXPERTS)
    offs = jnp.cumsum(counts, axis=1) - counts

    grid_spec = pltpu.PrefetchScalarGridSpec(
        num_scalar_prefetch=2,
        grid=(n_chunks, N_EXPERTS),
        in_specs=[pl.BlockSpec((None, 1, slots), lambda c, e, *_: (c, 0, 0), memory_space=pltpu.SMEM),
                  pl.BlockSpec((None, 1, slots), lambda c, e, *_: (c, 0, 0), memory_space=pltpu.SMEM),
                  pl.BlockSpec((tc, n_sub, LANES), lambda c, e, *_: (c, 0, 0)),
                  pl.BlockSpec((None, d, D_EXPERT), lambda c, e, *_: (e, 0, 0)),
                  pl.BlockSpec((None, d, D_EXPERT), lambda c, e, *_: (e, 0, 0)),
                  pl.BlockSpec((None, D_EXPERT, d), lambda c, e, *_: (e, 0, 0))],
        out_specs=pl.BlockSpec((tc, n_sub, LANES), lambda c, e, *_: (c, 0, 0)),
        scratch_shapes=[pltpu.VMEM((MOE_ROWS * n_sub, LANES), F32),
                        pltpu.VMEM((MOE_ROWS * n_sub, LANES), F32),
                        pltpu.VMEM((MOE_ROWS, d), BF16),
                        pltpu.VMEM((MOE_ROWS * n_sub, LANES), F32)],
    )
    out = pl.pallas_call(
        _moe_kernel,
        grid_spec=grid_spec,
        out_shape=jax.ShapeDtypeStruct((t, n_sub, LANES), F32),
        compiler_params=_cparams(("arbitrary", "arbitrary")),
        name="moe_experts",
    )(counts, offs, tok_sorted, w_sorted, h2.reshape(t, n_sub, LANES), w1, w3, w2)
    return out.reshape(t * n_sub, LANES)


def _resid_kernel(x_ref, y_ref, g_ref, nf_ref, o_ref, *, final):
    tm, d = x_ref.shape
    n_sub = d // LANES
    y = jnp.concatenate([y_ref[pl.ds(j, tm, stride=n_sub), :] for j in range(n_sub)], axis=1)
    x = x_ref[...] + g_ref[...] * y
    if final:
        ms = jnp.mean(x * x, axis=-1, keepdims=True)
        x = x * lax.rsqrt(ms + RMS_EPS) * nf_ref[...]
    o_ref[...] = x


def _residual(x2, y2, g2, norm_f, seq, final):
    t, d = x2.shape
    tm = TM_PROJ
    per_b = seq // tm
    row = lambda i: (i, 0)
    return pl.pallas_call(
        functools.partial(_resid_kernel, final=final),
        grid=(t // tm,),
        in_specs=[pl.BlockSpec((tm, d), row), pl.BlockSpec((tm * (d // LANES), LANES), row),
                  pl.BlockSpec((None, 1, d), lambda i: (i // per_b, 0, 0)),
                  pl.BlockSpec((1, d), lambda i: (0, 0))],
        out_specs=pl.BlockSpec((tm, d), row),
        out_shape=jax.ShapeDtypeStruct((t, d), F32),
        compiler_params=_cparams(("arbitrary",)),
        name="residual_final" if final else "residual",
    )(x2, y2, g2, norm_f)


def _split_w_in(w_in, d):
    scale = HEAD_DIM ** -0.5 * LOG2E
    nq = NSA_HEADS * HEAD_DIM
    nkv = 3 * 2 * NSA_KV_HEADS * HEAD_DIM
    ngate = 3 * NSA_HEADS
    ndil = 3 * N_DIL_GROUPS * DIL_HEADS_PER_GROUP * HEAD_DIM
    o1, o2, o3 = nq, nq + nkv, nq + nkv + ngate
    o4 = o3 + ndil
    wq = (w_in[:, :o1] * scale).reshape(d, NSA_KV_HEADS, NSA_GROUP, HEAD_DIM)
    wq = wq.transpose(0, 2, 1, 3).reshape(d, nq)
    wkv = w_in[:, o1:o2]
    wg = jnp.pad(w_in[:, o2:o3], ((0, 0), (0, LANES - ngate)))
    gw = DIL_HEADS_PER_GROUP * HEAD_DIM
    per_which = N_DIL_GROUPS * gw
    wds = []
    for grp in range(N_DIL_GROUPS):
        parts = [w_in[:, o3 + which * per_which + grp * gw: o3 + which * per_which + (grp + 1) * gw]
                 for which in range(3)]
        parts[0] = parts[0] * scale
        wds.append(jnp.concatenate(parts, axis=1))
    wm = w_in[:, o4:]
    cast = lambda w: w.astype(BF16)
    return [cast(wq), cast(wkv), cast(wds[0]), cast(wds[1]), cast(wds[2]), cast(wm), cast(wg)]


def kernel(x, c, rel_bias, ada_w, ada_b, norm1, norm2, w_in, cmp_pos, cmp_w1, cmp_w2, w_up_nsa, w_up_dil, w_o,
           router_wg, router_bg, router_we, router_be, exp_w1, exp_w3, exp_w2, norm_f):
    batch, seq, d = x.shape
    depth = ada_w.shape[0]
    t = batch * seq
    n_cmp = seq // CMP_STRIDE
    n_sel = seq // SEL_BLOCK
    assert seq % TQ_NSA == 0 and n_sel <= HEAD_DIM and n_sel >= SEL_TOPN
    assert all(seq % (dil * TQ_DIL) == 0 for _, dil in DIL_PAIRS)

    mod = _modulation(c, ada_w, ada_b)

    nq_nsa = seq // TQ_NSA
    bias_cmp = _expand_bias(rel_bias, _cmp_buckets(seq, n_cmp), 0, NSA_HEADS, 8 * SUBLANES)
    sel_b = _toeplitz_buckets(nq_nsa, TQ_NSA, seq, 1)
    n_sel_bias = nq_nsa
    while n_sel_bias > 1 and (sel_b[n_sel_bias - 2:] == sel_b[n_sel_bias - 1, 0, 0]).all():
        n_sel_bias -= 1
    sel_b = sel_b[:n_sel_bias].reshape(n_sel_bias * TQ_NSA, TQ_NSA)
    bias_sel = _expand_bias(rel_bias, sel_b, 0, NSA_HEADS, TQ_NSA).reshape(NSA_HEADS, n_sel_bias, TQ_NSA, TQ_NSA)
    nb_win = min(-(-NSA_WINDOW // TQ_NSA), nq_nsa - 1)
    win_b = _toeplitz_buckets(nb_win + 1, TQ_NSA, NSA_WINDOW, 1, masked_tail=True).reshape(-1, TQ_NSA)
    bias_win = _expand_bias(rel_bias, win_b, 0, NSA_HEADS, TQ_NSA).reshape(NSA_HEADS, nb_win + 2, TQ_NSA, TQ_NSA)
    bias_dil, nb_dil = [], []
    for grp, (window, dil) in enumerate(DIL_PAIRS):
        nb = min(-(-(window // dil) // TQ_DIL), seq // dil // TQ_DIL - 1)
        bk = _toeplitz_buckets(nb + 1, TQ_DIL, window // dil, dil, masked_tail=True).reshape(-1, TQ_DIL)
        hb = NSA_HEADS + grp * DIL_HEADS_PER_GROUP
        bias_dil.append(_expand_bias(rel_bias, bk, hb, DIL_HEADS_PER_GROUP, TQ_DIL)
                        .reshape(DIL_HEADS_PER_GROUP, nb + 2, TQ_DIL, TQ_DIL))
        nb_dil.append(nb)
    ovl_t = jnp.asarray(_overlap_t(n_cmp, n_sel), BF16)
    onehot = jnp.asarray(_block_onehot(seq), BF16)
    gate_e = jnp.asarray(_gate_expand(), BF16)

    x2 = x.reshape(t, d)
    for l in range(depth):
        sh1, sc1, g1, sh2, sc2, g2 = [m.reshape(batch, 1, d) for m in jnp.split(mod[l], 6, axis=-1)]
        weights = _split_w_in(w_in[l], d)
        (q_n, kc_in, vc_in, k_sel, v_sel, k_win, v_win, qkv_d0, qkv_d1, qkv_d2, merge_l, gate_l) = _inproj(
            x2, sc1, sh1, norm1[l].reshape(1, d), onehot, weights, seq)

        kc = _compress(kc_in, cmp_pos[l, 0], cmp_w1[l, 0], cmp_w2[l, 0], batch, seq, transpose_out=False)
        vct = _compress(vc_in, cmp_pos[l, 1], cmp_w1[l, 1], cmp_w2[l, 1], batch, seq, transpose_out=True)
        o_c, selb = _cmp_select(q_n, kc, vct, bias_cmp, ovl_t, batch, seq)
        nsa_common = dict(batch=batch, length=seq, dil=1, tile=TQ_NSA, n_pairs=NSA_GROUP, q_blk=lambda r: 0,
                          k_blk=lambda r: 0, v_blk=lambda r: 0, v_width=LANES, per_pair_kv=False,
                          heads=tuple((g, NSA_GROUP + g) for g in range(NSA_GROUP)))
        q3 = q_n.reshape(batch, seq, 512)
        (o_s,) = _flash(q3, k_sel.reshape(batch, seq, 2 * LANES), v_sel.reshape(batch, seq, LANES), bias_sel,
                        n_back=None, k_width=2 * LANES, selb=selb.reshape(batch, seq, 2 * LANES), **nsa_common)
        (o_w,) = _flash(q3, k_win.reshape(batch, seq, LANES), v_win.reshape(batch, seq, LANES), bias_win,
                        n_back=nb_win, k_width=LANES, **nsa_common)

        o_d, lse_d = [], []
        for grp, ((window, dil), qkv) in enumerate(zip(DIL_PAIRS, (qkv_d0, qkv_d1, qkv_d2))):
            length = seq // dil
            view = qkv.reshape(batch, length, dil * 768)
            o_g, lse_g = _flash(view, view, view, bias_dil[grp], batch=batch, length=length, dil=dil, tile=TQ_DIL,
                                n_back=nb_dil[grp], n_pairs=2, q_blk=lambda r: 3 * r, k_blk=lambda r: 3 * r + 1,
                                k_width=2 * LANES, v_blk=lambda r: 3 * r + 2, v_width=2 * LANES,
                                per_pair_kv=True, heads=((0, 1), (2, 3)), with_lse=True)
            o_d.append(o_g.reshape(t, 256))
            lse_d.append(lse_g.reshape(t, 256))

        wn = w_up_nsa[l].reshape(NSA_KV_HEADS, NSA_GROUP, HEAD_DIM, d).transpose(1, 0, 2, 3).reshape(512, d)
        x2 = _mix(x2, g1, o_c, o_s.reshape(t, 512), o_w.reshape(t, 512), gate_l, gate_e, o_d, lse_d, merge_l,
                  wn.astype(BF16), w_up_dil[l].astype(BF16), w_o[l].astype(BF16), seq)

        wr_t = jnp.concatenate([jnp.pad(router_wg[l], ((0, 0), (0, SUBLANES - N_EXPERT_GROUPS))),
                                router_we[l]], axis=1).T
        br = jnp.concatenate([jnp.pad(router_bg[l], (0, SUBLANES - N_EXPERT_GROUPS)),
                              router_be[l]]).reshape(-1, 1)
        h2, eid, wts = _router(x2, sc2, sh2, norm2[l].reshape(1, d), wr_t, br, seq)
        y = _moe(h2, eid, wts, exp_w1[l].astype(BF16), exp_w3[l].astype(BF16), exp_w2[l].astype(BF16))
        x2 = _residual(x2, y, g2, norm_f.reshape(1, d), seq, final=(l == depth - 1))
    return x2.reshape(batch, seq, d)
```

```python
import functools
import math

import numpy as np
import jax
import jax.numpy as jnp
from jax import lax
from jax.experimental import pallas as pl
from jax.experimental.pallas import tpu as pltpu

F32 = jnp.float32
BF16 = jnp.bfloat16

HEAD_DIM = 64
NSA_HEADS = 8
NSA_KV_HEADS = 2
NSA_GROUP = NSA_HEADS // NSA_KV_HEADS
CMP_LEN = 32
CMP_STRIDE = 16
CMP_HIDDEN = 256
SEL_BLOCK = 64
SEL_TOPN = 16
NSA_WINDOW = 512
FORCE_SCORE = 1e4
DIL_PAIRS = ((128, 1), (512, 4), (2048, 16))
N_DIL_GROUPS = 3
DIL_HEADS_PER_GROUP = 4
NUM_BUCKETS = 32
REL_MAX_DIST = 2048
N_EXPERT_GROUPS = 4
EXPERTS_PER_GROUP = 8
N_EXPERTS = N_EXPERT_GROUPS * EXPERTS_PER_GROUP
D_EXPERT = 512
RMS_EPS = 1e-6

LOG2E = math.log2(math.e)
LANES = 128
SUBLANES = 8
MASK_NEG = -1e30
SEL_NEG = -1e9
TQ_NSA = 256
TQ_DIL = 128
TM_PROJ = 512
FLASH_LOOKAHEAD = 4
MOE_CHUNK = 2048
MOE_ROWS = 128
MOE_UNROLL = 16
VMEM_LIMIT = 56 * 1024 * 1024


def _cparams(sem):
    return pltpu.CompilerParams(dimension_semantics=sem, vmem_limit_bytes=VMEM_LIMIT)


def _np_bucket(dist):
    dist = np.maximum(dist, 0)
    max_exact = NUM_BUCKETS // 2
    df = np.maximum(dist, 1).astype(np.float32)
    val = np.log(df / np.float32(max_exact)) / np.float32(math.log(REL_MAX_DIST / max_exact))
    large = max_exact + (val * np.float32(NUM_BUCKETS - max_exact)).astype(np.int32)
    large = np.minimum(large, NUM_BUCKETS - 1)
    return np.where(dist < max_exact, dist, large).astype(np.int32)


def _toeplitz_buckets(n_delta, tile, window, dist_scale, masked_tail=False):
    dd = np.arange(n_delta)[:, None, None]
    r = np.arange(tile)[None, :, None]
    c = np.arange(tile)[None, None, :]
    dist = dd * tile + c - r
    valid = (dist >= 0) & (dist <= window)
    out = np.where(valid, _np_bucket(dist * dist_scale), -1).astype(np.int32)
    if masked_tail:
        out = np.concatenate([out, np.full((1, tile, tile), -1, np.int32)], axis=0)
    return out


def _cmp_buckets(seq, n_cmp):
    c_end = np.arange(n_cmp)[:, None] * CMP_STRIDE + CMP_LEN - 1
    t = np.arange(seq)[None, :]
    dist = t - c_end
    return np.where(dist >= 0, _np_bucket(dist), -1).astype(np.int32)


def _overlap_t(n_cmp, n_sel):
    c_start = np.arange(n_cmp)[None, :] * CMP_STRIDE
    s_start = np.arange(n_sel)[:, None] * SEL_BLOCK
    ov = np.clip(np.minimum(c_start + CMP_LEN, s_start + SEL_BLOCK) - np.maximum(c_start, s_start), 0, None)
    out = np.zeros((LANES, n_cmp), np.float32)
    out[64:64 + n_sel] = ov.astype(np.float32) / CMP_STRIDE
    return out


def _block_onehot(seq):
    oh = np.zeros((seq, LANES), np.float32)
    blk = np.arange(seq) // SEL_BLOCK
    oh[np.arange(seq), blk] = 1.0
    oh[np.arange(seq), 64 + blk] = 1.0
    return oh


def _gate_expand():
    e = np.zeros((LANES, 3 * NSA_HEADS * HEAD_DIM), np.float32)
    for br in range(3):
        for g in range(NSA_GROUP):
            for ln in range(LANES):
                kv = ln // HEAD_DIM
                e[br * NSA_HEADS + kv * NSA_GROUP + g, br * 512 + g * LANES + ln] = 1.0
    return e


def _mod_kernel(c_ref, w_ref, b_ref, o_ref):
    c = c_ref[...]
    cond = c * (1.0 / (1.0 + jnp.exp(-c)))
    o_ref[...] = jnp.dot(cond, w_ref[...], preferred_element_type=F32,
                         precision=lax.Precision.HIGHEST) + b_ref[...]


def _modulation(c, ada_w, ada_b):
    depth, d, n = ada_w.shape
    b = c.shape[0]
    tn = 1536
    return pl.pallas_call(
        _mod_kernel,
        grid=(depth, n // tn),
        in_specs=[pl.BlockSpec((b, d), lambda l, j: (0, 0)),
                  pl.BlockSpec((None, d, tn), lambda l, j: (l, 0, j)),
                  pl.BlockSpec((None, 1, tn), lambda l, j: (l, 0, j))],
        out_specs=pl.BlockSpec((None, b, tn), lambda l, j: (l, 0, j)),
        out_shape=jax.ShapeDtypeStruct((depth, b, n), F32),
        compiler_params=_cparams(("arbitrary", "arbitrary")),
        name="adaln_modulation",
    )(c, ada_w, ada_b.reshape(depth, 1, n))


def _bias_kernel(tbl_ref, bkt_ref, o_ref, *, head_base):
    h = pl.program_id(0) + head_base
    bkt = bkt_ref[...]
    acc = jnp.full(bkt.shape, MASK_NEG, F32)
    for b in range(NUM_BUCKETS):
        acc = jnp.where(bkt == b, tbl_ref[b, h] * LOG2E, acc)
    o_ref[...] = acc


def _expand_bias(rel_bias, buckets, head_base, n_heads, row_tile):
    rows, cols = buckets.shape
    return pl.pallas_call(
        functools.partial(_bias_kernel, head_base=head_base),
        grid=(n_heads, rows // row_tile),
        in_specs=[pl.BlockSpec(memory_space=pltpu.SMEM),
                  pl.BlockSpec((row_tile, cols), lambda h, i: (i, 0))],
        out_specs=pl.BlockSpec((None, row_tile, cols), lambda h, i: (h, i, 0)),
        out_shape=jax.ShapeDtypeStruct((n_heads, rows, cols), F32),
        compiler_params=_cparams(("arbitrary", "arbitrary")),
        name="bias_expand",
    )(rel_bias, jnp.asarray(buckets))


def _norm_mod(x, g, sc, sh):
    ms = jnp.mean(x * x, axis=-1, keepdims=True)
    y = x * lax.rsqrt(ms + RMS_EPS) * g
    return y * (1.0 + sc) + sh


def _inproj_kernel(x_ref, sc_ref, sh_ref, g_ref, oh_ref, wq_ref, wkv_ref, wd0_ref, wd1_ref, wd2_ref, wm_ref, wg_ref,
                   q_ref, kc_ref, vc_ref, ks_ref, vs_ref, kw_ref, vw_ref, d0_ref, d1_ref, d2_ref, m_ref, gl_ref):
    h = _norm_mod(x_ref[...], g_ref[...], sc_ref[...], sh_ref[...]).astype(BF16)

    def proj(w_ref):
        return jnp.dot(h, w_ref[...], preferred_element_type=F32)

    q_ref[...] = proj(wq_ref).astype(BF16)
    kv = proj(wkv_ref).astype(BF16)
    for k, ref in ((0, kc_ref), (1, vc_ref), (3, vs_ref), (4, kw_ref), (5, vw_ref)):
        ref[...] = kv[:, k * LANES:(k + 1) * LANES]
    k_sel = kv[:, 2 * LANES:3 * LANES]
    oh = oh_ref[...]
    lo = lax.broadcasted_iota(jnp.int32, k_sel.shape, 1) < HEAD_DIM
    ks_ref[:, :LANES] = jnp.where(lo, k_sel, oh)
    ks_ref[:, LANES:] = jnp.where(lo, oh, k_sel)
    d0_ref[...] = proj(wd0_ref).astype(BF16)
    d1_ref[...] = proj(wd1_ref).astype(BF16)
    d2_ref[...] = proj(wd2_ref).astype(BF16)
    m_ref[...] = proj(wm_ref).astype(BF16)
    gl_ref[...] = proj(wg_ref)


def _inproj(x2, sc, sh, g, onehot, weights, seq):
    t, d = x2.shape
    tm = TM_PROJ
    per_b = seq // tm
    widths = (512, 128, 128, 256, 128, 128, 128, 768, 768, 768, 2 * d, LANES)
    dtypes = (BF16,) * 11 + (F32,)
    row = lambda i: (i, 0)
    const = lambda i: (0, 0)
    in_specs = [pl.BlockSpec((tm, d), row),
                pl.BlockSpec((None, 1, d), lambda i: (i // per_b, 0, 0)),
                pl.BlockSpec((None, 1, d), lambda i: (i // per_b, 0, 0)),
                pl.BlockSpec((1, d), const),
                pl.BlockSpec((tm, LANES), lambda i: (i % per_b, 0))]
    in_specs += [pl.BlockSpec(w.shape, const) for w in weights]
    return pl.pallas_call(
        _inproj_kernel,
        grid=(t // tm,),
        in_specs=in_specs,
        out_specs=[pl.BlockSpec((tm, w), row) for w in widths],
        out_shape=[jax.ShapeDtypeStruct((t, w), dt) for w, dt in zip(widths, dtypes)],
        compiler_params=_cparams(("arbitrary",)),
        name="norm_inproj",
    )(x2, sc, sh, g, onehot, *weights)


def _gelu_tanh(x):
    return 0.5 * x * (1.0 + jnp.tanh(math.sqrt(2.0 / math.pi) * (x + 0.044715 * (x * x * x))))


def _compress_kernel(r_ref, pt_ref, pb_ref, wt_ref, wb_ref, w2_ref, o_ref, *, transpose_out):
    r = r_ref[...].astype(F32)
    top = jnp.dot((r + pt_ref[...]).astype(BF16), wt_ref[...], preferred_element_type=F32)
    bot = jnp.dot((r + pb_ref[...]).astype(BF16), wb_ref[...], preferred_element_type=F32)
    n = bot.shape[0]
    hid = top + pltpu.roll(bot, n - 1, 0)
    act = _gelu_tanh(hid).astype(BF16)
    if transpose_out:
        o_ref[...] = lax.dot_general(w2_ref[...], act, (((1,), (1,)), ((), ())),
                                     preferred_element_type=F32).astype(BF16)
    else:
        o_ref[...] = jnp.dot(act, w2_ref[...], preferred_element_type=F32).astype(BF16)


def _compress(tok, pos, w1, w2, batch, seq, transpose_out):
    nc = seq // CMP_STRIDE
    half = CMP_LEN // 2
    eye = jnp.eye(NSA_KV_HEADS, dtype=F32)
    w1r = w1.reshape(CMP_LEN, HEAD_DIM, CMP_HIDDEN)
    blk = lambda w: jnp.einsum('ldn,hg->lhdgn', w, eye).reshape(half * LANES, 2 * CMP_HIDDEN).astype(BF16)
    wt, wb = blk(w1r[:half]), blk(w1r[half:])
    posr = lambda p: jnp.broadcast_to(p[:, None, :], (half, NSA_KV_HEADS, HEAD_DIM)).reshape(1, half * LANES)
    pt, pb = posr(pos[:half]), posr(pos[half:])
    w2b = jnp.einsum('nd,hg->hngd', w2, eye).reshape(2 * CMP_HIDDEN, LANES)
    if transpose_out:
        w2b = w2b.T
        out_block, out_shape = (None, LANES, nc), (batch, LANES, nc)
    else:
        out_block, out_shape = (None, nc, LANES), (batch, nc, LANES)
    w2b = w2b.astype(BF16)
    const = lambda b: (0, 0)
    return pl.pallas_call(
        functools.partial(_compress_kernel, transpose_out=transpose_out),
        grid=(batch,),
        in_specs=[pl.BlockSpec((None, nc, half * LANES), lambda b: (b, 0, 0)),
                  pl.BlockSpec(pt.shape, const), pl.BlockSpec(pb.shape, const),
                  pl.BlockSpec(wt.shape, const), pl.BlockSpec(wb.shape, const),
                  pl.BlockSpec(w2b.shape, const)],
        out_specs=pl.BlockSpec(out_block, lambda b: (b, 0, 0)),
        out_shape=jax.ShapeDtypeStruct(out_shape, BF16),
        compiler_params=_cparams(("arbitrary",)),
        name="nsa_compress",
    )(tok.reshape(batch, nc, half * LANES), pt, pb, wt, wb, w2b)


def _cmp_select_kernel(q_ref, kc_ref, vct_ref, bias_ref, ovl_ref, o_ref, selb_ref, *, n_sel, n_top):
    tq = q_ref.shape[0]
    nc = kc_ref.shape[0]
    qs = pl.program_id(1) * tq
    kc = kc_ref[...]
    vct = vct_ref[...]
    lane_q = lax.broadcasted_iota(jnp.int32, (tq, LANES), 1)
    row_o = lax.broadcasted_iota(jnp.int32, (LANES, tq), 0)
    heads = [(g, kv) for g in range(NSA_GROUP) for kv in range(NSA_KV_HEADS)]

    def scores(g, kv):
        qt = q_ref[:, g * LANES:(g + 1) * LANES]
        mine = (lane_q < HEAD_DIM) if kv == 0 else (lane_q >= HEAD_DIM)
        qm = jnp.where(mine, qt, jnp.zeros_like(qt))
        s = lax.dot_general(kc, qm, (((1,), (1,)), ((), ())), preferred_element_type=F32)
        return s + bias_ref[kv * NSA_GROUP + g]

    pc_sum = [jnp.zeros((nc, tq), F32) for _ in range(NSA_KV_HEADS)]
    outs = {}
    ahead = [scores(*hd) for hd in heads[:FLASH_LOOKAHEAD]]
    for n, (g, kv) in enumerate(heads):
        s = ahead.pop(0)
        if n + FLASH_LOOKAHEAD < len(heads):
            ahead.append(scores(*heads[n + FLASH_LOOKAHEAD]))
        m = jnp.max(s, axis=0, keepdims=True)
        m = jnp.where(m < 0.5 * MASK_NEG, 0.0, m)
        p = jnp.exp2(s - m)
        den = jnp.sum(p, axis=0, keepdims=True)
        pc = p * (1.0 / jnp.where(den > 0.0, den, 1.0))
        pc_sum[kv] = pc_sum[kv] + pc
        outs[kv] = jnp.dot(vct, pc.astype(BF16), preferred_element_type=F32)
        if kv == NSA_KV_HEADS - 1:
            o_t = jnp.where(row_o < HEAD_DIM, outs[0], outs[1])
            o_ref[:, g * LANES:(g + 1) * LANES] = o_t.T.astype(BF16)

    ovl = ovl_ref[...]
    imps = []
    for kv in range(NSA_KV_HEADS):
        hi = pc_sum[kv].astype(BF16)
        lo = (pc_sum[kv] - hi.astype(F32)).astype(BF16)
        imps.append((jnp.dot(ovl, hi, preferred_element_type=F32)
                     + jnp.dot(ovl, lo, preferred_element_type=F32))[HEAD_DIM:])
    imp = jnp.concatenate(imps, axis=1)
    wide = (HEAD_DIM, NSA_KV_HEADS * tq)
    rowj = lax.broadcasted_iota(jnp.int32, wide, 0)
    col = lax.broadcasted_iota(jnp.int32, wide, 1)
    t = qs + jnp.where(col >= tq, col - tq, col)
    jq = jnp.right_shift(t, SEL_BLOCK.bit_length() - 1)
    forced = (rowj == 0) | (rowj == jq) | (rowj == jq - 1)
    score = jnp.where(forced, FORCE_SCORE, jnp.where(rowj > jq, -1.0, imp))
    rem = jnp.where(rowj < n_sel, score, -3e38)
    sel = jnp.zeros(wide, F32)
    for _ in range(n_top):
        m = jnp.max(rem, axis=0, keepdims=True)
        idx = jnp.min(jnp.where(rem == m, rowj, HEAD_DIM), axis=0, keepdims=True)
        pick = rowj == idx
        sel = jnp.where(pick, 1.0, sel)
        rem = jnp.where(pick, -3e38, rem)
    sb = jnp.where(sel > 0.5, 0.0, SEL_NEG)
    zero = jnp.zeros((HEAD_DIM, tq), F32)
    selb_ref[:, :LANES] = jnp.concatenate([zero, sb[:, :tq]], axis=0).T.astype(BF16)
    selb_ref[:, LANES:] = jnp.concatenate([sb[:, tq:], zero], axis=0).T.astype(BF16)


def _cmp_select(q, kc, vct, bias_c, ovl_t, batch, seq):
    t = q.shape[0]
    tq = TQ_NSA
    nq = seq // tq
    nc = seq // CMP_STRIDE
    n_sel = seq // SEL_BLOCK
    n_top = min(SEL_TOPN, n_sel)
    return pl.pallas_call(
        functools.partial(_cmp_select_kernel, n_sel=n_sel, n_top=n_top),
        grid=(batch, nq),
        in_specs=[pl.BlockSpec((tq, 512), lambda b, i: (b * nq + i, 0)),
                  pl.BlockSpec((None, nc, LANES), lambda b, i: (b, 0, 0)),
                  pl.BlockSpec((None, LANES, nc), lambda b, i: (b, 0, 0)),
                  pl.BlockSpec((NSA_HEADS, nc, tq), lambda b, i: (0, 0, i)),
                  pl.BlockSpec((LANES, nc), lambda b, i: (0, 0))],
        out_specs=[pl.BlockSpec((tq, 512), lambda b, i: (b * nq + i, 0)),
                   pl.BlockSpec((tq, 2 * LANES), lambda b, i: (b * nq + i, 0))],
        out_shape=[jax.ShapeDtypeStruct((t, 512), BF16),
                   jax.ShapeDtypeStruct((t, 2 * LANES), BF16)],
        compiler_params=_cparams(("arbitrary", "arbitrary")),
        name="nsa_cmp_select",
    )(q, kc, vct, bias_c, ovl_t)


def _flash_kernel(*refs, tile, n_pairs, heads, per_pair_kv, n_back, n_bias, with_sel, with_lse):
    it = iter(refs)
    q_ref, k_ref, v_ref, b_ref = (next(it) for _ in range(4))
    selb_ref = next(it) if with_sel else None
    o_ref = next(it)
    lse_ref = next(it) if with_lse else None
    qa_ref = next(it)
    acc_ref = next(it) if n_back is None else None

    i = pl.program_id(1)
    n_heads = 2 * n_pairs
    lo = lax.broadcasted_iota(jnp.int32, (tile, LANES), 1) < HEAD_DIM
    for p in range(n_pairs):
        q = q_ref[:, p * LANES:(p + 1) * LANES]
        if with_sel:
            qa_ref[2 * p] = jnp.where(lo, q, selb_ref[:, :LANES])
            qa_ref[2 * p + 1] = jnp.where(lo, selb_ref[:, LANES:], q)
        else:
            zero = jnp.zeros_like(q)
            qa_ref[2 * p] = jnp.where(lo, q, zero)
            qa_ref[2 * p + 1] = jnp.where(lo, zero, q)
    nt = (((1,), (1,)), ((), ()))
    tn = (((0,), (0,)), ((), ()))
    top = lax.broadcasted_iota(jnp.int32, (LANES, tile), 0) < HEAD_DIM

    def finish(p, acc0, l0, m0, acc1, l1, m1):
        o_t = jnp.where(top, acc0 * (1.0 / l0), acc1 * (1.0 / l1))
        o_ref[:, p * LANES:(p + 1) * LANES] = o_t.T.astype(o_ref.dtype)
        if with_lse:
            lse_t = jnp.where(top, m0 + jnp.log2(l0), m1 + jnp.log2(l1))
            lse_ref[:, p * LANES:(p + 1) * LANES] = lse_t.T

    def scores(hd, tiles):
        p, half = divmod(hd, 2)
        kc = half if with_sel else (p if per_pair_kv else 0)
        parts = []
        for j, bidx in tiles:
            kt = k_ref[pl.ds(pl.multiple_of(j * tile, tile), tile), kc * LANES:(kc + 1) * LANES]
            s = lax.dot_general(kt, qa_ref[hd], nt, preferred_element_type=F32)
            parts.append(s + b_ref[heads[p][half], bidx])
        return parts

    def group(tiles, state):
        res = []
        ahead = [scores(hd, tiles) for hd in range(min(FLASH_LOOKAHEAD, n_heads))]
        for hd in range(n_heads):
            parts = ahead.pop(0)
            if hd + FLASH_LOOKAHEAD < n_heads:
                ahead.append(scores(hd + FLASH_LOOKAHEAD, tiles))
            m_new = functools.reduce(jnp.maximum, [jnp.max(s, axis=0, keepdims=True) for s in parts])
            if state is not None:
                m, l = state[hd]
                m_new = jnp.maximum(m, m_new)
                alpha = jnp.exp2(m - m_new)
            prs = [jnp.exp2(s - m_new) for s in parts]
            l_new = functools.reduce(jnp.add, [jnp.sum(pr, axis=0, keepdims=True) for pr in prs])
            vc = (hd // 2) if per_pair_kv else 0
            pv = None
            for (j, _), pr in zip(tiles, prs):
                vt = v_ref[pl.ds(pl.multiple_of(j * tile, tile), tile), vc * LANES:(vc + 1) * LANES]
                d = lax.dot_general(vt, pr.astype(BF16), tn, preferred_element_type=F32)
                pv = d if pv is None else pv + d
            if state is not None:
                l_new = alpha * l + l_new
                pv = alpha * acc_ref[hd] + pv
            res.append((m_new, l_new, pv))
        return res

    if n_back is not None:
        tiles = [(jnp.maximum(i - k, 0), jnp.where(i >= k, k, n_bias - 1)) for k in range(n_back + 1)]
        res = group(tiles, None)
        for p in range(n_pairs):
            (m0, l0, a0), (m1, l1, a1) = res[2 * p], res[2 * p + 1]
            finish(p, a0, l0, m0, a1, l1, m1)
        return

    def store(res):
        for hd, (_, _, acc) in enumerate(res):
            acc_ref[hd] = acc
        return tuple(x for m, l, _ in res for x in (m, l))

    def unpack(carry):
        return [(carry[2 * hd], carry[2 * hd + 1]) for hd in range(n_heads)]

    bias_of = lambda back: jnp.minimum(back, n_bias - 1)
    carry = store(group([(i, 0)], None))

    def pair_body(step, carry):
        back = 2 * step + 1
        return store(group([(i - back, bias_of(back)), (i - back - 1, bias_of(back + 1))], unpack(carry)))

    carry = lax.fori_loop(0, i // 2, pair_body, carry)

    def last_body(_, carry):
        return store(group([(0, bias_of(i))], unpack(carry)))

    carry = lax.fori_loop(0, i % 2, last_body, carry)
    fin = unpack(carry)
    for p in range(n_pairs):
        finish(p, acc_ref[2 * p], fin[2 * p][1], fin[2 * p][0], acc_ref[2 * p + 1], fin[2 * p + 1][1], fin[2 * p + 1][0])


def _flash(q_arr, k_arr, v_arr, bias, *, batch, length, dil, tile, n_back, n_pairs, q_blk, k_blk, k_width,
           v_blk, v_width, per_pair_kv, heads, selb=None, with_lse=False):
    nq = length // tile
    with_sel = selb is not None
    width = n_pairs * LANES

    in_specs = [pl.BlockSpec((None, tile, width), lambda br, i: (br // dil, i, q_blk(br % dil))),
                pl.BlockSpec((None, length, k_width), lambda br, i: (br // dil, 0, k_blk(br % dil))),
                pl.BlockSpec((None, length, v_width), lambda br, i: (br // dil, 0, v_blk(br % dil))),
                pl.BlockSpec(bias.shape, lambda br, i: (0, 0, 0, 0))]
    args = [q_arr, k_arr, v_arr, bias]
    if with_sel:
        in_specs.append(pl.BlockSpec((None, tile, 2 * LANES), lambda br, i: (br, i, 0)))
        args.append(selb)
    o_map = lambda br, i: (br // dil, i, br % dil)
    out_specs = [pl.BlockSpec((None, tile, width), o_map)]
    out_shape = [jax.ShapeDtypeStruct((batch, length, dil * width), BF16)]
    if with_lse:
        out_specs.append(pl.BlockSpec((None, tile, width), o_map))
        out_shape.append(jax.ShapeDtypeStruct((batch, length, dil * width), F32))
    return pl.pallas_call(
        functools.partial(_flash_kernel, tile=tile, n_pairs=n_pairs, heads=heads, per_pair_kv=per_pair_kv,
                          n_back=n_back, n_bias=bias.shape[1], with_sel=with_sel, with_lse=with_lse),
        grid=(batch * dil, nq),
        in_specs=in_specs,
        out_specs=out_specs,
        out_shape=out_shape,
        scratch_shapes=[pltpu.VMEM((2 * n_pairs, tile, LANES), BF16)]
        + ([pltpu.VMEM((2 * n_pairs, LANES, tile), F32)] if n_back is None else []),
        compiler_params=_cparams(("arbitrary", "arbitrary")),
        name="flash_sel" if with_sel else ("flash_dil" if with_lse else "flash_win"),
    )(*args)


def _mix_kernel(x_ref, g1_ref, oc_ref, os_ref, ow_ref, gl_ref, ge_ref,
                od0_ref, od1_ref, od2_ref, l0_ref, l1_ref, l2_ref, ml_ref,
                wn_ref, wd_ref, wo_ref, o_ref):
    d = x_ref.shape[1]
    sig = 1.0 / (1.0 + jnp.exp(-gl_ref[...]))
    hi = sig.astype(BF16)
    lo = (sig - hi.astype(F32)).astype(BF16)
    ge = ge_ref[...]
    gates = jnp.dot(hi, ge, preferred_element_type=F32) + jnp.dot(lo, ge, preferred_element_type=F32)
    o_nsa = (gates[:, 0:512] * oc_ref[...].astype(F32)
             + gates[:, 512:1024] * os_ref[...].astype(F32)
             + gates[:, 1024:1536] * ow_ref[...].astype(F32))
    u_nsa = jnp.dot(o_nsa.astype(BF16), wn_ref[...], preferred_element_type=F32)

    lses = (l0_ref[...], l1_ref[...], l2_ref[...])
    mx = jnp.maximum(jnp.maximum(lses[0], lses[1]), lses[2])
    es = [jnp.exp2(l - mx) for l in lses]
    inv = 1.0 / (es[0] + es[1] + es[2])
    o_dil = (es[0] * od0_ref[...].astype(F32) + es[1] * od1_ref[...].astype(F32)
             + es[2] * od2_ref[...].astype(F32)) * inv
    u_dil = jnp.dot(o_dil.astype(BF16), wd_ref[...], preferred_element_type=F32)

    gm = 1.0 / (1.0 + jnp.exp(-ml_ref[...].astype(F32)))
    merged = gm[:, :d] * u_nsa + gm[:, d:] * u_dil
    y = jnp.dot(merged.astype(BF16), wo_ref[...], preferred_element_type=F32)
    o_ref[...] = x_ref[...] + g1_ref[...] * y


def _mix(x2, g1, oc, osel, ow, gl, ge, od, lse, ml, wn, wd, wo, seq):
    t, d = x2.shape
    tm = TM_PROJ
    per_b = seq // tm
    row = lambda i: (i, 0)
    const = lambda i: (0, 0)
    rows = lambda a: pl.BlockSpec((tm, a.shape[1]), row)
    full = lambda a: pl.BlockSpec(a.shape, const)
    return pl.pallas_call(
        _mix_kernel,
        grid=(t // tm,),
        in_specs=[rows(x2), pl.BlockSpec((None, 1, d), lambda i: (i // per_b, 0, 0)),
                  rows(oc), rows(osel), rows(ow), rows(gl), full(ge),
                  rows(od[0]), rows(od[1]), rows(od[2]), rows(lse[0]), rows(lse[1]), rows(lse[2]), rows(ml),
                  full(wn), full(wd), full(wo)],
        out_specs=pl.BlockSpec((tm, d), row),
        out_shape=jax.ShapeDtypeStruct((t, d), F32),
        compiler_params=_cparams(("arbitrary",)),
        name="mix_outproj",
    )(x2, g1, oc, osel, ow, gl, ge, *od, *lse, ml, wn, wd, wo)


def _router_kernel(x_ref, sc_ref, sh_ref, g_ref, wr_ref, br_ref, h_ref, eid_ref, wts_ref):
    h = _norm_mod(x_ref[...], g_ref[...], sc_ref[...], sh_ref[...])
    n_sub = h.shape[1] // LANES
    for j in range(n_sub):
        h_ref[pl.ds(j, h.shape[0], stride=n_sub), :] = h[:, j * LANES:(j + 1) * LANES]
    logit = lax.dot_general(wr_ref[...], h, (((1,), (1,)), ((), ())), preferred_element_type=F32,
                            precision=lax.Precision.HIGHEST) + br_ref[...]
    grp = jnp.zeros((1, h.shape[0]), jnp.int32)
    best = logit[0:1]
    for k in range(1, N_EXPERT_GROUPS):
        better = logit[k:k + 1] > best
        grp = jnp.where(better, k, grp)
        best = jnp.where(better, logit[k:k + 1], best)
    den = jnp.zeros_like(best)
    for k in range(N_EXPERT_GROUPS):
        den = den + jnp.exp(logit[k:k + 1] - best)
    p_grp = 1.0 / den
    le = logit[SUBLANES:SUBLANES + EXPERTS_PER_GROUP]
    for k in range(1, N_EXPERT_GROUPS):
        lo = SUBLANES + k * EXPERTS_PER_GROUP
        le = jnp.where(grp == k, logit[lo:lo + EXPERTS_PER_GROUP], le)
    rowi = lax.broadcasted_iota(jnp.int32, le.shape, 0)
    v1 = jnp.max(le, axis=0, keepdims=True)
    i1 = jnp.min(jnp.where(le == v1, rowi, EXPERTS_PER_GROUP), axis=0, keepdims=True)
    rest = jnp.where(rowi == i1, -3e38, le)
    v2 = jnp.max(rest, axis=0, keepdims=True)
    i2 = jnp.min(jnp.where(rest == v2, rowi, EXPERTS_PER_GROUP), axis=0, keepdims=True)
    e2 = jnp.exp(v2 - v1)
    inv = p_grp / (1.0 + e2)
    eid_ref[...] = jnp.concatenate([grp * EXPERTS_PER_GROUP + i1, grp * EXPERTS_PER_GROUP + i2], axis=0)
    wts_ref[...] = jnp.concatenate([inv, e2 * inv], axis=0)


def _router(x2, sc, sh, g, wr_t, br, seq):
    t, d = x2.shape
    tm = TM_PROJ
    per_b = seq // tm
    row = lambda i: (i, 0)
    const = lambda i: (0, 0)
    return pl.pallas_call(
        _router_kernel,
        grid=(t // tm,),
        in_specs=[pl.BlockSpec((tm, d), row),
                  pl.BlockSpec((None, 1, d), lambda i: (i // per_b, 0, 0)),
                  pl.BlockSpec((None, 1, d), lambda i: (i // per_b, 0, 0)),
                  pl.BlockSpec((1, d), const),
                  pl.BlockSpec(wr_t.shape, const),
                  pl.BlockSpec(br.shape, const)],
        out_specs=[pl.BlockSpec((tm * (d // LANES), LANES), row),
                   pl.BlockSpec((2, tm), lambda i: (0, i)),
                   pl.BlockSpec((2, tm), lambda i: (0, i))],
        out_shape=[jax.ShapeDtypeStruct((t * (d // LANES), LANES), F32),
                   jax.ShapeDtypeStruct((2, t), jnp.int32),
                   jax.ShapeDtypeStruct((2, t), F32)],
        compiler_params=_cparams(("arbitrary",)),
        name="norm_router",
    )(x2, sc, sh, g, wr_t, br)


def _moe_kernel(nblk_ref, off_ref, tok_ref, wt_ref, h_ref, w1_ref, w3_ref, w2_ref, o_ref,
                xs_ref, os_ref, xb_ref, y3_ref):
    c = pl.program_id(0)
    e = pl.program_id(1)
    n_sub = xb_ref.shape[1] // LANES
    rb = xb_ref.shape[0]

    @pl.when(e == 0)
    def _():
        o_ref[...] = jnp.zeros_like(o_ref)

    off = off_ref[c, e]

    def every_row(base, fn):
        def grp(g, _):
            slot0 = base + g * MOE_UNROLL
            row0 = pl.multiple_of(g * (MOE_UNROLL * n_sub), MOE_UNROLL * n_sub)
            for u in range(MOE_UNROLL):
                fn(slot0 + u, pl.ds(row0 + u * n_sub, n_sub))
            return 0

        lax.fori_loop(0, rb // MOE_UNROLL, grp, 0)

    def token_rows(slot):
        return pl.ds(pl.multiple_of(tok_ref[0, slot], n_sub), n_sub)

    def block(bi, _):
        base = off + bi * rb

        def gather(slot, rows):
            src = token_rows(slot)
            xs_ref[rows, :] = h_ref[src, :]
            os_ref[rows, :] = o_ref[src, :]

        every_row(base, gather)
        for j in range(n_sub):
            xb_ref[:, j * LANES:(j + 1) * LANES] = xs_ref[pl.ds(j, rb, stride=n_sub), :].astype(BF16)
        xb = xb_ref[...]
        a = jnp.dot(xb, w1_ref[...], preferred_element_type=F32)
        b = jnp.dot(xb, w3_ref[...], preferred_element_type=F32)
        mid = (a * (1.0 / (1.0 + jnp.exp(-a))) * b).astype(BF16)
        y = jnp.dot(mid, w2_ref[...], preferred_element_type=F32)
        for j in range(n_sub):
            y3_ref[pl.ds(j, rb, stride=n_sub), :] = y[:, j * LANES:(j + 1) * LANES]

        def scatter(slot, rows):
            o_ref[token_rows(slot), :] = os_ref[rows, :] + wt_ref[0, slot] * y3_ref[rows, :]

        every_row(base, scatter)
        return 0

    lax.fori_loop(0, nblk_ref[c, e], block, 0)


def _moe(h2, eid, wts, w1, w3, w2):
    n_sub = w1.shape[1] // LANES
    t, d = h2.shape[0] // n_sub, w1.shape[1]
    tc = min(MOE_CHUNK, t)
    n_chunks = t // tc
    slots = 2 * tc
    tok = jnp.arange(t, dtype=jnp.int32)
    key = ((tok // tc)[None, :] * N_EXPERTS + eid).reshape(-1)
    order = jnp.argsort(key)
    tok_sorted = (jnp.tile(tok % tc * n_sub, 2)[order]).reshape(n_chunks, slots)
    w_sorted = wts.reshape(-1)[order].reshape(n_chunks, slots)
    counts = jnp.zeros((n_chunks * N_EXPERTS,), jnp.int32).at[key].add(1).reshape(n_chunks, N_EXPERTS)
    starts = jnp.cumsum(counts, axis=1) - counts
    nblk = (counts + MOE_ROWS - 1) // MOE_ROWS
    pends = jnp.cumsum(nblk * MOE_ROWS, axis=1)
    offs = pends - nblk * MOE_ROWS
    slots_p = slots + N_EXPERTS * MOE_ROWS
    q = jnp.arange(slots_p, dtype=jnp.int32)[None, :]
    e_of = jnp.minimum(jnp.sum(q[:, :, None] >= pends[:, None, :], axis=-1), N_EXPERTS - 1)
    take = lambda a: jnp.take_along_axis(a, e_of, axis=1)
    src = take(starts) + jnp.clip(q - take(offs), 0, jnp.maximum(take(counts) - 1, 0))
    src = jnp.minimum(src, slots - 1)
    tok_sorted = jnp.take_along_axis(tok_sorted, src, axis=1).reshape(n_chunks, 1, slots_p)
    w_sorted = jnp.take_along_axis(w_sorted, src, axis=1).reshape(n_chunks, 1, slots_p)

    grid_spec = pltpu.PrefetchScalarGridSpec(
        num_scalar_prefetch=2,
        grid=(n_chunks, N_EXPERTS),
        in_specs=[pl.BlockSpec((None, 1, slots_p), lambda c, e, *_: (c, 0, 0), memory_space=pltpu.SMEM),
                  pl.BlockSpec((None, 1, slots_p), lambda c, e, *_: (c, 0, 0), memory_space=pltpu.SMEM),
                  pl.BlockSpec((tc * n_sub, LANES), lambda c, e, *_: (c, 0)),
                  pl.BlockSpec((None, d, D_EXPERT), lambda c, e, *_: (e, 0, 0)),
                  pl.BlockSpec((None, d, D_EXPERT), lambda c, e, *_: (e, 0, 0)),
                  pl.BlockSpec((None, D_EXPERT, d), lambda c, e, *_: (e, 0, 0))],
        out_specs=pl.BlockSpec((tc * n_sub, LANES), lambda c, e, *_: (c, 0)),
        scratch_shapes=[pltpu.VMEM((MOE_ROWS * n_sub, LANES), F32),
                        pltpu.VMEM((MOE_ROWS * n_sub, LANES), F32),
                        pltpu.VMEM((MOE_ROWS, d), BF16),
                        pltpu.VMEM((MOE_ROWS * n_sub, LANES), F32)],
    )
    out = pl.pallas_call(
        _moe_kernel,
        grid_spec=grid_spec,
        out_shape=jax.ShapeDtypeStruct((t * n_sub, LANES), F32),
        compiler_params=_cparams(("arbitrary", "arbitrary")),
        name="moe_experts",
    )(nblk, offs, tok_sorted, w_sorted, h2, w1, w3, w2)
    return out


def _resid_kernel(x_ref, y_ref, g_ref, nf_ref, o_ref, *, final):
    tm, d = x_ref.shape
    n_sub = d // LANES
    y = jnp.concatenate([y_ref[pl.ds(j, tm, stride=n_sub), :] for j in range(n_sub)], axis=1)
    x = x_ref[...] + g_ref[...] * y
    if final:
        ms = jnp.mean(x * x, axis=-1, keepdims=True)
        x = x * lax.rsqrt(ms + RMS_EPS) * nf_ref[...]
    o_ref[...] = x


def _residual(x2, y2, g2, norm_f, seq, final):
    t, d = x2.shape
    tm = TM_PROJ
    per_b = seq // tm
    row = lambda i: (i, 0)
    return pl.pallas_call(
        functools.partial(_resid_kernel, final=final),
        grid=(t // tm,),
        in_specs=[pl.BlockSpec((tm, d), row), pl.BlockSpec((tm * (d // LANES), LANES), row),
                  pl.BlockSpec((None, 1, d), lambda i: (i // per_b, 0, 0)),
                  pl.BlockSpec((1, d), lambda i: (0, 0))],
        out_specs=pl.BlockSpec((tm, d), row),
        out_shape=jax.ShapeDtypeStruct((t, d), F32),
        compiler_params=_cparams(("arbitrary",)),
        name="residual_final" if final else "residual",
    )(x2, y2, g2, norm_f)


def _split_w_in(w_in, d):
    scale = HEAD_DIM ** -0.5 * LOG2E
    nq = NSA_HEADS * HEAD_DIM
    nkv = 3 * 2 * NSA_KV_HEADS * HEAD_DIM
    ngate = 3 * NSA_HEADS
    ndil = 3 * N_DIL_GROUPS * DIL_HEADS_PER_GROUP * HEAD_DIM
    o1, o2, o3 = nq, nq + nkv, nq + nkv + ngate
    o4 = o3 + ndil
    wq = (w_in[:, :o1] * scale).reshape(d, NSA_KV_HEADS, NSA_GROUP, HEAD_DIM)
    wq = wq.transpose(0, 2, 1, 3).reshape(d, nq)
    wkv = w_in[:, o1:o2]
    wg = jnp.pad(w_in[:, o2:o3], ((0, 0), (0, LANES - ngate)))
    gw = DIL_HEADS_PER_GROUP * HEAD_DIM
    per_which = N_DIL_GROUPS * gw
    wds = []
    for grp in range(N_DIL_GROUPS):
        parts = [w_in[:, o3 + which * per_which + grp * gw: o3 + which * per_which + (grp + 1) * gw]
                 for which in range(3)]
        parts[0] = parts[0] * scale
        wds.append(jnp.concatenate(parts, axis=1))
    wm = w_in[:, o4:]
    cast = lambda w: w.astype(BF16)
    return [cast(wq), cast(wkv), cast(wds[0]), cast(wds[1]), cast(wds[2]), cast(wm), cast(wg)]


def kernel(x, c, rel_bias, ada_w, ada_b, norm1, norm2, w_in, cmp_pos, cmp_w1, cmp_w2, w_up_nsa, w_up_dil, w_o,
           router_wg, router_bg, router_we, router_be, exp_w1, exp_w3, exp_w2, norm_f):
    batch, seq, d = x.shape
    depth = ada_w.shape[0]
    t = batch * seq
    n_cmp = seq // CMP_STRIDE
    n_sel = seq // SEL_BLOCK
    assert seq % TQ_NSA == 0 and n_sel <= HEAD_DIM and n_sel >= SEL_TOPN
    assert all(seq % (dil * TQ_DIL) == 0 for _, dil in DIL_PAIRS)

    mod = _modulation(c, ada_w, ada_b)

    nq_nsa = seq // TQ_NSA
    bias_cmp = _expand_bias(rel_bias, _cmp_buckets(seq, n_cmp), 0, NSA_HEADS, 8 * SUBLANES)
    sel_b = _toeplitz_buckets(nq_nsa, TQ_NSA, seq, 1)
    n_sel_bias = nq_nsa
    while n_sel_bias > 1 and (sel_b[n_sel_bias - 2:] == sel_b[n_sel_bias - 1, 0, 0]).all():
        n_sel_bias -= 1
    sel_b = sel_b[:n_sel_bias].reshape(n_sel_bias * TQ_NSA, TQ_NSA)
    bias_sel = _expand_bias(rel_bias, sel_b, 0, NSA_HEADS, TQ_NSA).reshape(NSA_HEADS, n_sel_bias, TQ_NSA, TQ_NSA)
    nb_win = min(-(-NSA_WINDOW // TQ_NSA), nq_nsa - 1)
    win_b = _toeplitz_buckets(nb_win + 1, TQ_NSA, NSA_WINDOW, 1, masked_tail=True).reshape(-1, TQ_NSA)
    bias_win = _expand_bias(rel_bias, win_b, 0, NSA_HEADS, TQ_NSA).reshape(NSA_HEADS, nb_win + 2, TQ_NSA, TQ_NSA)
    bias_dil, nb_dil = [], []
    for grp, (window, dil) in enumerate(DIL_PAIRS):
        nb = min(-(-(window // dil) // TQ_DIL), seq // dil // TQ_DIL - 1)
        bk = _toeplitz_buckets(nb + 1, TQ_DIL, window // dil, dil, masked_tail=True).reshape(-1, TQ_DIL)
        hb = NSA_HEADS + grp * DIL_HEADS_PER_GROUP
        bias_dil.append(_expand_bias(rel_bias, bk, hb, DIL_HEADS_PER_GROUP, TQ_DIL)
                        .reshape(DIL_HEADS_PER_GROUP, nb + 2, TQ_DIL, TQ_DIL))
        nb_dil.append(nb)
    ovl_t = jnp.asarray(_overlap_t(n_cmp, n_sel), BF16)
    onehot = jnp.asarray(_block_onehot(seq), BF16)
    gate_e = jnp.asarray(_gate_expand(), BF16)

    x2 = x.reshape(t, d)
    for l in range(depth):
        sh1, sc1, g1, sh2, sc2, g2 = [m.reshape(batch, 1, d) for m in jnp.split(mod[l], 6, axis=-1)]
        weights = _split_w_in(w_in[l], d)
        (q_n, kc_in, vc_in, k_sel, v_sel, k_win, v_win, qkv_d0, qkv_d1, qkv_d2, merge_l, gate_l) = _inproj(
            x2, sc1, sh1, norm1[l].reshape(1, d), onehot, weights, seq)

        kc = _compress(kc_in, cmp_pos[l, 0], cmp_w1[l, 0], cmp_w2[l, 0], batch, seq, transpose_out=False)
        vct = _compress(vc_in, cmp_pos[l, 1], cmp_w1[l, 1], cmp_w2[l, 1], batch, seq, transpose_out=True)
        o_c, selb = _cmp_select(q_n, kc, vct, bias_cmp, ovl_t, batch, seq)
        nsa_common = dict(batch=batch, length=seq, dil=1, tile=TQ_NSA, n_pairs=NSA_GROUP, q_blk=lambda r: 0,
                          k_blk=lambda r: 0, v_blk=lambda r: 0, v_width=LANES, per_pair_kv=False,
                          heads=tuple((g, NSA_GROUP + g) for g in range(NSA_GROUP)))
        q3 = q_n.reshape(batch, seq, 512)
        (o_s,) = _flash(q3, k_sel.reshape(batch, seq, 2 * LANES), v_sel.reshape(batch, seq, LANES), bias_sel,
                        n_back=None, k_width=2 * LANES, selb=selb.reshape(batch, seq, 2 * LANES), **nsa_common)
        (o_w,) = _flash(q3, k_win.reshape(batch, seq, LANES), v_win.reshape(batch, seq, LANES), bias_win,
                        n_back=nb_win, k_width=LANES, **nsa_common)

        o_d, lse_d = [], []
        for grp, ((window, dil), qkv) in enumerate(zip(DIL_PAIRS, (qkv_d0, qkv_d1, qkv_d2))):
            length = seq // dil
            view = qkv.reshape(batch, length, dil * 768)
            o_g, lse_g = _flash(view, view, view, bias_dil[grp], batch=batch, length=length, dil=dil, tile=TQ_DIL,
                                n_back=nb_dil[grp], n_pairs=2, q_blk=lambda r: 3 * r, k_blk=lambda r: 3 * r + 1,
                                k_width=2 * LANES, v_blk=lambda r: 3 * r + 2, v_width=2 * LANES,
                                per_pair_kv=True, heads=((0, 1), (2, 3)), with_lse=True)
            o_d.append(o_g.reshape(t, 256))
            lse_d.append(lse_g.reshape(t, 256))

        wn = w_up_nsa[l].reshape(NSA_KV_HEADS, NSA_GROUP, HEAD_DIM, d).transpose(1, 0, 2, 3).reshape(512, d)
        x2 = _mix(x2, g1, o_c, o_s.reshape(t, 512), o_w.reshape(t, 512), gate_l, gate_e, o_d, lse_d, merge_l,
                  wn.astype(BF16), w_up_dil[l].astype(BF16), w_o[l].astype(BF16), seq)

        wr_t = jnp.concatenate([jnp.pad(router_wg[l], ((0, 0), (0, SUBLANES - N_EXPERT_GROUPS))),
                                router_we[l]], axis=1).T
        br = jnp.concatenate([jnp.pad(router_bg[l], (0, SUBLANES - N_EXPERT_GROUPS)),
                              router_be[l]]).reshape(-1, 1)
        h2, eid, wts = _router(x2, sc2, sh2, norm2[l].reshape(1, d), wr_t, br, seq)
        y = _moe(h2, eid, wts, exp_w1[l].astype(BF16), exp_w3[l].astype(BF16), exp_w2[l].astype(BF16))
        x2 = _residual(x2, y, g2, norm_f.reshape(1, d), seq, final=(l == depth - 1))
    return x2.reshape(batch, seq, d)
```

```python
import functools
import math

import numpy as np
import jax
import jax.numpy as jnp
from jax import lax
from jax.experimental import pallas as pl
from jax.experimental.pallas import tpu as pltpu

F32 = jnp.float32
BF16 = jnp.bfloat16

HEAD_DIM = 64
NSA_HEADS = 8
NSA_KV_HEADS = 2
NSA_GROUP = NSA_HEADS // NSA_KV_HEADS
CMP_LEN = 32
CMP_STRIDE = 16
CMP_HIDDEN = 256
SEL_BLOCK = 64
SEL_TOPN = 16
NSA_WINDOW = 512
FORCE_SCORE = 1e4
DIL_PAIRS = ((128, 1), (512, 4), (2048, 16))
N_DIL_GROUPS = 3
DIL_HEADS_PER_GROUP = 4
NUM_BUCKETS = 32
REL_MAX_DIST = 2048
N_EXPERT_GROUPS = 4
EXPERTS_PER_GROUP = 8
N_EXPERTS = N_EXPERT_GROUPS * EXPERTS_PER_GROUP
D_EXPERT = 512
RMS_EPS = 1e-6

LOG2E = math.log2(math.e)
LANES = 128
SUBLANES = 8
MASK_NEG = -1e30
SEL_NEG = -1e9
TQ_NSA = 256
TQ_DIL = 128
TM_PROJ = 512
DIL_Q_TILES = 4
FLASH_LOOKAHEAD = 4
MOE_CHUNK = 2048
MOE_ROWS = 128
MOE_UNROLL = 16
VMEM_LIMIT = 56 * 1024 * 1024


def _cparams(sem):
    return pltpu.CompilerParams(dimension_semantics=sem, vmem_limit_bytes=VMEM_LIMIT)


def _np_bucket(dist):
    dist = np.maximum(dist, 0)
    max_exact = NUM_BUCKETS // 2
    df = np.maximum(dist, 1).astype(np.float32)
    val = np.log(df / np.float32(max_exact)) / np.float32(math.log(REL_MAX_DIST / max_exact))
    large = max_exact + (val * np.float32(NUM_BUCKETS - max_exact)).astype(np.int32)
    large = np.minimum(large, NUM_BUCKETS - 1)
    return np.where(dist < max_exact, dist, large).astype(np.int32)


def _toeplitz_buckets(n_delta, tile, window, dist_scale, masked_tail=False):
    dd = np.arange(n_delta)[:, None, None]
    r = np.arange(tile)[None, :, None]
    c = np.arange(tile)[None, None, :]
    dist = dd * tile + c - r
    valid = (dist >= 0) & (dist <= window)
    out = np.where(valid, _np_bucket(dist * dist_scale), -1).astype(np.int32)
    if masked_tail:
        out = np.concatenate([out, np.full((1, tile, tile), -1, np.int32)], axis=0)
    return out


def _cmp_buckets(seq, n_cmp):
    c_end = np.arange(n_cmp)[:, None] * CMP_STRIDE + CMP_LEN - 1
    t = np.arange(seq)[None, :]
    dist = t - c_end
    return np.where(dist >= 0, _np_bucket(dist), -1).astype(np.int32)


def _overlap_t(n_cmp, n_sel):
    c_start = np.arange(n_cmp)[None, :] * CMP_STRIDE
    s_start = np.arange(n_sel)[:, None] * SEL_BLOCK
    ov = np.clip(np.minimum(c_start + CMP_LEN, s_start + SEL_BLOCK) - np.maximum(c_start, s_start), 0, None)
    out = np.zeros((LANES, n_cmp), np.float32)
    out[64:64 + n_sel] = ov.astype(np.float32) / CMP_STRIDE
    return out


def _block_onehot(seq):
    oh = np.zeros((seq, LANES), np.float32)
    blk = np.arange(seq) // SEL_BLOCK
    oh[np.arange(seq), blk] = 1.0
    oh[np.arange(seq), 64 + blk] = 1.0
    return oh


def _gate_expand():
    e = np.zeros((LANES, 3 * NSA_HEADS * HEAD_DIM), np.float32)
    for br in range(3):
        for g in range(NSA_GROUP):
            for ln in range(LANES):
                kv = ln // HEAD_DIM
                e[br * NSA_HEADS + kv * NSA_GROUP + g, br * 512 + g * LANES + ln] = 1.0
    return e


def _mod_kernel(c_ref, w_ref, b_ref, o_ref):
    c = c_ref[...]
    cond = c * (1.0 / (1.0 + jnp.exp(-c)))
    o_ref[...] = jnp.dot(cond, w_ref[...], preferred_element_type=F32,
                         precision=lax.Precision.HIGHEST) + b_ref[...]


def _modulation(c, ada_w, ada_b):
    depth, d, n = ada_w.shape
    b = c.shape[0]
    tn = 1536
    return pl.pallas_call(
        _mod_kernel,
        grid=(depth, n // tn),
        in_specs=[pl.BlockSpec((b, d), lambda l, j: (0, 0)),
                  pl.BlockSpec((None, d, tn), lambda l, j: (l, 0, j)),
                  pl.BlockSpec((None, 1, tn), lambda l, j: (l, 0, j))],
        out_specs=pl.BlockSpec((None, b, tn), lambda l, j: (l, 0, j)),
        out_shape=jax.ShapeDtypeStruct((depth, b, n), F32),
        compiler_params=_cparams(("arbitrary", "arbitrary")),
        name="adaln_modulation",
    )(c, ada_w, ada_b.reshape(depth, 1, n))


def _bias_kernel(tbl_ref, bkt_ref, o_ref, *, head_base):
    h = pl.program_id(0) + head_base
    bkt = bkt_ref[...]
    acc = jnp.full(bkt.shape, MASK_NEG, F32)
    for b in range(NUM_BUCKETS):
        acc = jnp.where(bkt == b, tbl_ref[b, h] * LOG2E, acc)
    o_ref[...] = acc


def _expand_bias(rel_bias, buckets, head_base, n_heads, row_tile):
    rows, cols = buckets.shape
    return pl.pallas_call(
        functools.partial(_bias_kernel, head_base=head_base),
        grid=(n_heads, rows // row_tile),
        in_specs=[pl.BlockSpec(memory_space=pltpu.SMEM),
                  pl.BlockSpec((row_tile, cols), lambda h, i: (i, 0))],
        out_specs=pl.BlockSpec((None, row_tile, cols), lambda h, i: (h, i, 0)),
        out_shape=jax.ShapeDtypeStruct((n_heads, rows, cols), F32),
        compiler_params=_cparams(("arbitrary", "arbitrary")),
        name="bias_expand",
    )(rel_bias, jnp.asarray(buckets))


def _norm_mod(x, g, sc, sh):
    ms = jnp.mean(x * x, axis=-1, keepdims=True)
    y = x * lax.rsqrt(ms + RMS_EPS) * g
    return y * (1.0 + sc) + sh


def _inproj_kernel(x_ref, sc_ref, sh_ref, g_ref, oh_ref, wq_ref, wkv_ref, wd0_ref, wd1_ref, wd2_ref, wm_ref, wg_ref,
                   q_ref, kc_ref, vc_ref, ks_ref, vs_ref, kw_ref, vw_ref, d0_ref, d1_ref, d2_ref, m_ref, gl_ref):
    h = _norm_mod(x_ref[...], g_ref[...], sc_ref[...], sh_ref[...]).astype(BF16)

    def proj(w_ref):
        return jnp.dot(h, w_ref[...], preferred_element_type=F32)

    q_ref[...] = proj(wq_ref).astype(BF16)
    kv = proj(wkv_ref).astype(BF16)
    for k, ref in ((0, kc_ref), (1, vc_ref), (3, vs_ref), (4, kw_ref), (5, vw_ref)):
        ref[...] = kv[:, k * LANES:(k + 1) * LANES]
    k_sel = kv[:, 2 * LANES:3 * LANES]
    oh = oh_ref[...]
    lo = lax.broadcasted_iota(jnp.int32, k_sel.shape, 1) < HEAD_DIM
    ks_ref[:, :LANES] = jnp.where(lo, k_sel, oh)
    ks_ref[:, LANES:] = jnp.where(lo, oh, k_sel)
    d0_ref[...] = proj(wd0_ref).astype(BF16)
    d1_ref[...] = proj(wd1_ref).astype(BF16)
    d2_ref[...] = proj(wd2_ref).astype(BF16)
    m_ref[...] = proj(wm_ref).astype(BF16)
    gl_ref[...] = proj(wg_ref)


def _inproj(x2, sc, sh, g, onehot, weights, seq):
    t, d = x2.shape
    tm = TM_PROJ
    per_b = seq // tm
    widths = (512, 128, 128, 256, 128, 128, 128, 768, 768, 768, 2 * d, LANES)
    dtypes = (BF16,) * 11 + (F32,)
    row = lambda i: (i, 0)
    const = lambda i: (0, 0)
    in_specs = [pl.BlockSpec((tm, d), row),
                pl.BlockSpec((None, 1, d), lambda i: (i // per_b, 0, 0)),
                pl.BlockSpec((None, 1, d), lambda i: (i // per_b, 0, 0)),
                pl.BlockSpec((1, d), const),
                pl.BlockSpec((tm, LANES), lambda i: (i % per_b, 0))]
    in_specs += [pl.BlockSpec(w.shape, const) for w in weights]
    return pl.pallas_call(
        _inproj_kernel,
        grid=(t // tm,),
        in_specs=in_specs,
        out_specs=[pl.BlockSpec((tm, w), row) for w in widths],
        out_shape=[jax.ShapeDtypeStruct((t, w), dt) for w, dt in zip(widths, dtypes)],
        compiler_params=_cparams(("arbitrary",)),
        name="norm_inproj",
    )(x2, sc, sh, g, onehot, *weights)


def _gelu_tanh(x):
    return 0.5 * x * (1.0 + jnp.tanh(math.sqrt(2.0 / math.pi) * (x + 0.044715 * (x * x * x))))


def _compress_kernel(r_ref, pt_ref, pb_ref, wt_ref, wb_ref, w2_ref, o_ref, *, transpose_out):
    r = r_ref[...].astype(F32)
    top = jnp.dot((r + pt_ref[...]).astype(BF16), wt_ref[...], preferred_element_type=F32)
    bot = jnp.dot((r + pb_ref[...]).astype(BF16), wb_ref[...], preferred_element_type=F32)
    n = bot.shape[0]
    hid = top + pltpu.roll(bot, n - 1, 0)
    act = _gelu_tanh(hid).astype(BF16)
    if transpose_out:
        o_ref[...] = lax.dot_general(w2_ref[...], act, (((1,), (1,)), ((), ())),
                                     preferred_element_type=F32).astype(BF16)
    else:
        o_ref[...] = jnp.dot(act, w2_ref[...], preferred_element_type=F32).astype(BF16)


def _compress(tok, pos, w1, w2, batch, seq, transpose_out):
    nc = seq // CMP_STRIDE
    half = CMP_LEN // 2
    eye = jnp.eye(NSA_KV_HEADS, dtype=F32)
    w1r = w1.reshape(CMP_LEN, HEAD_DIM, CMP_HIDDEN)
    blk = lambda w: jnp.einsum('ldn,hg->lhdgn', w, eye).reshape(half * LANES, 2 * CMP_HIDDEN).astype(BF16)
    wt, wb = blk(w1r[:half]), blk(w1r[half:])
    posr = lambda p: jnp.broadcast_to(p[:, None, :], (half, NSA_KV_HEADS, HEAD_DIM)).reshape(1, half * LANES)
    pt, pb = posr(pos[:half]), posr(pos[half:])
    w2b = jnp.einsum('nd,hg->hngd', w2, eye).reshape(2 * CMP_HIDDEN, LANES)
    if transpose_out:
        w2b = w2b.T
        out_block, out_shape = (None, LANES, nc), (batch, LANES, nc)
    else:
        out_block, out_shape = (None, nc, LANES), (batch, nc, LANES)
    w2b = w2b.astype(BF16)
    const = lambda b: (0, 0)
    return pl.pallas_call(
        functools.partial(_compress_kernel, transpose_out=transpose_out),
        grid=(batch,),
        in_specs=[pl.BlockSpec((None, nc, half * LANES), lambda b: (b, 0, 0)),
                  pl.BlockSpec(pt.shape, const), pl.BlockSpec(pb.shape, const),
                  pl.BlockSpec(wt.shape, const), pl.BlockSpec(wb.shape, const),
                  pl.BlockSpec(w2b.shape, const)],
        out_specs=pl.BlockSpec(out_block, lambda b: (b, 0, 0)),
        out_shape=jax.ShapeDtypeStruct(out_shape, BF16),
        compiler_params=_cparams(("arbitrary",)),
        name="nsa_compress",
    )(tok.reshape(batch, nc, half * LANES), pt, pb, wt, wb, w2b)


def _cmp_select_kernel(q_ref, kc_ref, vct_ref, bias_ref, ovl_ref, o_ref, selb_ref, *, n_sel, n_top):
    tq = q_ref.shape[0]
    nc = kc_ref.shape[0]
    qs = pl.program_id(1) * tq
    kc = kc_ref[...]
    vct = vct_ref[...]
    lane_q = lax.broadcasted_iota(jnp.int32, (tq, LANES), 1)
    row_o = lax.broadcasted_iota(jnp.int32, (LANES, tq), 0)
    heads = [(g, kv) for g in range(NSA_GROUP) for kv in range(NSA_KV_HEADS)]

    def scores(g, kv):
        qt = q_ref[:, g * LANES:(g + 1) * LANES]
        mine = (lane_q < HEAD_DIM) if kv == 0 else (lane_q >= HEAD_DIM)
        qm = jnp.where(mine, qt, jnp.zeros_like(qt))
        s = lax.dot_general(kc, qm, (((1,), (1,)), ((), ())), preferred_element_type=F32)
        return s + bias_ref[kv * NSA_GROUP + g]

    pc_sum = [jnp.zeros((nc, tq), F32) for _ in range(NSA_KV_HEADS)]
    outs = {}
    ahead = [scores(*hd) for hd in heads[:FLASH_LOOKAHEAD]]
    for n, (g, kv) in enumerate(heads):
        s = ahead.pop(0)
        if n + FLASH_LOOKAHEAD < len(heads):
            ahead.append(scores(*heads[n + FLASH_LOOKAHEAD]))
        m = jnp.max(s, axis=0, keepdims=True)
        m = jnp.where(m < 0.5 * MASK_NEG, 0.0, m)
        p = jnp.exp2(s - m)
        den = jnp.sum(p, axis=0, keepdims=True)
        pc = p * (1.0 / jnp.where(den > 0.0, den, 1.0))
        pc_sum[kv] = pc_sum[kv] + pc
        outs[kv] = jnp.dot(vct, pc.astype(BF16), preferred_element_type=F32)
        if kv == NSA_KV_HEADS - 1:
            o_t = jnp.where(row_o < HEAD_DIM, outs[0], outs[1])
            o_ref[:, g * LANES:(g + 1) * LANES] = o_t.T.astype(BF16)

    ovl = ovl_ref[...]
    imps = []
    for kv in range(NSA_KV_HEADS):
        hi = pc_sum[kv].astype(BF16)
        lo = (pc_sum[kv] - hi.astype(F32)).astype(BF16)
        imps.append((jnp.dot(ovl, hi, preferred_element_type=F32)
                     + jnp.dot(ovl, lo, preferred_element_type=F32))[HEAD_DIM:])
    imp = jnp.concatenate(imps, axis=1)
    wide = (HEAD_DIM, NSA_KV_HEADS * tq)
    rowj = lax.broadcasted_iota(jnp.int32, wide, 0)
    col = lax.broadcasted_iota(jnp.int32, wide, 1)
    t = qs + jnp.where(col >= tq, col - tq, col)
    jq = jnp.right_shift(t, SEL_BLOCK.bit_length() - 1)
    forced = (rowj == 0) | (rowj == jq) | (rowj == jq - 1)
    score = jnp.where(forced, FORCE_SCORE, jnp.where(rowj > jq, -1.0, imp))
    rem = jnp.where(rowj < n_sel, score, -3e38)
    sel = jnp.zeros(wide, F32)
    for _ in range(n_top):
        m = jnp.max(rem, axis=0, keepdims=True)
        idx = jnp.min(jnp.where(rem == m, rowj, HEAD_DIM), axis=0, keepdims=True)
        pick = rowj == idx
        sel = jnp.where(pick, 1.0, sel)
        rem = jnp.where(pick, -3e38, rem)
    sb = jnp.where(sel > 0.5, 0.0, SEL_NEG)
    zero = jnp.zeros((HEAD_DIM, tq), F32)
    selb_ref[:, :LANES] = jnp.concatenate([zero, sb[:, :tq]], axis=0).T.astype(BF16)
    selb_ref[:, LANES:] = jnp.concatenate([sb[:, tq:], zero], axis=0).T.astype(BF16)


def _cmp_select(q, kc, vct, bias_c, ovl_t, batch, seq):
    t = q.shape[0]
    tq = TQ_NSA
    nq = seq // tq
    nc = seq // CMP_STRIDE
    n_sel = seq // SEL_BLOCK
    n_top = min(SEL_TOPN, n_sel)
    return pl.pallas_call(
        functools.partial(_cmp_select_kernel, n_sel=n_sel, n_top=n_top),
        grid=(batch, nq),
        in_specs=[pl.BlockSpec((tq, 512), lambda b, i: (b * nq + i, 0)),
                  pl.BlockSpec((None, nc, LANES), lambda b, i: (b, 0, 0)),
                  pl.BlockSpec((None, LANES, nc), lambda b, i: (b, 0, 0)),
                  pl.BlockSpec((NSA_HEADS, nc, tq), lambda b, i: (0, 0, i)),
                  pl.BlockSpec((LANES, nc), lambda b, i: (0, 0))],
        out_specs=[pl.BlockSpec((tq, 512), lambda b, i: (b * nq + i, 0)),
                   pl.BlockSpec((tq, 2 * LANES), lambda b, i: (b * nq + i, 0))],
        out_shape=[jax.ShapeDtypeStruct((t, 512), BF16),
                   jax.ShapeDtypeStruct((t, 2 * LANES), BF16)],
        compiler_params=_cparams(("arbitrary", "arbitrary")),
        name="nsa_cmp_select",
    )(q, kc, vct, bias_c, ovl_t)


def _flash_kernel(*refs, tile, q_tiles, n_pairs, heads, per_pair_kv, n_back, n_bias, with_sel, with_lse):
    it = iter(refs)
    q_ref, k_ref, v_ref, b_ref = (next(it) for _ in range(4))
    selb_ref = next(it) if with_sel else None
    o_ref = next(it)
    lse_ref = next(it) if with_lse else None
    qa_ref = next(it)
    acc_ref = next(it) if n_back is None else None

    i = pl.program_id(1)
    n_heads = 2 * n_pairs
    lo = lax.broadcasted_iota(jnp.int32, (tile, LANES), 1) < HEAD_DIM
    for u in range(q_tiles):
        rows = slice(u * tile, (u + 1) * tile)
        for p in range(n_pairs):
            q = q_ref[rows, p * LANES:(p + 1) * LANES]
            if with_sel:
                qa_ref[u * n_heads + 2 * p] = jnp.where(lo, q, selb_ref[rows, :LANES])
                qa_ref[u * n_heads + 2 * p + 1] = jnp.where(lo, selb_ref[rows, LANES:], q)
            else:
                zero = jnp.zeros_like(q)
                qa_ref[u * n_heads + 2 * p] = jnp.where(lo, q, zero)
                qa_ref[u * n_heads + 2 * p + 1] = jnp.where(lo, zero, q)
    nt = (((1,), (1,)), ((), ()))
    tn = (((0,), (0,)), ((), ()))
    top = lax.broadcasted_iota(jnp.int32, (LANES, tile), 0) < HEAD_DIM

    def finish(u, p, acc0, l0, m0, acc1, l1, m1):
        rows = slice(u * tile, (u + 1) * tile)
        o_t = jnp.where(top, acc0 * (1.0 / l0), acc1 * (1.0 / l1))
        o_ref[rows, p * LANES:(p + 1) * LANES] = o_t.T.astype(o_ref.dtype)
        if with_lse:
            lse_t = jnp.where(top, m0 + jnp.log2(l0), m1 + jnp.log2(l1))
            lse_ref[rows, p * LANES:(p + 1) * LANES] = lse_t.T

    def scores(unit, tiles):
        hd = unit % n_heads
        p, half = divmod(hd, 2)
        kc = half if with_sel else (p if per_pair_kv else 0)
        parts = []
        for j, bidx in tiles:
            kt = k_ref[pl.ds(pl.multiple_of(j * tile, tile), tile), kc * LANES:(kc + 1) * LANES]
            s = lax.dot_general(kt, qa_ref[unit], nt, preferred_element_type=F32)
            parts.append(s + b_ref[heads[p][half], bidx])
        return parts

    def group(unit_tiles, state):
        res = []
        n_units = len(unit_tiles)
        ahead = [scores(n, unit_tiles[n]) for n in range(min(FLASH_LOOKAHEAD, n_units))]
        for hd in range(n_units):
            tiles = unit_tiles[hd]
            parts = ahead.pop(0)
            if hd + FLASH_LOOKAHEAD < n_units:
                ahead.append(scores(hd + FLASH_LOOKAHEAD, unit_tiles[hd + FLASH_LOOKAHEAD]))
            m_new = functools.reduce(jnp.maximum, [jnp.max(s, axis=0, keepdims=True) for s in parts])
            if state is not None:
                m, l = state[hd]
                m_new = jnp.maximum(m, m_new)
                alpha = jnp.exp2(m - m_new)
            prs = [jnp.exp2(s - m_new) for s in parts]
            l_new = functools.reduce(jnp.add, [jnp.sum(pr, axis=0, keepdims=True) for pr in prs])
            vc = (hd % n_heads // 2) if per_pair_kv else 0
            pv = None
            for (j, _), pr in zip(tiles, prs):
                vt = v_ref[pl.ds(pl.multiple_of(j * tile, tile), tile), vc * LANES:(vc + 1) * LANES]
                d = lax.dot_general(vt, pr.astype(BF16), tn, preferred_element_type=F32)
                pv = d if pv is None else pv + d
            if state is not None:
                l_new = alpha * l + l_new
                pv = alpha * acc_ref[hd] + pv
            res.append((m_new, l_new, pv))
        return res

    if n_back is not None:
        unit_tiles = []
        for u in range(q_tiles):
            iu = i * q_tiles + u
            tiles = [(jnp.maximum(iu - k, 0), jnp.where(iu >= k, k, n_bias - 1)) for k in range(n_back + 1)]
            unit_tiles += [tiles] * n_heads
        res = group(unit_tiles, None)
        for u in range(q_tiles):
            for p in range(n_pairs):
                (m0, l0, a0), (m1, l1, a1) = res[u * n_heads + 2 * p], res[u * n_heads + 2 * p + 1]
                finish(u, p, a0, l0, m0, a1, l1, m1)
        return

    def store(res):
        for hd, (_, _, acc) in enumerate(res):
            acc_ref[hd] = acc
        return tuple(x for m, l, _ in res for x in (m, l))

    def unpack(carry):
        return [(carry[2 * hd], carry[2 * hd + 1]) for hd in range(n_heads)]

    bias_of = lambda back: jnp.minimum(back, n_bias - 1)
    carry = store(group([[(i, 0)]] * n_heads, None))

    def pair_body(step, carry):
        back = 2 * step + 1
        tiles = [(i - back, bias_of(back)), (i - back - 1, bias_of(back + 1))]
        return store(group([tiles] * n_heads, unpack(carry)))

    carry = lax.fori_loop(0, i // 2, pair_body, carry)

    def last_body(_, carry):
        return store(group([[(0, bias_of(i))]] * n_heads, unpack(carry)))

    carry = lax.fori_loop(0, i % 2, last_body, carry)
    fin = unpack(carry)
    for p in range(n_pairs):
        finish(0, p, acc_ref[2 * p], fin[2 * p][1], fin[2 * p][0], acc_ref[2 * p + 1], fin[2 * p + 1][1], fin[2 * p + 1][0])


def _flash(q_arr, k_arr, v_arr, bias, *, batch, length, dil, tile, n_back, n_pairs, q_blk, k_blk, k_width,
           v_blk, v_width, per_pair_kv, heads, selb=None, with_lse=False, q_tiles=1):
    nq = length // tile
    with_sel = selb is not None
    width = n_pairs * LANES

    in_specs = [pl.BlockSpec((None, q_tiles * tile, width), lambda br, i: (br // dil, i, q_blk(br % dil))),
                pl.BlockSpec((None, length, k_width), lambda br, i: (br // dil, 0, k_blk(br % dil))),
                pl.BlockSpec((None, length, v_width), lambda br, i: (br // dil, 0, v_blk(br % dil))),
                pl.BlockSpec(bias.shape, lambda br, i: (0, 0, 0, 0))]
    args = [q_arr, k_arr, v_arr, bias]
    if with_sel:
        in_specs.append(pl.BlockSpec((None, q_tiles * tile, 2 * LANES), lambda br, i: (br, i, 0)))
        args.append(selb)
    o_map = lambda br, i: (br // dil, i, br % dil)
    out_specs = [pl.BlockSpec((None, q_tiles * tile, width), o_map)]
    out_shape = [jax.ShapeDtypeStruct((batch, length, dil * width), BF16)]
    if with_lse:
        out_specs.append(pl.BlockSpec((None, q_tiles * tile, width), o_map))
        out_shape.append(jax.ShapeDtypeStruct((batch, length, dil * width), F32))
    return pl.pallas_call(
        functools.partial(_flash_kernel, tile=tile, q_tiles=q_tiles, n_pairs=n_pairs, heads=heads, per_pair_kv=per_pair_kv,
                          n_back=n_back, n_bias=bias.shape[1], with_sel=with_sel, with_lse=with_lse),
        grid=(batch * dil, nq // q_tiles),
        in_specs=in_specs,
        out_specs=out_specs,
        out_shape=out_shape,
        scratch_shapes=[pltpu.VMEM((q_tiles * 2 * n_pairs, tile, LANES), BF16)]
        + ([pltpu.VMEM((2 * n_pairs, LANES, tile), F32)] if n_back is None else []),
        compiler_params=_cparams(("arbitrary", "arbitrary")),
        name="flash_sel" if with_sel else ("flash_dil" if with_lse else "flash_win"),
    )(*args)


def _mix_kernel(x_ref, g1_ref, oc_ref, os_ref, ow_ref, gl_ref, ge_ref,
                od0_ref, od1_ref, od2_ref, l0_ref, l1_ref, l2_ref, ml_ref,
                wn_ref, wd_ref, wo_ref, o_ref):
    d = x_ref.shape[1]
    sig = 1.0 / (1.0 + jnp.exp(-gl_ref[...]))
    hi = sig.astype(BF16)
    lo = (sig - hi.astype(F32)).astype(BF16)
    ge = ge_ref[...]
    gates = jnp.dot(hi, ge, preferred_element_type=F32) + jnp.dot(lo, ge, preferred_element_type=F32)
    o_nsa = (gates[:, 0:512] * oc_ref[...].astype(F32)
             + gates[:, 512:1024] * os_ref[...].astype(F32)
             + gates[:, 1024:1536] * ow_ref[...].astype(F32))
    u_nsa = jnp.dot(o_nsa.astype(BF16), wn_ref[...], preferred_element_type=F32)

    lses = (l0_ref[...], l1_ref[...], l2_ref[...])
    mx = jnp.maximum(jnp.maximum(lses[0], lses[1]), lses[2])
    es = [jnp.exp2(l - mx) for l in lses]
    inv = 1.0 / (es[0] + es[1] + es[2])
    o_dil = (es[0] * od0_ref[...].astype(F32) + es[1] * od1_ref[...].astype(F32)
             + es[2] * od2_ref[...].astype(F32)) * inv
    u_dil = jnp.dot(o_dil.astype(BF16), wd_ref[...], preferred_element_type=F32)

    gm = 1.0 / (1.0 + jnp.exp(-ml_ref[...].astype(F32)))
    merged = gm[:, :d] * u_nsa + gm[:, d:] * u_dil
    y = jnp.dot(merged.astype(BF16), wo_ref[...], preferred_element_type=F32)
    o_ref[...] = x_ref[...] + g1_ref[...] * y


def _mix(x2, g1, oc, osel, ow, gl, ge, od, lse, ml, wn, wd, wo, seq):
    t, d = x2.shape
    tm = TM_PROJ
    per_b = seq // tm
    row = lambda i: (i, 0)
    const = lambda i: (0, 0)
    rows = lambda a: pl.BlockSpec((tm, a.shape[1]), row)
    full = lambda a: pl.BlockSpec(a.shape, const)
    return pl.pallas_call(
        _mix_kernel,
        grid=(t // tm,),
        in_specs=[rows(x2), pl.BlockSpec((None, 1, d), lambda i: (i // per_b, 0, 0)),
                  rows(oc), rows(osel), rows(ow), rows(gl), full(ge),
                  rows(od[0]), rows(od[1]), rows(od[2]), rows(lse[0]), rows(lse[1]), rows(lse[2]), rows(ml),
                  full(wn), full(wd), full(wo)],
        out_specs=pl.BlockSpec((tm, d), row),
        out_shape=jax.ShapeDtypeStruct((t, d), F32),
        compiler_params=_cparams(("arbitrary",)),
        name="mix_outproj",
    )(x2, g1, oc, osel, ow, gl, ge, *od, *lse, ml, wn, wd, wo)


def _router_kernel(x_ref, sc_ref, sh_ref, g_ref, wr_ref, br_ref, h_ref, eid_ref, wts_ref):
    h = _norm_mod(x_ref[...], g_ref[...], sc_ref[...], sh_ref[...])
    n_sub = h.shape[1] // LANES
    for j in range(n_sub):
        h_ref[pl.ds(j, h.shape[0], stride=n_sub), :] = h[:, j * LANES:(j + 1) * LANES]
    logit = lax.dot_general(wr_ref[...], h, (((1,), (1,)), ((), ())), preferred_element_type=F32,
                            precision=lax.Precision.HIGHEST) + br_ref[...]
    grp = jnp.zeros((1, h.shape[0]), jnp.int32)
    best = logit[0:1]
    for k in range(1, N_EXPERT_GROUPS):
        better = logit[k:k + 1] > best
        grp = jnp.where(better, k, grp)
        best = jnp.where(better, logit[k:k + 1], best)
    den = jnp.zeros_like(best)
    for k in range(N_EXPERT_GROUPS):
        den = den + jnp.exp(logit[k:k + 1] - best)
    p_grp = 1.0 / den
    le = logit[SUBLANES:SUBLANES + EXPERTS_PER_GROUP]
    for k in range(1, N_EXPERT_GROUPS):
        lo = SUBLANES + k * EXPERTS_PER_GROUP
        le = jnp.where(grp == k, logit[lo:lo + EXPERTS_PER_GROUP], le)
    rowi = lax.broadcasted_iota(jnp.int32, le.shape, 0)
    v1 = jnp.max(le, axis=0, keepdims=True)
    i1 = jnp.min(jnp.where(le == v1, rowi, EXPERTS_PER_GROUP), axis=0, keepdims=True)
    rest = jnp.where(rowi == i1, -3e38, le)
    v2 = jnp.max(rest, axis=0, keepdims=True)
    i2 = jnp.min(jnp.where(rest == v2, rowi, EXPERTS_PER_GROUP), axis=0, keepdims=True)
    e2 = jnp.exp(v2 - v1)
    inv = p_grp / (1.0 + e2)
    eid_ref[...] = jnp.concatenate([grp * EXPERTS_PER_GROUP + i1, grp * EXPERTS_PER_GROUP + i2], axis=0)
    wts_ref[...] = jnp.concatenate([inv, e2 * inv], axis=0)


def _router(x2, sc, sh, g, wr_t, br, seq):
    t, d = x2.shape
    tm = TM_PROJ
    per_b = seq // tm
    row = lambda i: (i, 0)
    const = lambda i: (0, 0)
    return pl.pallas_call(
        _router_kernel,
        grid=(t // tm,),
        in_specs=[pl.BlockSpec((tm, d), row),
                  pl.BlockSpec((None, 1, d), lambda i: (i // per_b, 0, 0)),
                  pl.BlockSpec((None, 1, d), lambda i: (i // per_b, 0, 0)),
                  pl.BlockSpec((1, d), const),
                  pl.BlockSpec(wr_t.shape, const),
                  pl.BlockSpec(br.shape, const)],
        out_specs=[pl.BlockSpec((tm * (d // LANES), LANES), row),
                   pl.BlockSpec((2, tm), lambda i: (0, i)),
                   pl.BlockSpec((2, tm), lambda i: (0, i))],
        out_shape=[jax.ShapeDtypeStruct((t * (d // LANES), LANES), F32),
                   jax.ShapeDtypeStruct((2, t), jnp.int32),
                   jax.ShapeDtypeStruct((2, t), F32)],
        compiler_params=_cparams(("arbitrary",)),
        name="norm_router",
    )(x2, sc, sh, g, wr_t, br)


def _moe_kernel(nblk_ref, off_ref, tok_ref, wt_ref, h_ref, w1_ref, w3_ref, w2_ref, o_ref,
                xs_ref, os_ref, xb_ref, y3_ref):
    c = pl.program_id(0)
    e = pl.program_id(1)
    n_sub = xb_ref.shape[1] // LANES
    rb = xb_ref.shape[0]

    @pl.when(e == 0)
    def _():
        o_ref[...] = jnp.zeros_like(o_ref)

    off = off_ref[c, e]

    def every_row(base, fn):
        def grp(g, _):
            slot0 = base + g * MOE_UNROLL
            row0 = pl.multiple_of(g * (MOE_UNROLL * n_sub), MOE_UNROLL * n_sub)
            for u in range(MOE_UNROLL):
                fn(slot0 + u, pl.ds(row0 + u * n_sub, n_sub))
            return 0

        lax.fori_loop(0, rb // MOE_UNROLL, grp, 0)

    def token_rows(slot):
        return pl.ds(pl.multiple_of(tok_ref[0, slot], n_sub), n_sub)

    def block(bi, _):
        base = off + bi * rb

        def gather(slot, rows):
            src = token_rows(slot)
            xs_ref[rows, :] = h_ref[src, :]
            os_ref[rows, :] = o_ref[src, :]

        every_row(base, gather)
        for j in range(n_sub):
            xb_ref[:, j * LANES:(j + 1) * LANES] = xs_ref[pl.ds(j, rb, stride=n_sub), :].astype(BF16)
        xb = xb_ref[...]
        a = jnp.dot(xb, w1_ref[...], preferred_element_type=F32)
        b = jnp.dot(xb, w3_ref[...], preferred_element_type=F32)
        mid = (a * (1.0 / (1.0 + jnp.exp(-a))) * b).astype(BF16)
        y = jnp.dot(mid, w2_ref[...], preferred_element_type=F32)
        for j in range(n_sub):
            y3_ref[pl.ds(j, rb, stride=n_sub), :] = y[:, j * LANES:(j + 1) * LANES]

        def scatter(slot, rows):
            o_ref[token_rows(slot), :] = os_ref[rows, :] + wt_ref[0, slot] * y3_ref[rows, :]

        every_row(base, scatter)
        return 0

    lax.fori_loop(0, nblk_ref[c, e], block, 0)


def _moe(h2, eid, wts, w1, w3, w2):
    n_sub = w1.shape[1] // LANES
    t, d = h2.shape[0] // n_sub, w1.shape[1]
    tc = min(MOE_CHUNK, t)
    n_chunks = t // tc
    slots = 2 * tc
    tok = jnp.arange(t, dtype=jnp.int32)
    key = ((tok // tc)[None, :] * N_EXPERTS + eid).reshape(-1)
    order = jnp.argsort(key)
    tok_sorted = (jnp.tile(tok % tc * n_sub, 2)[order]).reshape(n_chunks, slots)
    w_sorted = wts.reshape(-1)[order].reshape(n_chunks, slots)
    counts = jnp.zeros((n_chunks * N_EXPERTS,), jnp.int32).at[key].add(1).reshape(n_chunks, N_EXPERTS)
    starts = jnp.cumsum(counts, axis=1) - counts
    nblk = (counts + MOE_ROWS - 1) // MOE_ROWS
    pends = jnp.cumsum(nblk * MOE_ROWS, axis=1)
    offs = pends - nblk * MOE_ROWS
    slots_p = slots + N_EXPERTS * MOE_ROWS
    q = jnp.arange(slots_p, dtype=jnp.int32)[None, :, None]
    mine = (q >= offs[:, None, :]) & (q < pends[:, None, :])
    pick = lambda a: jnp.sum(jnp.where(mine, a[:, None, :], 0), axis=-1)
    src = pick(starts) + jnp.clip(q[:, :, 0] - pick(offs), 0, jnp.maximum(pick(counts) - 1, 0))
    flat = (jnp.arange(n_chunks, dtype=jnp.int32)[:, None] * slots + src).reshape(-1)
    tok_sorted = tok_sorted.reshape(-1)[flat].reshape(n_chunks, 1, slots_p)
    w_sorted = w_sorted.reshape(-1)[flat].reshape(n_chunks, 1, slots_p)

    grid_spec = pltpu.PrefetchScalarGridSpec(
        num_scalar_prefetch=2,
        grid=(n_chunks, N_EXPERTS),
        in_specs=[pl.BlockSpec((None, 1, slots_p), lambda c, e, *_: (c, 0, 0), memory_space=pltpu.SMEM),
                  pl.BlockSpec((None, 1, slots_p), lambda c, e, *_: (c, 0, 0), memory_space=pltpu.SMEM),
                  pl.BlockSpec((tc * n_sub, LANES), lambda c, e, *_: (c, 0)),
                  pl.BlockSpec((None, d, D_EXPERT), lambda c, e, *_: (e, 0, 0)),
                  pl.BlockSpec((None, d, D_EXPERT), lambda c, e, *_: (e, 0, 0)),
                  pl.BlockSpec((None, D_EXPERT, d), lambda c, e, *_: (e, 0, 0))],
        out_specs=pl.BlockSpec((tc * n_sub, LANES), lambda c, e, *_: (c, 0)),
        scratch_shapes=[pltpu.VMEM((MOE_ROWS * n_sub, LANES), F32),
                        pltpu.VMEM((MOE_ROWS * n_sub, LANES), F32),
                        pltpu.VMEM((MOE_ROWS, d), BF16),
                        pltpu.VMEM((MOE_ROWS * n_sub, LANES), F32)],
    )
    out = pl.pallas_call(
        _moe_kernel,
        grid_spec=grid_spec,
        out_shape=jax.ShapeDtypeStruct((t * n_sub, LANES), F32),
        compiler_params=_cparams(("arbitrary", "arbitrary")),
        name="moe_experts",
    )(nblk, offs, tok_sorted, w_sorted, h2, w1, w3, w2)
    return out


def _resid_kernel(x_ref, y_ref, g_ref, nf_ref, o_ref, *, final):
    tm, d = x_ref.shape
    n_sub = d // LANES
    y = jnp.concatenate([y_ref[pl.ds(j, tm, stride=n_sub), :] for j in range(n_sub)], axis=1)
    x = x_ref[...] + g_ref[...] * y
    if final:
        ms = jnp.mean(x * x, axis=-1, keepdims=True)
        x = x * lax.rsqrt(ms + RMS_EPS) * nf_ref[...]
    o_ref[...] = x


def _residual(x2, y2, g2, norm_f, seq, final):
    t, d = x2.shape
    tm = TM_PROJ
    per_b = seq // tm
    row = lambda i: (i, 0)
    return pl.pallas_call(
        functools.partial(_resid_kernel, final=final),
        grid=(t // tm,),
        in_specs=[pl.BlockSpec((tm, d), row), pl.BlockSpec((tm * (d // LANES), LANES), row),
                  pl.BlockSpec((None, 1, d), lambda i: (i // per_b, 0, 0)),
                  pl.BlockSpec((1, d), lambda i: (0, 0))],
        out_specs=pl.BlockSpec((tm, d), row),
        out_shape=jax.ShapeDtypeStruct((t, d), F32),
        compiler_params=_cparams(("arbitrary",)),
        name="residual_final" if final else "residual",
    )(x2, y2, g2, norm_f)


def _split_w_in(w_in, d):
    scale = HEAD_DIM ** -0.5 * LOG2E
    nq = NSA_HEADS * HEAD_DIM
    nkv = 3 * 2 * NSA_KV_HEADS * HEAD_DIM
    ngate = 3 * NSA_HEADS
    ndil = 3 * N_DIL_GROUPS * DIL_HEADS_PER_GROUP * HEAD_DIM
    o1, o2, o3 = nq, nq + nkv, nq + nkv + ngate
    o4 = o3 + ndil
    wq = (w_in[:, :o1] * scale).reshape(d, NSA_KV_HEADS, NSA_GROUP, HEAD_DIM)
    wq = wq.transpose(0, 2, 1, 3).reshape(d, nq)
    wkv = w_in[:, o1:o2]
    wg = jnp.pad(w_in[:, o2:o3], ((0, 0), (0, LANES - ngate)))
    gw = DIL_HEADS_PER_GROUP * HEAD_DIM
    per_which = N_DIL_GROUPS * gw
    wds = []
    for grp in range(N_DIL_GROUPS):
        parts = [w_in[:, o3 + which * per_which + grp * gw: o3 + which * per_which + (grp + 1) * gw]
                 for which in range(3)]
        parts[0] = parts[0] * scale
        wds.append(jnp.concatenate(parts, axis=1))
    wm = w_in[:, o4:]
    cast = lambda w: w.astype(BF16)
    return [cast(wq), cast(wkv), cast(wds[0]), cast(wds[1]), cast(wds[2]), cast(wm), cast(wg)]


def kernel(x, c, rel_bias, ada_w, ada_b, norm1, norm2, w_in, cmp_pos, cmp_w1, cmp_w2, w_up_nsa, w_up_dil, w_o,
           router_wg, router_bg, router_we, router_be, exp_w1, exp_w3, exp_w2, norm_f):
    batch, seq, d = x.shape
    depth = ada_w.shape[0]
    t = batch * seq
    n_cmp = seq // CMP_STRIDE
    n_sel = seq // SEL_BLOCK
    assert seq % TQ_NSA == 0 and n_sel <= HEAD_DIM and n_sel >= SEL_TOPN
    assert all(seq % (dil * TQ_DIL) == 0 for _, dil in DIL_PAIRS)

    mod = _modulation(c, ada_w, ada_b)

    nq_nsa = seq // TQ_NSA
    bias_cmp = _expand_bias(rel_bias, _cmp_buckets(seq, n_cmp), 0, NSA_HEADS, 8 * SUBLANES)
    sel_b = _toeplitz_buckets(nq_nsa, TQ_NSA, seq, 1)
    n_sel_bias = nq_nsa
    while n_sel_bias > 1 and (sel_b[n_sel_bias - 2:] == sel_b[n_sel_bias - 1, 0, 0]).all():
        n_sel_bias -= 1
    sel_b = sel_b[:n_sel_bias].reshape(n_sel_bias * TQ_NSA, TQ_NSA)
    bias_sel = _expand_bias(rel_bias, sel_b, 0, NSA_HEADS, TQ_NSA).reshape(NSA_HEADS, n_sel_bias, TQ_NSA, TQ_NSA)
    nb_win = min(-(-NSA_WINDOW // TQ_NSA), nq_nsa - 1)
    win_b = _toeplitz_buckets(nb_win + 1, TQ_NSA, NSA_WINDOW, 1, masked_tail=True).reshape(-1, TQ_NSA)
    bias_win = _expand_bias(rel_bias, win_b, 0, NSA_HEADS, TQ_NSA).reshape(NSA_HEADS, nb_win + 2, TQ_NSA, TQ_NSA)
    bias_dil, nb_dil = [], []
    for grp, (window, dil) in enumerate(DIL_PAIRS):
        nb = min(-(-(window // dil) // TQ_DIL), seq // dil // TQ_DIL - 1)
        bk = _toeplitz_buckets(nb + 1, TQ_DIL, window // dil, dil, masked_tail=True).reshape(-1, TQ_DIL)
        hb = NSA_HEADS + grp * DIL_HEADS_PER_GROUP
        bias_dil.append(_expand_bias(rel_bias, bk, hb, DIL_HEADS_PER_GROUP, TQ_DIL)
                        .reshape(DIL_HEADS_PER_GROUP, nb + 2, TQ_DIL, TQ_DIL))
        nb_dil.append(nb)
    ovl_t = jnp.asarray(_overlap_t(n_cmp, n_sel), BF16)
    onehot = jnp.asarray(_block_onehot(seq), BF16)
    gate_e = jnp.asarray(_gate_expand(), BF16)

    x2 = x.reshape(t, d)
    for l in range(depth):
        sh1, sc1, g1, sh2, sc2, g2 = [m.reshape(batch, 1, d) for m in jnp.split(mod[l], 6, axis=-1)]
        weights = _split_w_in(w_in[l], d)
        (q_n, kc_in, vc_in, k_sel, v_sel, k_win, v_win, qkv_d0, qkv_d1, qkv_d2, merge_l, gate_l) = _inproj(
            x2, sc1, sh1, norm1[l].reshape(1, d), onehot, weights, seq)

        kc = _compress(kc_in, cmp_pos[l, 0], cmp_w1[l, 0], cmp_w2[l, 0], batch, seq, transpose_out=False)
        vct = _compress(vc_in, cmp_pos[l, 1], cmp_w1[l, 1], cmp_w2[l, 1], batch, seq, transpose_out=True)
        o_c, selb = _cmp_select(q_n, kc, vct, bias_cmp, ovl_t, batch, seq)
        nsa_common = dict(batch=batch, length=seq, dil=1, tile=TQ_NSA, n_pairs=NSA_GROUP, q_blk=lambda r: 0,
                          k_blk=lambda r: 0, v_blk=lambda r: 0, v_width=LANES, per_pair_kv=False,
                          heads=tuple((g, NSA_GROUP + g) for g in range(NSA_GROUP)))
        q3 = q_n.reshape(batch, seq, 512)
        (o_s,) = _flash(q3, k_sel.reshape(batch, seq, 2 * LANES), v_sel.reshape(batch, seq, LANES), bias_sel,
                        n_back=None, k_width=2 * LANES, selb=selb.reshape(batch, seq, 2 * LANES), **nsa_common)
        (o_w,) = _flash(q3, k_win.reshape(batch, seq, LANES), v_win.reshape(batch, seq, LANES), bias_win,
                        n_back=nb_win, k_width=LANES, **nsa_common)

        o_d, lse_d = [], []
        for grp, ((window, dil), qkv) in enumerate(zip(DIL_PAIRS, (qkv_d0, qkv_d1, qkv_d2))):
            length = seq // dil
            view = qkv.reshape(batch, length, dil * 768)
            o_g, lse_g = _flash(view, view, view, bias_dil[grp], batch=batch, length=length, dil=dil, tile=TQ_DIL,
                                n_back=nb_dil[grp], n_pairs=2, q_blk=lambda r: 3 * r, k_blk=lambda r: 3 * r + 1,
                                k_width=2 * LANES, v_blk=lambda r: 3 * r + 2, v_width=2 * LANES,
                                per_pair_kv=True, heads=((0, 1), (2, 3)), with_lse=True,
                                q_tiles=math.gcd(DIL_Q_TILES, length // TQ_DIL))
            o_d.append(o_g.reshape(t, 256))
            lse_d.append(lse_g.reshape(t, 256))

        wn = w_up_nsa[l].reshape(NSA_KV_HEADS, NSA_GROUP, HEAD_DIM, d).transpose(1, 0, 2, 3).reshape(512, d)
        x2 = _mix(x2, g1, o_c, o_s.reshape(t, 512), o_w.reshape(t, 512), gate_l, gate_e, o_d, lse_d, merge_l,
                  wn.astype(BF16), w_up_dil[l].astype(BF16), w_o[l].astype(BF16), seq)

        wr_t = jnp.concatenate([jnp.pad(router_wg[l], ((0, 0), (0, SUBLANES - N_EXPERT_GROUPS))),
                                router_we[l]], axis=1).T
        br = jnp.concatenate([jnp.pad(router_bg[l], (0, SUBLANES - N_EXPERT_GROUPS)),
                              router_be[l]]).reshape(-1, 1)
        h2, eid, wts = _router(x2, sc2, sh2, norm2[l].reshape(1, d), wr_t, br, seq)
        y = _moe(h2, eid, wts, exp_w1[l].astype(BF16), exp_w3[l].astype(BF16), exp_w2[l].astype(BF16))
        x2 = _residual(x2, y, g2, norm_f.reshape(1, d), seq, final=(l == depth - 1))
    return x2.reshape(batch, seq, d)
```

```python
import functools
import math

import numpy as np
import jax
import jax.numpy as jnp
from jax import lax
from jax.experimental import pallas as pl
from jax.experimental.pallas import tpu as pltpu

F32 = jnp.float32
BF16 = jnp.bfloat16

HEAD_DIM = 64
NSA_HEADS = 8
NSA_KV_HEADS = 2
NSA_GROUP = NSA_HEADS // NSA_KV_HEADS
CMP_LEN = 32
CMP_STRIDE = 16
CMP_HIDDEN = 256
SEL_BLOCK = 64
SEL_TOPN = 16
NSA_WINDOW = 512
FORCE_SCORE = 1e4
DIL_PAIRS = ((128, 1), (512, 4), (2048, 16))
N_DIL_GROUPS = 3
DIL_HEADS_PER_GROUP = 4
NUM_BUCKETS = 32
REL_MAX_DIST = 2048
N_EXPERT_GROUPS = 4
EXPERTS_PER_GROUP = 8
N_EXPERTS = N_EXPERT_GROUPS * EXPERTS_PER_GROUP
D_EXPERT = 512
RMS_EPS = 1e-6

LOG2E = math.log2(math.e)
LANES = 128
SUBLANES = 8
MASK_NEG = -1e30
SEL_NEG = -1e9
TQ_NSA = 256
TQ_DIL = 128
TM_PROJ = 512
DIL_Q_TILES = 4
FLASH_LOOKAHEAD = 4
MOE_CHUNK = 2048
MOE_ROWS = 128
MOE_UNROLL = 16
VMEM_LIMIT = 56 * 1024 * 1024


def _cparams(sem):
    return pltpu.CompilerParams(dimension_semantics=sem, vmem_limit_bytes=VMEM_LIMIT)


def _np_bucket(dist):
    dist = np.maximum(dist, 0)
    max_exact = NUM_BUCKETS // 2
    df = np.maximum(dist, 1).astype(np.float32)
    val = np.log(df / np.float32(max_exact)) / np.float32(math.log(REL_MAX_DIST / max_exact))
    large = max_exact + (val * np.float32(NUM_BUCKETS - max_exact)).astype(np.int32)
    large = np.minimum(large, NUM_BUCKETS - 1)
    return np.where(dist < max_exact, dist, large).astype(np.int32)


def _toeplitz_buckets(n_delta, tile, window, dist_scale, masked_tail=False):
    dd = np.arange(n_delta)[:, None, None]
    r = np.arange(tile)[None, :, None]
    c = np.arange(tile)[None, None, :]
    dist = dd * tile + c - r
    valid = (dist >= 0) & (dist <= window)
    out = np.where(valid, _np_bucket(dist * dist_scale), -1).astype(np.int32)
    if masked_tail:
        out = np.concatenate([out, np.full((1, tile, tile), -1, np.int32)], axis=0)
    return out


def _cmp_buckets(seq, n_cmp):
    c_end = np.arange(n_cmp)[:, None] * CMP_STRIDE + CMP_LEN - 1
    t = np.arange(seq)[None, :]
    dist = t - c_end
    return np.where(dist >= 0, _np_bucket(dist), -1).astype(np.int32)


def _overlap_t(n_cmp, n_sel):
    c_start = np.arange(n_cmp)[None, :] * CMP_STRIDE
    s_start = np.arange(n_sel)[:, None] * SEL_BLOCK
    ov = np.clip(np.minimum(c_start + CMP_LEN, s_start + SEL_BLOCK) - np.maximum(c_start, s_start), 0, None)
    out = np.zeros((LANES, n_cmp), np.float32)
    out[64:64 + n_sel] = ov.astype(np.float32) / CMP_STRIDE
    return out


def _block_onehot(seq):
    oh = np.zeros((seq, LANES), np.float32)
    blk = np.arange(seq) // SEL_BLOCK
    oh[np.arange(seq), blk] = 1.0
    oh[np.arange(seq), 64 + blk] = 1.0
    return oh


def _gate_expand():
    e = np.zeros((LANES, 3 * NSA_HEADS * HEAD_DIM), np.float32)
    for br in range(3):
        for g in range(NSA_GROUP):
            for ln in range(LANES):
                kv = ln // HEAD_DIM
                e[br * NSA_HEADS + kv * NSA_GROUP + g, br * 512 + g * LANES + ln] = 1.0
    return e


def _mod_kernel(c_ref, w_ref, b_ref, o_ref):
    c = c_ref[...]
    cond = c * (1.0 / (1.0 + jnp.exp(-c)))
    o_ref[...] = jnp.dot(cond, w_ref[...], preferred_element_type=F32,
                         precision=lax.Precision.HIGHEST) + b_ref[...]


def _modulation(c, ada_w, ada_b):
    depth, d, n = ada_w.shape
    b = c.shape[0]
    tn = 1536
    return pl.pallas_call(
        _mod_kernel,
        grid=(depth, n // tn),
        in_specs=[pl.BlockSpec((b, d), lambda l, j: (0, 0)),
                  pl.BlockSpec((None, d, tn), lambda l, j: (l, 0, j)),
                  pl.BlockSpec((None, 1, tn), lambda l, j: (l, 0, j))],
        out_specs=pl.BlockSpec((None, b, tn), lambda l, j: (l, 0, j)),
        out_shape=jax.ShapeDtypeStruct((depth, b, n), F32),
        compiler_params=_cparams(("arbitrary", "arbitrary")),
        name="adaln_modulation",
    )(c, ada_w, ada_b.reshape(depth, 1, n))


def _bias_kernel(tbl_ref, bkt_ref, o_ref, *, head_base):
    h = pl.program_id(0) + head_base
    bkt = bkt_ref[...]
    acc = jnp.full(bkt.shape, MASK_NEG, F32)
    for b in range(NUM_BUCKETS):
        acc = jnp.where(bkt == b, tbl_ref[b, h] * LOG2E, acc)
    o_ref[...] = acc


def _expand_bias(rel_bias, buckets, head_base, n_heads, row_tile):
    rows, cols = buckets.shape
    return pl.pallas_call(
        functools.partial(_bias_kernel, head_base=head_base),
        grid=(n_heads, rows // row_tile),
        in_specs=[pl.BlockSpec(memory_space=pltpu.SMEM),
                  pl.BlockSpec((row_tile, cols), lambda h, i: (i, 0))],
        out_specs=pl.BlockSpec((None, row_tile, cols), lambda h, i: (h, i, 0)),
        out_shape=jax.ShapeDtypeStruct((n_heads, rows, cols), F32),
        compiler_params=_cparams(("arbitrary", "arbitrary")),
        name="bias_expand",
    )(rel_bias, jnp.asarray(buckets))


def _norm_mod(x, g, sc, sh):
    ms = jnp.mean(x * x, axis=-1, keepdims=True)
    y = x * lax.rsqrt(ms + RMS_EPS) * g
    return y * (1.0 + sc) + sh


def _inproj_kernel(x_ref, sc_ref, sh_ref, g_ref, oh_ref, wq_ref, wkv_ref, wd0_ref, wd1_ref, wd2_ref, wm_ref, wg_ref,
                   q_ref, kc_ref, vc_ref, ks_ref, vs_ref, kw_ref, vw_ref, d0_ref, d1_ref, d2_ref, m_ref, gl_ref,
                   slab_ref):
    h = _norm_mod(x_ref[...], g_ref[...], sc_ref[...], sh_ref[...]).astype(BF16)
    tm = h.shape[0]

    def by_residue(w_ref, o_ref):
        dil = o_ref.shape[0]
        res = jnp.dot(h, w_ref[...], preferred_element_type=F32)
        n_slab = res.shape[1] // LANES
        for c in range(n_slab):
            slab_ref[c] = res[:, c * LANES:(c + 1) * LANES]
        for r in range(dil):
            for c in range(n_slab):
                o_ref[r, :, c * LANES:(c + 1) * LANES] = slab_ref[c, pl.ds(r, tm // dil, stride=dil), :].astype(BF16)

    def proj(w_ref):
        return jnp.dot(h, w_ref[...], preferred_element_type=F32)

    q_ref[...] = proj(wq_ref).astype(BF16)
    kv = proj(wkv_ref).astype(BF16)
    for k, ref in ((0, kc_ref), (1, vc_ref), (3, vs_ref), (4, kw_ref), (5, vw_ref)):
        ref[...] = kv[:, k * LANES:(k + 1) * LANES]
    k_sel = kv[:, 2 * LANES:3 * LANES]
    oh = oh_ref[...]
    lo = lax.broadcasted_iota(jnp.int32, k_sel.shape, 1) < HEAD_DIM
    ks_ref[:, :LANES] = jnp.where(lo, k_sel, oh)
    ks_ref[:, LANES:] = jnp.where(lo, oh, k_sel)
    d0_ref[...] = proj(wd0_ref).astype(BF16)
    by_residue(wd1_ref, d1_ref)
    by_residue(wd2_ref, d2_ref)
    m_ref[...] = proj(wm_ref).astype(BF16)
    gl_ref[...] = proj(wg_ref)


def _inproj(x2, sc, sh, g, onehot, weights, seq):
    t, d = x2.shape
    tm = TM_PROJ
    per_b = seq // tm
    widths = (512, 128, 128, 256, 128, 128, 128, 768, 768, 768, 2 * d, LANES)
    dtypes = (BF16,) * 11 + (F32,)
    row = lambda i: (i, 0)
    const = lambda i: (0, 0)
    in_specs = [pl.BlockSpec((tm, d), row),
                pl.BlockSpec((None, 1, d), lambda i: (i // per_b, 0, 0)),
                pl.BlockSpec((None, 1, d), lambda i: (i // per_b, 0, 0)),
                pl.BlockSpec((1, d), const),
                pl.BlockSpec((tm, LANES), lambda i: (i % per_b, 0))]
    in_specs += [pl.BlockSpec(w.shape, const) for w in weights]
    out_specs = [pl.BlockSpec((tm, w), row) for w in widths]
    out_shape = [jax.ShapeDtypeStruct((t, w), dt) for w, dt in zip(widths, dtypes)]
    for k, (_, dil) in zip((8, 9), DIL_PAIRS[1:]):
        out_specs[k] = pl.BlockSpec((None, dil, tm // dil, 768), lambda i: (i // per_b, 0, i % per_b, 0))
        out_shape[k] = jax.ShapeDtypeStruct((t // seq, dil, seq // dil, 768), BF16)
    return pl.pallas_call(
        _inproj_kernel,
        grid=(t // tm,),
        in_specs=in_specs,
        out_specs=out_specs,
        out_shape=out_shape,
        scratch_shapes=[pltpu.VMEM((768 // LANES, tm, LANES), F32)],
        compiler_params=_cparams(("arbitrary",)),
        name="norm_inproj",
    )(x2, sc, sh, g, onehot, *weights)


def _gelu_tanh(x):
    return 0.5 * x * (1.0 + jnp.tanh(math.sqrt(2.0 / math.pi) * (x + 0.044715 * (x * x * x))))


def _compress_kernel(r_ref, pt_ref, pb_ref, wt_ref, wb_ref, w2_ref, o_ref, *, transpose_out):
    r = r_ref[...].astype(F32)
    top = jnp.dot((r + pt_ref[...]).astype(BF16), wt_ref[...], preferred_element_type=F32)
    bot = jnp.dot((r + pb_ref[...]).astype(BF16), wb_ref[...], preferred_element_type=F32)
    n = bot.shape[0]
    hid = top + pltpu.roll(bot, n - 1, 0)
    act = _gelu_tanh(hid).astype(BF16)
    if transpose_out:
        o_ref[...] = lax.dot_general(w2_ref[...], act, (((1,), (1,)), ((), ())),
                                     preferred_element_type=F32).astype(BF16)
    else:
        o_ref[...] = jnp.dot(act, w2_ref[...], preferred_element_type=F32).astype(BF16)


def _compress(tok, pos, w1, w2, batch, seq, transpose_out):
    nc = seq // CMP_STRIDE
    half = CMP_LEN // 2
    eye = jnp.eye(NSA_KV_HEADS, dtype=F32)
    w1r = w1.reshape(CMP_LEN, HEAD_DIM, CMP_HIDDEN)
    blk = lambda w: jnp.einsum('ldn,hg->lhdgn', w, eye).reshape(half * LANES, 2 * CMP_HIDDEN).astype(BF16)
    wt, wb = blk(w1r[:half]), blk(w1r[half:])
    posr = lambda p: jnp.broadcast_to(p[:, None, :], (half, NSA_KV_HEADS, HEAD_DIM)).reshape(1, half * LANES)
    pt, pb = posr(pos[:half]), posr(pos[half:])
    w2b = jnp.einsum('nd,hg->hngd', w2, eye).reshape(2 * CMP_HIDDEN, LANES)
    if transpose_out:
        w2b = w2b.T
        out_block, out_shape = (None, LANES, nc), (batch, LANES, nc)
    else:
        out_block, out_shape = (None, nc, LANES), (batch, nc, LANES)
    w2b = w2b.astype(BF16)
    const = lambda b: (0, 0)
    return pl.pallas_call(
        functools.partial(_compress_kernel, transpose_out=transpose_out),
        grid=(batch,),
        in_specs=[pl.BlockSpec((None, nc, half * LANES), lambda b: (b, 0, 0)),
                  pl.BlockSpec(pt.shape, const), pl.BlockSpec(pb.shape, const),
                  pl.BlockSpec(wt.shape, const), pl.BlockSpec(wb.shape, const),
                  pl.BlockSpec(w2b.shape, const)],
        out_specs=pl.BlockSpec(out_block, lambda b: (b, 0, 0)),
        out_shape=jax.ShapeDtypeStruct(out_shape, BF16),
        compiler_params=_cparams(("arbitrary",)),
        name="nsa_compress",
    )(tok.reshape(batch, nc, half * LANES), pt, pb, wt, wb, w2b)


def _cmp_select_kernel(q_ref, kc_ref, vct_ref, bias_ref, ovl_ref, o_ref, selb_ref, *, n_sel, n_top):
    tq = q_ref.shape[0]
    nc = kc_ref.shape[0]
    qs = pl.program_id(1) * tq
    kc = kc_ref[...]
    vct = vct_ref[...]
    lane_q = lax.broadcasted_iota(jnp.int32, (tq, LANES), 1)
    row_o = lax.broadcasted_iota(jnp.int32, (LANES, tq), 0)
    heads = [(g, kv) for g in range(NSA_GROUP) for kv in range(NSA_KV_HEADS)]

    def scores(g, kv):
        qt = q_ref[:, g * LANES:(g + 1) * LANES]
        mine = (lane_q < HEAD_DIM) if kv == 0 else (lane_q >= HEAD_DIM)
        qm = jnp.where(mine, qt, jnp.zeros_like(qt))
        s = lax.dot_general(kc, qm, (((1,), (1,)), ((), ())), preferred_element_type=F32)
        return s + bias_ref[kv * NSA_GROUP + g]

    pc_sum = [jnp.zeros((nc, tq), F32) for _ in range(NSA_KV_HEADS)]
    outs = {}
    ahead = [scores(*hd) for hd in heads[:FLASH_LOOKAHEAD]]
    for n, (g, kv) in enumerate(heads):
        s = ahead.pop(0)
        if n + FLASH_LOOKAHEAD < len(heads):
            ahead.append(scores(*heads[n + FLASH_LOOKAHEAD]))
        m = jnp.max(s, axis=0, keepdims=True)
        m = jnp.where(m < 0.5 * MASK_NEG, 0.0, m)
        p = jnp.exp2(s - m)
        den = jnp.sum(p, axis=0, keepdims=True)
        pc = p * (1.0 / jnp.where(den > 0.0, den, 1.0))
        pc_sum[kv] = pc_sum[kv] + pc
        outs[kv] = jnp.dot(vct, pc.astype(BF16), preferred_element_type=F32)
        if kv == NSA_KV_HEADS - 1:
            o_t = jnp.where(row_o < HEAD_DIM, outs[0], outs[1])
            o_ref[:, g * LANES:(g + 1) * LANES] = o_t.T.astype(BF16)

    ovl = ovl_ref[...]
    imps = []
    for kv in range(NSA_KV_HEADS):
        hi = pc_sum[kv].astype(BF16)
        lo = (pc_sum[kv] - hi.astype(F32)).astype(BF16)
        imps.append((jnp.dot(ovl, hi, preferred_element_type=F32)
                     + jnp.dot(ovl, lo, preferred_element_type=F32))[HEAD_DIM:])
    imp = jnp.concatenate(imps, axis=1)
    wide = (HEAD_DIM, NSA_KV_HEADS * tq)
    rowj = lax.broadcasted_iota(jnp.int32, wide, 0)
    col = lax.broadcasted_iota(jnp.int32, wide, 1)
    t = qs + jnp.where(col >= tq, col - tq, col)
    jq = jnp.right_shift(t, SEL_BLOCK.bit_length() - 1)
    forced = (rowj == 0) | (rowj == jq) | (rowj == jq - 1)
    score = jnp.where(forced, FORCE_SCORE, jnp.where(rowj > jq, -1.0, imp))
    rem = jnp.where(rowj < n_sel, score, -3e38)
    sel = jnp.zeros(wide, F32)
    for _ in range(n_top):
        m = jnp.max(rem, axis=0, keepdims=True)
        idx = jnp.min(jnp.where(rem == m, rowj, HEAD_DIM), axis=0, keepdims=True)
        pick = rowj == idx
        sel = jnp.where(pick, 1.0, sel)
        rem = jnp.where(pick, -3e38, rem)
    sb = jnp.where(sel > 0.5, 0.0, SEL_NEG)
    zero = jnp.zeros((HEAD_DIM, tq), F32)
    selb_ref[:, :LANES] = jnp.concatenate([zero, sb[:, :tq]], axis=0).T.astype(BF16)
    selb_ref[:, LANES:] = jnp.concatenate([sb[:, tq:], zero], axis=0).T.astype(BF16)


def _cmp_select(q, kc, vct, bias_c, ovl_t, batch, seq):
    t = q.shape[0]
    tq = TQ_NSA
    nq = seq // tq
    nc = seq // CMP_STRIDE
    n_sel = seq // SEL_BLOCK
    n_top = min(SEL_TOPN, n_sel)
    return pl.pallas_call(
        functools.partial(_cmp_select_kernel, n_sel=n_sel, n_top=n_top),
        grid=(batch, nq),
        in_specs=[pl.BlockSpec((tq, 512), lambda b, i: (b * nq + i, 0)),
                  pl.BlockSpec((None, nc, LANES), lambda b, i: (b, 0, 0)),
                  pl.BlockSpec((None, LANES, nc), lambda b, i: (b, 0, 0)),
                  pl.BlockSpec((NSA_HEADS, nc, tq), lambda b, i: (0, 0, i)),
                  pl.BlockSpec((LANES, nc), lambda b, i: (0, 0))],
        out_specs=[pl.BlockSpec((tq, 512), lambda b, i: (b * nq + i, 0)),
                   pl.BlockSpec((tq, 2 * LANES), lambda b, i: (b * nq + i, 0))],
        out_shape=[jax.ShapeDtypeStruct((t, 512), BF16),
                   jax.ShapeDtypeStruct((t, 2 * LANES), BF16)],
        compiler_params=_cparams(("arbitrary", "arbitrary")),
        name="nsa_cmp_select",
    )(q, kc, vct, bias_c, ovl_t)


def _flash_kernel(*refs, tile, q_tiles, n_pairs, heads, per_pair_kv, n_back, n_bias, with_sel, with_lse):
    it = iter(refs)
    q_ref, k_ref, v_ref, b_ref = (next(it) for _ in range(4))
    selb_ref = next(it) if with_sel else None
    o_ref = next(it)
    lse_ref = next(it) if with_lse else None
    qa_ref = next(it)
    acc_ref = next(it) if n_back is None else None

    i = pl.program_id(1)
    n_heads = 2 * n_pairs
    lo = lax.broadcasted_iota(jnp.int32, (tile, LANES), 1) < HEAD_DIM
    for u in range(q_tiles):
        rows = slice(u * tile, (u + 1) * tile)
        for p in range(n_pairs):
            q = q_ref[rows, p * LANES:(p + 1) * LANES]
            if with_sel:
                qa_ref[u * n_heads + 2 * p] = jnp.where(lo, q, selb_ref[rows, :LANES])
                qa_ref[u * n_heads + 2 * p + 1] = jnp.where(lo, selb_ref[rows, LANES:], q)
            else:
                zero = jnp.zeros_like(q)
                qa_ref[u * n_heads + 2 * p] = jnp.where(lo, q, zero)
                qa_ref[u * n_heads + 2 * p + 1] = jnp.where(lo, zero, q)
    nt = (((1,), (1,)), ((), ()))
    tn = (((0,), (0,)), ((), ()))
    top = lax.broadcasted_iota(jnp.int32, (LANES, tile), 0) < HEAD_DIM

    def finish(u, p, acc0, l0, m0, acc1, l1, m1):
        rows = slice(u * tile, (u + 1) * tile)
        o_t = jnp.where(top, acc0 * (1.0 / l0), acc1 * (1.0 / l1))
        o_ref[rows, p * LANES:(p + 1) * LANES] = o_t.T.astype(o_ref.dtype)
        if with_lse:
            lse_t = jnp.where(top, m0 + jnp.log2(l0), m1 + jnp.log2(l1))
            lse_ref[rows, p * LANES:(p + 1) * LANES] = lse_t.T

    def scores(unit, tiles):
        hd = unit % n_heads
        p, half = divmod(hd, 2)
        kc = half if with_sel else (p if per_pair_kv else 0)
        parts = []
        for j, bidx in tiles:
            kt = k_ref[pl.ds(pl.multiple_of(j * tile, tile), tile), kc * LANES:(kc + 1) * LANES]
            s = lax.dot_general(kt, qa_ref[unit], nt, preferred_element_type=F32)
            parts.append(s + b_ref[heads[p][half], bidx])
        return parts

    def group(unit_tiles, state):
        res = []
        n_units = len(unit_tiles)
        ahead = [scores(n, unit_tiles[n]) for n in range(min(FLASH_LOOKAHEAD, n_units))]
        for hd in range(n_units):
            tiles = unit_tiles[hd]
            parts = ahead.pop(0)
            if hd + FLASH_LOOKAHEAD < n_units:
                ahead.append(scores(hd + FLASH_LOOKAHEAD, unit_tiles[hd + FLASH_LOOKAHEAD]))
            m_new = functools.reduce(jnp.maximum, [jnp.max(s, axis=0, keepdims=True) for s in parts])
            if state is not None:
                m, l = state[hd]
                m_new = jnp.maximum(m, m_new)
                alpha = jnp.exp2(m - m_new)
            prs = [jnp.exp2(s - m_new) for s in parts]
            l_new = functools.reduce(jnp.add, [jnp.sum(pr, axis=0, keepdims=True) for pr in prs])
            vc = (hd % n_heads // 2) if per_pair_kv else 0
            pv = None
            for (j, _), pr in zip(tiles, prs):
                vt = v_ref[pl.ds(pl.multiple_of(j * tile, tile), tile), vc * LANES:(vc + 1) * LANES]
                d = lax.dot_general(vt, pr.astype(BF16), tn, preferred_element_type=F32)
                pv = d if pv is None else pv + d
            if state is not None:
                l_new = alpha * l + l_new
                pv = alpha * acc_ref[hd] + pv
            res.append((m_new, l_new, pv))
        return res

    if n_back is not None:
        unit_tiles = []
        for u in range(q_tiles):
            iu = i * q_tiles + u
            tiles = [(jnp.maximum(iu - k, 0), jnp.where(iu >= k, k, n_bias - 1)) for k in range(n_back + 1)]
            unit_tiles += [tiles] * n_heads
        res = group(unit_tiles, None)
        for u in range(q_tiles):
            for p in range(n_pairs):
                (m0, l0, a0), (m1, l1, a1) = res[u * n_heads + 2 * p], res[u * n_heads + 2 * p + 1]
                finish(u, p, a0, l0, m0, a1, l1, m1)
        return

    def store(res):
        for hd, (_, _, acc) in enumerate(res):
            acc_ref[hd] = acc
        return tuple(x for m, l, _ in res for x in (m, l))

    def unpack(carry):
        return [(carry[2 * hd], carry[2 * hd + 1]) for hd in range(n_heads)]

    bias_of = lambda back: jnp.minimum(back, n_bias - 1)
    carry = store(group([[(i, 0)]] * n_heads, None))

    def pair_body(step, carry):
        back = 2 * step + 1
        tiles = [(i - back, bias_of(back)), (i - back - 1, bias_of(back + 1))]
        return store(group([tiles] * n_heads, unpack(carry)))

    carry = lax.fori_loop(0, i // 2, pair_body, carry)

    def last_body(_, carry):
        return store(group([[(0, bias_of(i))]] * n_heads, unpack(carry)))

    carry = lax.fori_loop(0, i % 2, last_body, carry)
    fin = unpack(carry)
    for p in range(n_pairs):
        finish(0, p, acc_ref[2 * p], fin[2 * p][1], fin[2 * p][0], acc_ref[2 * p + 1], fin[2 * p + 1][1], fin[2 * p + 1][0])


def _flash(q_arr, k_arr, v_arr, bias, *, batch, length, dil, tile, n_back, n_pairs, q_blk, k_blk, k_width,
           v_blk, v_width, per_pair_kv, heads, selb=None, with_lse=False, q_tiles=1):
    nq = length // tile
    with_sel = selb is not None
    width = n_pairs * LANES

    in_specs = [pl.BlockSpec((None, q_tiles * tile, width), lambda br, i: (br // dil, i, q_blk(br % dil))),
                pl.BlockSpec((None, length, k_width), lambda br, i: (br // dil, 0, k_blk(br % dil))),
                pl.BlockSpec((None, length, v_width), lambda br, i: (br // dil, 0, v_blk(br % dil))),
                pl.BlockSpec(bias.shape, lambda br, i: (0, 0, 0, 0))]
    args = [q_arr, k_arr, v_arr, bias]
    if with_sel:
        in_specs.append(pl.BlockSpec((None, q_tiles * tile, 2 * LANES), lambda br, i: (br, i, 0)))
        args.append(selb)
    o_map = lambda br, i: (br // dil, i, br % dil)
    out_specs = [pl.BlockSpec((None, q_tiles * tile, width), o_map)]
    out_shape = [jax.ShapeDtypeStruct((batch, length, dil * width), BF16)]
    if with_lse:
        out_specs.append(pl.BlockSpec((None, q_tiles * tile, width), o_map))
        out_shape.append(jax.ShapeDtypeStruct((batch, length, dil * width), F32))
    return pl.pallas_call(
        functools.partial(_flash_kernel, tile=tile, q_tiles=q_tiles, n_pairs=n_pairs, heads=heads, per_pair_kv=per_pair_kv,
                          n_back=n_back, n_bias=bias.shape[1], with_sel=with_sel, with_lse=with_lse),
        grid=(batch * dil, nq // q_tiles),
        in_specs=in_specs,
        out_specs=out_specs,
        out_shape=out_shape,
        scratch_shapes=[pltpu.VMEM((q_tiles * 2 * n_pairs, tile, LANES), BF16)]
        + ([pltpu.VMEM((2 * n_pairs, LANES, tile), F32)] if n_back is None else []),
        compiler_params=_cparams(("arbitrary", "arbitrary")),
        name="flash_sel" if with_sel else ("flash_dil" if with_lse else "flash_win"),
    )(*args)


def _mix_kernel(x_ref, g1_ref, oc_ref, os_ref, ow_ref, gl_ref, ge_ref,
                od0_ref, od1_ref, od2_ref, l0_ref, l1_ref, l2_ref, ml_ref,
                wn_ref, wd_ref, wo_ref, o_ref, slab_ref):
    tm, d = x_ref.shape

    def in_token_order(ref):
        if len(ref.shape) == 2:
            return ref[...].astype(F32)
        dil = ref.shape[0]
        n_slab = ref.shape[2] // LANES
        for r in range(dil):
            for c in range(n_slab):
                slab_ref[c, pl.ds(r, tm // dil, stride=dil), :] = ref[r, :, c * LANES:(c + 1) * LANES].astype(F32)
        return jnp.concatenate([slab_ref[c] for c in range(n_slab)], axis=1)

    sig = 1.0 / (1.0 + jnp.exp(-gl_ref[...]))
    hi = sig.astype(BF16)
    lo = (sig - hi.astype(F32)).astype(BF16)
    ge = ge_ref[...]
    gates = jnp.dot(hi, ge, preferred_element_type=F32) + jnp.dot(lo, ge, preferred_element_type=F32)
    o_nsa = (gates[:, 0:512] * oc_ref[...].astype(F32)
             + gates[:, 512:1024] * os_ref[...].astype(F32)
             + gates[:, 1024:1536] * ow_ref[...].astype(F32))
    u_nsa = jnp.dot(o_nsa.astype(BF16), wn_ref[...], preferred_element_type=F32)

    lses = [in_token_order(ref) for ref in (l0_ref, l1_ref, l2_ref)]
    mx = jnp.maximum(jnp.maximum(lses[0], lses[1]), lses[2])
    es = [jnp.exp2(l - mx) for l in lses]
    inv = 1.0 / (es[0] + es[1] + es[2])
    o_dil = (es[0] * in_token_order(od0_ref) + es[1] * in_token_order(od1_ref)
             + es[2] * in_token_order(od2_ref)) * inv
    u_dil = jnp.dot(o_dil.astype(BF16), wd_ref[...], preferred_element_type=F32)

    gm = 1.0 / (1.0 + jnp.exp(-ml_ref[...].astype(F32)))
    merged = gm[:, :d] * u_nsa + gm[:, d:] * u_dil
    y = jnp.dot(merged.astype(BF16), wo_ref[...], preferred_element_type=F32)
    o_ref[...] = x_ref[...] + g1_ref[...] * y


def _mix(x2, g1, oc, osel, ow, gl, ge, od, lse, ml, wn, wd, wo, seq):
    t, d = x2.shape
    tm = TM_PROJ
    per_b = seq // tm
    row = lambda i: (i, 0)
    const = lambda i: (0, 0)
    full = lambda a: pl.BlockSpec(a.shape, const)

    def rows(a):
        if a.ndim == 2:
            return pl.BlockSpec((tm, a.shape[1]), row)
        dil = a.shape[1]
        return pl.BlockSpec((None, dil, tm // dil, a.shape[3]), lambda i: (i // per_b, 0, i % per_b, 0))

    return pl.pallas_call(
        _mix_kernel,
        grid=(t // tm,),
        in_specs=[rows(x2), pl.BlockSpec((None, 1, d), lambda i: (i // per_b, 0, 0)),
                  rows(oc), rows(osel), rows(ow), rows(gl), full(ge),
                  rows(od[0]), rows(od[1]), rows(od[2]), rows(lse[0]), rows(lse[1]), rows(lse[2]), rows(ml),
                  full(wn), full(wd), full(wo)],
        out_specs=pl.BlockSpec((tm, d), row),
        out_shape=jax.ShapeDtypeStruct((t, d), F32),
        scratch_shapes=[pltpu.VMEM((2, tm, LANES), F32)],
        compiler_params=_cparams(("arbitrary",)),
        name="mix_outproj",
    )(x2, g1, oc, osel, ow, gl, ge, *od, *lse, ml, wn, wd, wo)


def _router_kernel(x_ref, sc_ref, sh_ref, g_ref, wr_ref, br_ref, h_ref, eid_ref, wts_ref):
    h = _norm_mod(x_ref[...], g_ref[...], sc_ref[...], sh_ref[...])
    n_sub = h.shape[1] // LANES
    for j in range(n_sub):
        h_ref[pl.ds(j, h.shape[0], stride=n_sub), :] = h[:, j * LANES:(j + 1) * LANES]
    logit = lax.dot_general(wr_ref[...], h, (((1,), (1,)), ((), ())), preferred_element_type=F32,
                            precision=lax.Precision.HIGHEST) + br_ref[...]
    grp = jnp.zeros((1, h.shape[0]), jnp.int32)
    best = logit[0:1]
    for k in range(1, N_EXPERT_GROUPS):
        better = logit[k:k + 1] > best
        grp = jnp.where(better, k, grp)
        best = jnp.where(better, logit[k:k + 1], best)
    den = jnp.zeros_like(best)
    for k in range(N_EXPERT_GROUPS):
        den = den + jnp.exp(logit[k:k + 1] - best)
    p_grp = 1.0 / den
    le = logit[SUBLANES:SUBLANES + EXPERTS_PER_GROUP]
    for k in range(1, N_EXPERT_GROUPS):
        lo = SUBLANES + k * EXPERTS_PER_GROUP
        le = jnp.where(grp == k, logit[lo:lo + EXPERTS_PER_GROUP], le)
    rowi = lax.broadcasted_iota(jnp.int32, le.shape, 0)
    v1 = jnp.max(le, axis=0, keepdims=True)
    i1 = jnp.min(jnp.where(le == v1, rowi, EXPERTS_PER_GROUP), axis=0, keepdims=True)
    rest = jnp.where(rowi == i1, -3e38, le)
    v2 = jnp.max(rest, axis=0, keepdims=True)
    i2 = jnp.min(jnp.where(rest == v2, rowi, EXPERTS_PER_GROUP), axis=0, keepdims=True)
    e2 = jnp.exp(v2 - v1)
    inv = p_grp / (1.0 + e2)
    eid_ref[...] = jnp.concatenate([grp * EXPERTS_PER_GROUP + i1, grp * EXPERTS_PER_GROUP + i2], axis=0)
    wts_ref[...] = jnp.concatenate([inv, e2 * inv], axis=0)


def _router(x2, sc, sh, g, wr_t, br, seq):
    t, d = x2.shape
    tm = TM_PROJ
    per_b = seq // tm
    row = lambda i: (i, 0)
    const = lambda i: (0, 0)
    return pl.pallas_call(
        _router_kernel,
        grid=(t // tm,),
        in_specs=[pl.BlockSpec((tm, d), row),
                  pl.BlockSpec((None, 1, d), lambda i: (i // per_b, 0, 0)),
                  pl.BlockSpec((None, 1, d), lambda i: (i // per_b, 0, 0)),
                  pl.BlockSpec((1, d), const),
                  pl.BlockSpec(wr_t.shape, const),
                  pl.BlockSpec(br.shape, const)],
        out_specs=[pl.BlockSpec((tm * (d // LANES), LANES), row),
                   pl.BlockSpec((2, tm), lambda i: (0, i)),
                   pl.BlockSpec((2, tm), lambda i: (0, i))],
        out_shape=[jax.ShapeDtypeStruct((t * (d // LANES), LANES), F32),
                   jax.ShapeDtypeStruct((2, t), jnp.int32),
                   jax.ShapeDtypeStruct((2, t), F32)],
        compiler_params=_cparams(("arbitrary",)),
        name="norm_router",
    )(x2, sc, sh, g, wr_t, br)


def _moe_kernel(cnt_ref, off_ref, tok_ref, wt_ref, h_ref, w1_ref, w3_ref, w2_ref, o_ref,
                xs_ref, os_ref, xb_ref, y3_ref):
    c = pl.program_id(0)
    e = pl.program_id(1)
    n_sub = xb_ref.shape[1] // LANES
    rb = xb_ref.shape[0]

    @pl.when(e == 0)
    def _():
        o_ref[...] = jnp.zeros_like(o_ref)

    n = cnt_ref[c, e]
    off = off_ref[c, e]

    def every_row(base, last, fn):
        def grp(g, _):
            slot0 = base + g * MOE_UNROLL
            row0 = pl.multiple_of(g * (MOE_UNROLL * n_sub), MOE_UNROLL * n_sub)
            for u in range(MOE_UNROLL):
                fn(jnp.minimum(slot0 + u, last), pl.ds(row0 + u * n_sub, n_sub))
            return 0

        lax.fori_loop(0, rb // MOE_UNROLL, grp, 0)

    def token_rows(slot):
        return pl.ds(pl.multiple_of(tok_ref[0, slot], n_sub), n_sub)

    def block(bi, _):
        base = off + bi * rb
        last = off + jnp.minimum(n, (bi + 1) * rb) - 1

        def gather(slot, rows):
            src = token_rows(slot)
            xs_ref[rows, :] = h_ref[src, :]
            os_ref[rows, :] = o_ref[src, :]

        every_row(base, last, gather)
        for j in range(n_sub):
            xb_ref[:, j * LANES:(j + 1) * LANES] = xs_ref[pl.ds(j, rb, stride=n_sub), :].astype(BF16)
        xb = xb_ref[...]
        a = jnp.dot(xb, w1_ref[...], preferred_element_type=F32)
        b = jnp.dot(xb, w3_ref[...], preferred_element_type=F32)
        mid = (a * (1.0 / (1.0 + jnp.exp(-a))) * b).astype(BF16)
        y = jnp.dot(mid, w2_ref[...], preferred_element_type=F32)
        for j in range(n_sub):
            y3_ref[pl.ds(j, rb, stride=n_sub), :] = y[:, j * LANES:(j + 1) * LANES]

        def scatter(slot, rows):
            o_ref[token_rows(slot), :] = os_ref[rows, :] + wt_ref[0, slot] * y3_ref[rows, :]

        every_row(base, last, scatter)
        return 0

    lax.fori_loop(0, (n + rb - 1) // rb, block, 0)


def _moe(h2, eid, wts, w1, w3, w2):
    n_sub = w1.shape[1] // LANES
    t, d = h2.shape[0] // n_sub, w1.shape[1]
    tc = min(MOE_CHUNK, t)
    n_chunks = t // tc
    slots = 2 * tc
    tok = jnp.arange(t, dtype=jnp.int32)
    key = ((tok // tc)[None, :] * N_EXPERTS + eid).reshape(-1)
    order = jnp.argsort(key)
    tok_sorted = (jnp.tile(tok % tc * n_sub, 2)[order]).reshape(n_chunks, slots)
    w_sorted = wts.reshape(-1)[order].reshape(n_chunks, slots)
    counts = jnp.zeros((n_chunks * N_EXPERTS,), jnp.int32).at[key].add(1).reshape(n_chunks, N_EXPERTS)
    starts = jnp.cumsum(counts, axis=1) - counts
    tok_sorted = tok_sorted.reshape(n_chunks, 1, slots)
    w_sorted = w_sorted.reshape(n_chunks, 1, slots)

    grid_spec = pltpu.PrefetchScalarGridSpec(
        num_scalar_prefetch=2,
        grid=(n_chunks, N_EXPERTS),
        in_specs=[pl.BlockSpec((None, 1, slots), lambda c, e, *_: (c, 0, 0), memory_space=pltpu.SMEM),
                  pl.BlockSpec((None, 1, slots), lambda c, e, *_: (c, 0, 0), memory_space=pltpu.SMEM),
                  pl.BlockSpec((tc * n_sub, LANES), lambda c, e, *_: (c, 0)),
                  pl.BlockSpec((None, d, D_EXPERT), lambda c, e, *_: (e, 0, 0)),
                  pl.BlockSpec((None, d, D_EXPERT), lambda c, e, *_: (e, 0, 0)),
                  pl.BlockSpec((None, D_EXPERT, d), lambda c, e, *_: (e, 0, 0))],
        out_specs=pl.BlockSpec((tc * n_sub, LANES), lambda c, e, *_: (c, 0)),
        scratch_shapes=[pltpu.VMEM((MOE_ROWS * n_sub, LANES), F32),
                        pltpu.VMEM((MOE_ROWS * n_sub, LANES), F32),
                        pltpu.VMEM((MOE_ROWS, d), BF16),
                        pltpu.VMEM((MOE_ROWS * n_sub, LANES), F32)],
    )
    out = pl.pallas_call(
        _moe_kernel,
        grid_spec=grid_spec,
        out_shape=jax.ShapeDtypeStruct((t * n_sub, LANES), F32),
        compiler_params=_cparams(("arbitrary", "arbitrary")),
        name="moe_experts",
    )(counts, starts, tok_sorted, w_sorted, h2, w1, w3, w2)
    return out


def _resid_kernel(x_ref, y_ref, g_ref, nf_ref, o_ref, *, final):
    tm, d = x_ref.shape
    n_sub = d // LANES
    y = jnp.concatenate([y_ref[pl.ds(j, tm, stride=n_sub), :] for j in range(n_sub)], axis=1)
    x = x_ref[...] + g_ref[...] * y
    if final:
        ms = jnp.mean(x * x, axis=-1, keepdims=True)
        x = x * lax.rsqrt(ms + RMS_EPS) * nf_ref[...]
    o_ref[...] = x


def _residual(x2, y2, g2, norm_f, seq, final):
    t, d = x2.shape
    tm = TM_PROJ
    per_b = seq // tm
    row = lambda i: (i, 0)
    return pl.pallas_call(
        functools.partial(_resid_kernel, final=final),
        grid=(t // tm,),
        in_specs=[pl.BlockSpec((tm, d), row), pl.BlockSpec((tm * (d // LANES), LANES), row),
                  pl.BlockSpec((None, 1, d), lambda i: (i // per_b, 0, 0)),
                  pl.BlockSpec((1, d), lambda i: (0, 0))],
        out_specs=pl.BlockSpec((tm, d), row),
        out_shape=jax.ShapeDtypeStruct((t, d), F32),
        compiler_params=_cparams(("arbitrary",)),
        name="residual_final" if final else "residual",
    )(x2, y2, g2, norm_f)


def _split_w_in(w_in, d):
    scale = HEAD_DIM ** -0.5 * LOG2E
    nq = NSA_HEADS * HEAD_DIM
    nkv = 3 * 2 * NSA_KV_HEADS * HEAD_DIM
    ngate = 3 * NSA_HEADS
    ndil = 3 * N_DIL_GROUPS * DIL_HEADS_PER_GROUP * HEAD_DIM
    o1, o2, o3 = nq, nq + nkv, nq + nkv + ngate
    o4 = o3 + ndil
    wq = (w_in[:, :o1] * scale).reshape(d, NSA_KV_HEADS, NSA_GROUP, HEAD_DIM)
    wq = wq.transpose(0, 2, 1, 3).reshape(d, nq)
    wkv = w_in[:, o1:o2]
    wg = jnp.pad(w_in[:, o2:o3], ((0, 0), (0, LANES - ngate)))
    gw = DIL_HEADS_PER_GROUP * HEAD_DIM
    per_which = N_DIL_GROUPS * gw
    wds = []
    for grp in range(N_DIL_GROUPS):
        parts = [w_in[:, o3 + which * per_which + grp * gw: o3 + which * per_which + (grp + 1) * gw]
                 for which in range(3)]
        parts[0] = parts[0] * scale
        wds.append(jnp.concatenate(parts, axis=1))
    wm = w_in[:, o4:]
    cast = lambda w: w.astype(BF16)
    return [cast(wq), cast(wkv), cast(wds[0]), cast(wds[1]), cast(wds[2]), cast(wm), cast(wg)]


def kernel(x, c, rel_bias, ada_w, ada_b, norm1, norm2, w_in, cmp_pos, cmp_w1, cmp_w2, w_up_nsa, w_up_dil, w_o,
           router_wg, router_bg, router_we, router_be, exp_w1, exp_w3, exp_w2, norm_f):
    batch, seq, d = x.shape
    depth = ada_w.shape[0]
    t = batch * seq
    n_cmp = seq // CMP_STRIDE
    n_sel = seq // SEL_BLOCK
    assert seq % TQ_NSA == 0 and n_sel <= HEAD_DIM and n_sel >= SEL_TOPN
    assert all(seq % (dil * TQ_DIL) == 0 for _, dil in DIL_PAIRS)

    mod = _modulation(c, ada_w, ada_b)

    nq_nsa = seq // TQ_NSA
    bias_cmp = _expand_bias(rel_bias, _cmp_buckets(seq, n_cmp), 0, NSA_HEADS, 8 * SUBLANES)
    sel_b = _toeplitz_buckets(nq_nsa, TQ_NSA, seq, 1)
    n_sel_bias = nq_nsa
    while n_sel_bias > 1 and (sel_b[n_sel_bias - 2:] == sel_b[n_sel_bias - 1, 0, 0]).all():
        n_sel_bias -= 1
    sel_b = sel_b[:n_sel_bias].reshape(n_sel_bias * TQ_NSA, TQ_NSA)
    bias_sel = _expand_bias(rel_bias, sel_b, 0, NSA_HEADS, TQ_NSA).reshape(NSA_HEADS, n_sel_bias, TQ_NSA, TQ_NSA)
    nb_win = min(-(-NSA_WINDOW // TQ_NSA), nq_nsa - 1)
    win_b = _toeplitz_buckets(nb_win + 1, TQ_NSA, NSA_WINDOW, 1, masked_tail=True).reshape(-1, TQ_NSA)
    bias_win = _expand_bias(rel_bias, win_b, 0, NSA_HEADS, TQ_NSA).reshape(NSA_HEADS, nb_win + 2, TQ_NSA, TQ_NSA)
    bias_dil, nb_dil = [], []
    for grp, (window, dil) in enumerate(DIL_PAIRS):
        nb = min(-(-(window // dil) // TQ_DIL), seq // dil // TQ_DIL - 1)
        bk = _toeplitz_buckets(nb + 1, TQ_DIL, window // dil, dil, masked_tail=True).reshape(-1, TQ_DIL)
        hb = NSA_HEADS + grp * DIL_HEADS_PER_GROUP
        bias_dil.append(_expand_bias(rel_bias, bk, hb, DIL_HEADS_PER_GROUP, TQ_DIL)
                        .reshape(DIL_HEADS_PER_GROUP, nb + 2, TQ_DIL, TQ_DIL))
        nb_dil.append(nb)
    ovl_t = jnp.asarray(_overlap_t(n_cmp, n_sel), BF16)
    onehot = jnp.asarray(_block_onehot(seq), BF16)
    gate_e = jnp.asarray(_gate_expand(), BF16)

    x2 = x.reshape(t, d)
    for l in range(depth):
        sh1, sc1, g1, sh2, sc2, g2 = [m.reshape(batch, 1, d) for m in jnp.split(mod[l], 6, axis=-1)]
        weights = _split_w_in(w_in[l], d)
        (q_n, kc_in, vc_in, k_sel, v_sel, k_win, v_win, qkv_d0, qkv_d1, qkv_d2, merge_l, gate_l) = _inproj(
            x2, sc1, sh1, norm1[l].reshape(1, d), onehot, weights, seq)

        kc = _compress(kc_in, cmp_pos[l, 0], cmp_w1[l, 0], cmp_w2[l, 0], batch, seq, transpose_out=False)
        vct = _compress(vc_in, cmp_pos[l, 1], cmp_w1[l, 1], cmp_w2[l, 1], batch, seq, transpose_out=True)
        o_c, selb = _cmp_select(q_n, kc, vct, bias_cmp, ovl_t, batch, seq)
        nsa_common = dict(batch=batch, length=seq, dil=1, tile=TQ_NSA, n_pairs=NSA_GROUP, q_blk=lambda r: 0,
                          k_blk=lambda r: 0, v_blk=lambda r: 0, v_width=LANES, per_pair_kv=False,
                          heads=tuple((g, NSA_GROUP + g) for g in range(NSA_GROUP)))
        q3 = q_n.reshape(batch, seq, 512)
        (o_s,) = _flash(q3, k_sel.reshape(batch, seq, 2 * LANES), v_sel.reshape(batch, seq, LANES), bias_sel,
                        n_back=None, k_width=2 * LANES, selb=selb.reshape(batch, seq, 2 * LANES), **nsa_common)
        (o_w,) = _flash(q3, k_win.reshape(batch, seq, LANES), v_win.reshape(batch, seq, LANES), bias_win,
                        n_back=nb_win, k_width=LANES, **nsa_common)

        o_d, lse_d = [], []
        for grp, ((window, dil), qkv) in enumerate(zip(DIL_PAIRS, (qkv_d0, qkv_d1, qkv_d2))):
            length = seq // dil
            view = qkv.reshape(batch * dil, length, 768)
            o_g, lse_g = _flash(view, view, view, bias_dil[grp], batch=batch * dil, length=length, dil=1,
                                tile=TQ_DIL, n_back=nb_dil[grp], n_pairs=2, q_blk=lambda r: 0, k_blk=lambda r: 1,
                                k_width=2 * LANES, v_blk=lambda r: 2, v_width=2 * LANES,
                                per_pair_kv=True, heads=((0, 1), (2, 3)), with_lse=True,
                                q_tiles=math.gcd(DIL_Q_TILES, length // TQ_DIL))
            shape = (t, 256) if dil == 1 else (batch, dil, length, 256)
            o_d.append(o_g.reshape(shape))
            lse_d.append(lse_g.reshape(shape))

        wn = w_up_nsa[l].reshape(NSA_KV_HEADS, NSA_GROUP, HEAD_DIM, d).transpose(1, 0, 2, 3).reshape(512, d)
        x2 = _mix(x2, g1, o_c, o_s.reshape(t, 512), o_w.reshape(t, 512), gate_l, gate_e, o_d, lse_d, merge_l,
                  wn.astype(BF16), w_up_dil[l].astype(BF16), w_o[l].astype(BF16), seq)

        wr_t = jnp.concatenate([jnp.pad(router_wg[l], ((0, 0), (0, SUBLANES - N_EXPERT_GROUPS))),
                                router_we[l]], axis=1).T
        br = jnp.concatenate([jnp.pad(router_bg[l], (0, SUBLANES - N_EXPERT_GROUPS)),
                              router_be[l]]).reshape(-1, 1)
        h2, eid, wts = _router(x2, sc2, sh2, norm2[l].reshape(1, d), wr_t, br, seq)
        y = _moe(h2, eid, wts, exp_w1[l].astype(BF16), exp_w3[l].astype(BF16), exp_w2[l].astype(BF16))
        x2 = _residual(x2, y, g2, norm_f.reshape(1, d), seq, final=(l == depth - 1))
    return x2.reshape(batch, seq, d)
```

```python
import functools
import math

import numpy as np
import jax
import jax.numpy as jnp
from jax import lax
from jax.experimental import pallas as pl
from jax.experimental.pallas import tpu as pltpu

F32 = jnp.float32
BF16 = jnp.bfloat16

HEAD_DIM = 64
NSA_HEADS = 8
NSA_KV_HEADS = 2
NSA_GROUP = NSA_HEADS // NSA_KV_HEADS
CMP_LEN = 32
CMP_STRIDE = 16
CMP_HIDDEN = 256
SEL_BLOCK = 64
SEL_TOPN = 16
NSA_WINDOW = 512
FORCE_SCORE = 1e4
DIL_PAIRS = ((128, 1), (512, 4), (2048, 16))
N_DIL_GROUPS = 3
DIL_HEADS_PER_GROUP = 4
NUM_BUCKETS = 32
REL_MAX_DIST = 2048
N_EXPERT_GROUPS = 4
EXPERTS_PER_GROUP = 8
N_EXPERTS = N_EXPERT_GROUPS * EXPERTS_PER_GROUP
D_EXPERT = 512
RMS_EPS = 1e-6

LOG2E = math.log2(math.e)
LANES = 128
SUBLANES = 8
MASK_NEG = -1e30
SEL_NEG = -1e9
TQ_NSA = 256
TQ_DIL = 128
TM_PROJ = 512
WIN_Q_TILES = 2
DIL_Q_TILES = 4
FLASH_LOOKAHEAD = 4
MOE_CHUNK = 4096
MOE_ROWS = 128
MOE_UNROLL = 16
VMEM_LIMIT = 56 * 1024 * 1024


def _cparams(sem):
    return pltpu.CompilerParams(dimension_semantics=sem, vmem_limit_bytes=VMEM_LIMIT)


def _np_bucket(dist):
    dist = np.maximum(dist, 0)
    max_exact = NUM_BUCKETS // 2
    df = np.maximum(dist, 1).astype(np.float32)
    val = np.log(df / np.float32(max_exact)) / np.float32(math.log(REL_MAX_DIST / max_exact))
    large = max_exact + (val * np.float32(NUM_BUCKETS - max_exact)).astype(np.int32)
    large = np.minimum(large, NUM_BUCKETS - 1)
    return np.where(dist < max_exact, dist, large).astype(np.int32)


def _toeplitz_buckets(n_delta, tile, window, dist_scale, masked_tail=False):
    dd = np.arange(n_delta)[:, None, None]
    r = np.arange(tile)[None, :, None]
    c = np.arange(tile)[None, None, :]
    dist = dd * tile + c - r
    valid = (dist >= 0) & (dist <= window)
    out = np.where(valid, _np_bucket(dist * dist_scale), -1).astype(np.int32)
    if masked_tail:
        out = np.concatenate([out, np.full((1, tile, tile), -1, np.int32)], axis=0)
    return out


def _cmp_buckets(seq, n_cmp):
    c_end = np.arange(n_cmp)[:, None] * CMP_STRIDE + CMP_LEN - 1
    t = np.arange(seq)[None, :]
    dist = t - c_end
    return np.where(dist >= 0, _np_bucket(dist), -1).astype(np.int32)


def _overlap_t(n_cmp, n_sel):
    c_start = np.arange(n_cmp)[None, :] * CMP_STRIDE
    s_start = np.arange(n_sel)[:, None] * SEL_BLOCK
    ov = np.clip(np.minimum(c_start + CMP_LEN, s_start + SEL_BLOCK) - np.maximum(c_start, s_start), 0, None)
    out = np.zeros((LANES, n_cmp), np.float32)
    out[64:64 + n_sel] = ov.astype(np.float32) / CMP_STRIDE
    return out


def _block_onehot(seq):
    oh = np.zeros((seq, LANES), np.float32)
    blk = np.arange(seq) // SEL_BLOCK
    oh[np.arange(seq), blk] = 1.0
    oh[np.arange(seq), 64 + blk] = 1.0
    return oh


def _gate_expand():
    e = np.zeros((LANES, 3 * NSA_HEADS * HEAD_DIM), np.float32)
    for br in range(3):
        for g in range(NSA_GROUP):
            for ln in range(LANES):
                kv = ln // HEAD_DIM
                e[br * NSA_HEADS + kv * NSA_GROUP + g, br * 512 + g * LANES + ln] = 1.0
    return e


def _mod_kernel(c_ref, w_ref, b_ref, o_ref):
    c = c_ref[...]
    cond = c * (1.0 / (1.0 + jnp.exp(-c)))
    o_ref[...] = jnp.dot(cond, w_ref[...], preferred_element_type=F32,
                         precision=lax.Precision.HIGHEST) + b_ref[...]


def _modulation(c, ada_w, ada_b):
    depth, d, n = ada_w.shape
    b = c.shape[0]
    tn = 1536
    return pl.pallas_call(
        _mod_kernel,
        grid=(depth, n // tn),
        in_specs=[pl.BlockSpec((b, d), lambda l, j: (0, 0)),
                  pl.BlockSpec((None, d, tn), lambda l, j: (l, 0, j)),
                  pl.BlockSpec((None, 1, tn), lambda l, j: (l, 0, j))],
        out_specs=pl.BlockSpec((None, b, tn), lambda l, j: (l, 0, j)),
        out_shape=jax.ShapeDtypeStruct((depth, b, n), F32),
        compiler_params=_cparams(("arbitrary", "arbitrary")),
        name="adaln_modulation",
    )(c, ada_w, ada_b.reshape(depth, 1, n))


def _bias_kernel(tbl_ref, bkt_ref, o_ref, *, head_base):
    h = pl.program_id(0) + head_base
    bkt = bkt_ref[...]
    acc = jnp.full(bkt.shape, MASK_NEG, F32)
    for b in range(NUM_BUCKETS):
        acc = jnp.where(bkt == b, tbl_ref[b, h] * LOG2E, acc)
    o_ref[...] = acc


def _expand_bias(rel_bias, buckets, head_base, n_heads, row_tile):
    rows, cols = buckets.shape
    return pl.pallas_call(
        functools.partial(_bias_kernel, head_base=head_base),
        grid=(n_heads, rows // row_tile),
        in_specs=[pl.BlockSpec(memory_space=pltpu.SMEM),
                  pl.BlockSpec((row_tile, cols), lambda h, i: (i, 0))],
        out_specs=pl.BlockSpec((None, row_tile, cols), lambda h, i: (h, i, 0)),
        out_shape=jax.ShapeDtypeStruct((n_heads, rows, cols), F32),
        compiler_params=_cparams(("arbitrary", "arbitrary")),
        name="bias_expand",
    )(rel_bias, jnp.asarray(buckets))


def _norm_mod(x, g, sc, sh):
    ms = jnp.mean(x * x, axis=-1, keepdims=True)
    y = x * lax.rsqrt(ms + RMS_EPS) * g
    return y * (1.0 + sc) + sh


def _inproj_kernel(x_ref, sc_ref, sh_ref, g_ref, oh_ref, wq_ref, wkv_ref, wd0_ref, wd1_ref, wd2_ref, wm_ref, wg_ref,
                   q_ref, kc_ref, vc_ref, ks_ref, vs_ref, kw_ref, vw_ref, d0_ref, d1_ref, d2_ref, m_ref, gl_ref,
                   slab_ref):
    h = _norm_mod(x_ref[...], g_ref[...], sc_ref[...], sh_ref[...]).astype(BF16)
    tm = h.shape[0]

    def by_residue(w_ref, o_ref):
        dil = o_ref.shape[0]
        res = jnp.dot(h, w_ref[...], preferred_element_type=F32)
        n_slab = res.shape[1] // LANES
        for c in range(n_slab):
            slab_ref[c] = res[:, c * LANES:(c + 1) * LANES]
        for r in range(dil):
            for c in range(n_slab):
                o_ref[r, :, c * LANES:(c + 1) * LANES] = slab_ref[c, pl.ds(r, tm // dil, stride=dil), :].astype(BF16)

    def proj(w_ref):
        return jnp.dot(h, w_ref[...], preferred_element_type=F32)

    q_ref[...] = proj(wq_ref).astype(BF16)
    kv = proj(wkv_ref).astype(BF16)
    for k, ref in ((0, kc_ref), (1, vc_ref), (3, vs_ref), (4, kw_ref), (5, vw_ref)):
        ref[...] = kv[:, k * LANES:(k + 1) * LANES]
    k_sel = kv[:, 2 * LANES:3 * LANES]
    oh = oh_ref[...]
    lo = lax.broadcasted_iota(jnp.int32, k_sel.shape, 1) < HEAD_DIM
    ks_ref[:, :LANES] = jnp.where(lo, k_sel, oh)
    ks_ref[:, LANES:] = jnp.where(lo, oh, k_sel)
    d0_ref[...] = proj(wd0_ref).astype(BF16)
    by_residue(wd1_ref, d1_ref)
    by_residue(wd2_ref, d2_ref)
    m_ref[...] = proj(wm_ref).astype(BF16)
    gl_ref[...] = proj(wg_ref)


def _inproj(x2, sc, sh, g, onehot, weights, seq):
    t, d = x2.shape
    tm = TM_PROJ
    per_b = seq // tm
    widths = (512, 128, 128, 256, 128, 128, 128, 768, 768, 768, 2 * d, LANES)
    dtypes = (BF16,) * 11 + (F32,)
    row = lambda i: (i, 0)
    const = lambda i: (0, 0)
    in_specs = [pl.BlockSpec((tm, d), row),
                pl.BlockSpec((None, 1, d), lambda i: (i // per_b, 0, 0)),
                pl.BlockSpec((None, 1, d), lambda i: (i // per_b, 0, 0)),
                pl.BlockSpec((1, d), const),
                pl.BlockSpec((tm, LANES), lambda i: (i % per_b, 0))]
    in_specs += [pl.BlockSpec(w.shape, const) for w in weights]
    out_specs = [pl.BlockSpec((tm, w), row) for w in widths]
    out_shape = [jax.ShapeDtypeStruct((t, w), dt) for w, dt in zip(widths, dtypes)]
    for k, (_, dil) in zip((8, 9), DIL_PAIRS[1:]):
        out_specs[k] = pl.BlockSpec((None, dil, tm // dil, 768), lambda i: (i // per_b, 0, i % per_b, 0))
        out_shape[k] = jax.ShapeDtypeStruct((t // seq, dil, seq // dil, 768), BF16)
    return pl.pallas_call(
        _inproj_kernel,
        grid=(t // tm,),
        in_specs=in_specs,
        out_specs=out_specs,
        out_shape=out_shape,
        scratch_shapes=[pltpu.VMEM((768 // LANES, tm, LANES), F32)],
        compiler_params=_cparams(("arbitrary",)),
        name="norm_inproj",
    )(x2, sc, sh, g, onehot, *weights)


def _gelu_tanh(x):
    return 0.5 * x * (1.0 + jnp.tanh(math.sqrt(2.0 / math.pi) * (x + 0.044715 * (x * x * x))))


def _compress_kernel(r_ref, pt_ref, pb_ref, wt_ref, wb_ref, w2_ref, o_ref, *, transpose_out):
    r = r_ref[...].astype(F32)
    top = jnp.dot((r + pt_ref[...]).astype(BF16), wt_ref[...], preferred_element_type=F32)
    bot = jnp.dot((r + pb_ref[...]).astype(BF16), wb_ref[...], preferred_element_type=F32)
    n = bot.shape[0]
    hid = top + pltpu.roll(bot, n - 1, 0)
    act = _gelu_tanh(hid).astype(BF16)
    if transpose_out:
        o_ref[...] = lax.dot_general(w2_ref[...], act, (((1,), (1,)), ((), ())),
                                     preferred_element_type=F32).astype(BF16)
    else:
        o_ref[...] = jnp.dot(act, w2_ref[...], preferred_element_type=F32).astype(BF16)


def _compress(tok, pos, w1, w2, batch, seq, transpose_out):
    nc = seq // CMP_STRIDE
    half = CMP_LEN // 2
    eye = jnp.eye(NSA_KV_HEADS, dtype=F32)
    w1r = w1.reshape(CMP_LEN, HEAD_DIM, CMP_HIDDEN)
    blk = lambda w: jnp.einsum('ldn,hg->lhdgn', w, eye).reshape(half * LANES, 2 * CMP_HIDDEN).astype(BF16)
    wt, wb = blk(w1r[:half]), blk(w1r[half:])
    posr = lambda p: jnp.broadcast_to(p[:, None, :], (half, NSA_KV_HEADS, HEAD_DIM)).reshape(1, half * LANES)
    pt, pb = posr(pos[:half]), posr(pos[half:])
    w2b = jnp.einsum('nd,hg->hngd', w2, eye).reshape(2 * CMP_HIDDEN, LANES)
    if transpose_out:
        w2b = w2b.T
        out_block, out_shape = (None, LANES, nc), (batch, LANES, nc)
    else:
        out_block, out_shape = (None, nc, LANES), (batch, nc, LANES)
    w2b = w2b.astype(BF16)
    const = lambda b: (0, 0)
    return pl.pallas_call(
        functools.partial(_compress_kernel, transpose_out=transpose_out),
        grid=(batch,),
        in_specs=[pl.BlockSpec((None, nc, half * LANES), lambda b: (b, 0, 0)),
                  pl.BlockSpec(pt.shape, const), pl.BlockSpec(pb.shape, const),
                  pl.BlockSpec(wt.shape, const), pl.BlockSpec(wb.shape, const),
                  pl.BlockSpec(w2b.shape, const)],
        out_specs=pl.BlockSpec(out_block, lambda b: (b, 0, 0)),
        out_shape=jax.ShapeDtypeStruct(out_shape, BF16),
        compiler_params=_cparams(("arbitrary",)),
        name="nsa_compress",
    )(tok.reshape(batch, nc, half * LANES), pt, pb, wt, wb, w2b)


def _cmp_select_kernel(q_ref, kc_ref, vct_ref, bias_ref, ovl_ref, o_ref, selb_ref, *, n_sel, n_top):
    tq = q_ref.shape[0]
    nc = kc_ref.shape[0]
    qs = pl.program_id(1) * tq
    kc = kc_ref[...]
    vct = vct_ref[...]
    lane_q = lax.broadcasted_iota(jnp.int32, (tq, LANES), 1)
    row_o = lax.broadcasted_iota(jnp.int32, (LANES, tq), 0)
    heads = [(g, kv) for g in range(NSA_GROUP) for kv in range(NSA_KV_HEADS)]

    def scores(g, kv):
        qt = q_ref[:, g * LANES:(g + 1) * LANES]
        mine = (lane_q < HEAD_DIM) if kv == 0 else (lane_q >= HEAD_DIM)
        qm = jnp.where(mine, qt, jnp.zeros_like(qt))
        s = lax.dot_general(kc, qm, (((1,), (1,)), ((), ())), preferred_element_type=F32)
        return s + bias_ref[kv * NSA_GROUP + g]

    pc_sum = [jnp.zeros((nc, tq), F32) for _ in range(NSA_KV_HEADS)]
    outs = {}
    ahead = [scores(*hd) for hd in heads[:FLASH_LOOKAHEAD]]
    for n, (g, kv) in enumerate(heads):
        s = ahead.pop(0)
        if n + FLASH_LOOKAHEAD < len(heads):
            ahead.append(scores(*heads[n + FLASH_LOOKAHEAD]))
        m = jnp.max(s, axis=0, keepdims=True)
        m = jnp.where(m < 0.5 * MASK_NEG, 0.0, m)
        p = jnp.exp2(s - m)
        den = jnp.sum(p, axis=0, keepdims=True)
        pc = p * (1.0 / jnp.where(den > 0.0, den, 1.0))
        pc_sum[kv] = pc_sum[kv] + pc
        outs[kv] = jnp.dot(vct, pc.astype(BF16), preferred_element_type=F32)
        if kv == NSA_KV_HEADS - 1:
            o_t = jnp.where(row_o < HEAD_DIM, outs[0], outs[1])
            o_ref[:, g * LANES:(g + 1) * LANES] = o_t.T.astype(BF16)

    ovl = ovl_ref[...]
    imps = []
    for kv in range(NSA_KV_HEADS):
        hi = pc_sum[kv].astype(BF16)
        lo = (pc_sum[kv] - hi.astype(F32)).astype(BF16)
        imps.append((jnp.dot(ovl, hi, preferred_element_type=F32)
                     + jnp.dot(ovl, lo, preferred_element_type=F32))[HEAD_DIM:])
    imp = jnp.concatenate(imps, axis=1)
    wide = (HEAD_DIM, NSA_KV_HEADS * tq)
    rowj = lax.broadcasted_iota(jnp.int32, wide, 0)
    col = lax.broadcasted_iota(jnp.int32, wide, 1)
    t = qs + jnp.where(col >= tq, col - tq, col)
    jq = jnp.right_shift(t, SEL_BLOCK.bit_length() - 1)
    forced = (rowj == 0) | (rowj == jq) | (rowj == jq - 1)
    score = jnp.where(forced, FORCE_SCORE, jnp.where(rowj > jq, -1.0, imp))
    rem = jnp.where(rowj < n_sel, score, -3e38)
    sel = jnp.zeros(wide, F32)
    for _ in range(n_top):
        m = jnp.max(rem, axis=0, keepdims=True)
        idx = jnp.min(jnp.where(rem == m, rowj, HEAD_DIM), axis=0, keepdims=True)
        pick = rowj == idx
        sel = jnp.where(pick, 1.0, sel)
        rem = jnp.where(pick, -3e38, rem)
    sb = jnp.where(sel > 0.5, 0.0, SEL_NEG)
    zero = jnp.zeros((HEAD_DIM, tq), F32)
    selb_ref[:, :LANES] = jnp.concatenate([zero, sb[:, :tq]], axis=0).T.astype(BF16)
    selb_ref[:, LANES:] = jnp.concatenate([sb[:, tq:], zero], axis=0).T.astype(BF16)


def _cmp_select(q, kc, vct, bias_c, ovl_t, batch, seq):
    t = q.shape[0]
    tq = TQ_NSA
    nq = seq // tq
    nc = seq // CMP_STRIDE
    n_sel = seq // SEL_BLOCK
    n_top = min(SEL_TOPN, n_sel)
    return pl.pallas_call(
        functools.partial(_cmp_select_kernel, n_sel=n_sel, n_top=n_top),
        grid=(batch, nq),
        in_specs=[pl.BlockSpec((tq, 512), lambda b, i: (b * nq + i, 0)),
                  pl.BlockSpec((None, nc, LANES), lambda b, i: (b, 0, 0)),
                  pl.BlockSpec((None, LANES, nc), lambda b, i: (b, 0, 0)),
                  pl.BlockSpec((NSA_HEADS, nc, tq), lambda b, i: (0, 0, i)),
                  pl.BlockSpec((LANES, nc), lambda b, i: (0, 0))],
        out_specs=[pl.BlockSpec((tq, 512), lambda b, i: (b * nq + i, 0)),
                   pl.BlockSpec((tq, 2 * LANES), lambda b, i: (b * nq + i, 0))],
        out_shape=[jax.ShapeDtypeStruct((t, 512), BF16),
                   jax.ShapeDtypeStruct((t, 2 * LANES), BF16)],
        compiler_params=_cparams(("arbitrary", "arbitrary")),
        name="nsa_cmp_select",
    )(q, kc, vct, bias_c, ovl_t)


def _flash_kernel(*refs, tile, q_tiles, n_pairs, heads, per_pair_kv, n_back, n_bias, with_sel, with_lse):
    it = iter(refs)
    q_ref, k_ref, v_ref, b_ref = (next(it) for _ in range(4))
    selb_ref = next(it) if with_sel else None
    o_ref = next(it)
    lse_ref = next(it) if with_lse else None
    qa_ref = next(it)
    acc_ref = next(it) if n_back is None else None

    i = pl.program_id(1)
    n_heads = 2 * n_pairs
    lo = lax.broadcasted_iota(jnp.int32, (tile, LANES), 1) < HEAD_DIM
    for u in range(q_tiles):
        rows = slice(u * tile, (u + 1) * tile)
        for p in range(n_pairs):
            q = q_ref[rows, p * LANES:(p + 1) * LANES]
            if with_sel:
                qa_ref[u * n_heads + 2 * p] = jnp.where(lo, q, selb_ref[rows, :LANES])
                qa_ref[u * n_heads + 2 * p + 1] = jnp.where(lo, selb_ref[rows, LANES:], q)
            else:
                zero = jnp.zeros_like(q)
                qa_ref[u * n_heads + 2 * p] = jnp.where(lo, q, zero)
                qa_ref[u * n_heads + 2 * p + 1] = jnp.where(lo, zero, q)
    nt = (((1,), (1,)), ((), ()))
    tn = (((0,), (0,)), ((), ()))
    top = lax.broadcasted_iota(jnp.int32, (LANES, tile), 0) < HEAD_DIM

    def finish(u, p, acc0, l0, m0, acc1, l1, m1):
        rows = slice(u * tile, (u + 1) * tile)
        o_t = jnp.where(top, acc0 * (1.0 / l0), acc1 * (1.0 / l1))
        o_ref[rows, p * LANES:(p + 1) * LANES] = o_t.T.astype(o_ref.dtype)
        if with_lse:
            lse_t = jnp.where(top, m0 + jnp.log2(l0), m1 + jnp.log2(l1))
            lse_ref[rows, p * LANES:(p + 1) * LANES] = lse_t.T

    def scores(unit, tiles):
        hd = unit % n_heads
        p, half = divmod(hd, 2)
        kc = half if with_sel else (p if per_pair_kv else 0)
        parts = []
        for j, bidx in tiles:
            kt = k_ref[pl.ds(pl.multiple_of(j * tile, tile), tile), kc * LANES:(kc + 1) * LANES]
            s = lax.dot_general(kt, qa_ref[unit], nt, preferred_element_type=F32)
            parts.append(s + b_ref[heads[p][half], bidx])
        return parts

    def group(unit_tiles, state):
        res = []
        n_units = len(unit_tiles)
        ahead = [scores(n, unit_tiles[n]) for n in range(min(FLASH_LOOKAHEAD, n_units))]
        for hd in range(n_units):
            tiles = unit_tiles[hd]
            parts = ahead.pop(0)
            if hd + FLASH_LOOKAHEAD < n_units:
                ahead.append(scores(hd + FLASH_LOOKAHEAD, unit_tiles[hd + FLASH_LOOKAHEAD]))
            m_new = functools.reduce(jnp.maximum, [jnp.max(s, axis=0, keepdims=True) for s in parts])
            if state is not None:
                m, l = state[hd]
                m_new = jnp.maximum(m, m_new)
                alpha = jnp.exp2(m - m_new)
            prs = [jnp.exp2(s - m_new) for s in parts]
            l_new = functools.reduce(jnp.add, [jnp.sum(pr, axis=0, keepdims=True) for pr in prs])
            vc = (hd % n_heads // 2) if per_pair_kv else 0
            pv = None
            for (j, _), pr in zip(tiles, prs):
                vt = v_ref[pl.ds(pl.multiple_of(j * tile, tile), tile), vc * LANES:(vc + 1) * LANES]
                d = lax.dot_general(vt, pr.astype(BF16), tn, preferred_element_type=F32)
                pv = d if pv is None else pv + d
            if state is not None:
                l_new = alpha * l + l_new
                pv = alpha * acc_ref[hd] + pv
            res.append((m_new, l_new, pv))
        return res

    if n_back is not None:
        unit_tiles = []
        for u in range(q_tiles):
            iu = i * q_tiles + u
            tiles = [(jnp.maximum(iu - k, 0), jnp.where(iu >= k, k, n_bias - 1)) for k in range(n_back + 1)]
            unit_tiles += [tiles] * n_heads
        res = group(unit_tiles, None)
        for u in range(q_tiles):
            for p in range(n_pairs):
                (m0, l0, a0), (m1, l1, a1) = res[u * n_heads + 2 * p], res[u * n_heads + 2 * p + 1]
                finish(u, p, a0, l0, m0, a1, l1, m1)
        return

    def store(res):
        for hd, (_, _, acc) in enumerate(res):
            acc_ref[hd] = acc
        return tuple(x for m, l, _ in res for x in (m, l))

    def unpack(carry):
        return [(carry[2 * hd], carry[2 * hd + 1]) for hd in range(n_heads)]

    bias_of = lambda back: jnp.minimum(back, n_bias - 1)
    carry = store(group([[(i, 0)]] * n_heads, None))

    def pair_body(step, carry):
        back = 2 * step + 1
        tiles = [(i - back, bias_of(back)), (i - back - 1, bias_of(back + 1))]
        return store(group([tiles] * n_heads, unpack(carry)))

    carry = lax.fori_loop(0, i // 2, pair_body, carry)

    def last_body(_, carry):
        return store(group([[(0, bias_of(i))]] * n_heads, unpack(carry)))

    carry = lax.fori_loop(0, i % 2, last_body, carry)
    fin = unpack(carry)
    for p in range(n_pairs):
        finish(0, p, acc_ref[2 * p], fin[2 * p][1], fin[2 * p][0], acc_ref[2 * p + 1], fin[2 * p + 1][1], fin[2 * p + 1][0])


def _flash(q_arr, k_arr, v_arr, bias, *, batch, length, dil, tile, n_back, n_pairs, q_blk, k_blk, k_width,
           v_blk, v_width, per_pair_kv, heads, selb=None, with_lse=False, q_tiles=1):
    nq = length // tile
    with_sel = selb is not None
    width = n_pairs * LANES

    in_specs = [pl.BlockSpec((None, q_tiles * tile, width), lambda br, i: (br // dil, i, q_blk(br % dil))),
                pl.BlockSpec((None, length, k_width), lambda br, i: (br // dil, 0, k_blk(br % dil))),
                pl.BlockSpec((None, length, v_width), lambda br, i: (br // dil, 0, v_blk(br % dil))),
                pl.BlockSpec(bias.shape, lambda br, i: (0, 0, 0, 0))]
    args = [q_arr, k_arr, v_arr, bias]
    if with_sel:
        in_specs.append(pl.BlockSpec((None, q_tiles * tile, 2 * LANES), lambda br, i: (br, i, 0)))
        args.append(selb)
    o_map = lambda br, i: (br // dil, i, br % dil)
    out_specs = [pl.BlockSpec((None, q_tiles * tile, width), o_map)]
    out_shape = [jax.ShapeDtypeStruct((batch, length, dil * width), BF16)]
    if with_lse:
        out_specs.append(pl.BlockSpec((None, q_tiles * tile, width), o_map))
        out_shape.append(jax.ShapeDtypeStruct((batch, length, dil * width), F32))
    return pl.pallas_call(
        functools.partial(_flash_kernel, tile=tile, q_tiles=q_tiles, n_pairs=n_pairs, heads=heads, per_pair_kv=per_pair_kv,
                          n_back=n_back, n_bias=bias.shape[1], with_sel=with_sel, with_lse=with_lse),
        grid=(batch * dil, nq // q_tiles),
        in_specs=in_specs,
        out_specs=out_specs,
        out_shape=out_shape,
        scratch_shapes=[pltpu.VMEM((q_tiles * 2 * n_pairs, tile, LANES), BF16)]
        + ([pltpu.VMEM((2 * n_pairs, LANES, tile), F32)] if n_back is None else []),
        compiler_params=_cparams(("arbitrary", "arbitrary")),
        name="flash_sel" if with_sel else ("flash_dil" if with_lse else "flash_win"),
    )(*args)


def _mix_kernel(x_ref, g1_ref, oc_ref, os_ref, ow_ref, gl_ref, ge_ref,
                od0_ref, od1_ref, od2_ref, l0_ref, l1_ref, l2_ref, ml_ref,
                wn_ref, wd_ref, wo_ref, o_ref, slab_ref):
    tm, d = x_ref.shape

    def in_token_order(ref):
        if len(ref.shape) == 2:
            return ref[...].astype(F32)
        dil = ref.shape[0]
        n_slab = ref.shape[2] // LANES
        for r in range(dil):
            for c in range(n_slab):
                slab_ref[c, pl.ds(r, tm // dil, stride=dil), :] = ref[r, :, c * LANES:(c + 1) * LANES].astype(F32)
        return jnp.concatenate([slab_ref[c] for c in range(n_slab)], axis=1)

    sig = 1.0 / (1.0 + jnp.exp(-gl_ref[...]))
    hi = sig.astype(BF16)
    lo = (sig - hi.astype(F32)).astype(BF16)
    ge = ge_ref[...]
    gates = jnp.dot(hi, ge, preferred_element_type=F32) + jnp.dot(lo, ge, preferred_element_type=F32)
    o_nsa = (gates[:, 0:512] * oc_ref[...].astype(F32)
             + gates[:, 512:1024] * os_ref[...].astype(F32)
             + gates[:, 1024:1536] * ow_ref[...].astype(F32))
    u_nsa = jnp.dot(o_nsa.astype(BF16), wn_ref[...], preferred_element_type=F32)

    lses = [in_token_order(ref) for ref in (l0_ref, l1_ref, l2_ref)]
    mx = jnp.maximum(jnp.maximum(lses[0], lses[1]), lses[2])
    es = [jnp.exp2(l - mx) for l in lses]
    inv = 1.0 / (es[0] + es[1] + es[2])
    o_dil = (es[0] * in_token_order(od0_ref) + es[1] * in_token_order(od1_ref)
             + es[2] * in_token_order(od2_ref)) * inv
    u_dil = jnp.dot(o_dil.astype(BF16), wd_ref[...], preferred_element_type=F32)

    gm = 1.0 / (1.0 + jnp.exp(-ml_ref[...].astype(F32)))
    merged = gm[:, :d] * u_nsa + gm[:, d:] * u_dil
    y = jnp.dot(merged.astype(BF16), wo_ref[...], preferred_element_type=F32)
    o_ref[...] = x_ref[...] + g1_ref[...] * y


def _mix(x2, g1, oc, osel, ow, gl, ge, od, lse, ml, wn, wd, wo, seq):
    t, d = x2.shape
    tm = TM_PROJ
    per_b = seq // tm
    row = lambda i: (i, 0)
    const = lambda i: (0, 0)
    full = lambda a: pl.BlockSpec(a.shape, const)

    def rows(a):
        if a.ndim == 2:
            return pl.BlockSpec((tm, a.shape[1]), row)
        dil = a.shape[1]
        return pl.BlockSpec((None, dil, tm // dil, a.shape[3]), lambda i: (i // per_b, 0, i % per_b, 0))

    return pl.pallas_call(
        _mix_kernel,
        grid=(t // tm,),
        in_specs=[rows(x2), pl.BlockSpec((None, 1, d), lambda i: (i // per_b, 0, 0)),
                  rows(oc), rows(osel), rows(ow), rows(gl), full(ge),
                  rows(od[0]), rows(od[1]), rows(od[2]), rows(lse[0]), rows(lse[1]), rows(lse[2]), rows(ml),
                  full(wn), full(wd), full(wo)],
        out_specs=pl.BlockSpec((tm, d), row),
        out_shape=jax.ShapeDtypeStruct((t, d), F32),
        scratch_shapes=[pltpu.VMEM((2, tm, LANES), F32)],
        compiler_params=_cparams(("arbitrary",)),
        name="mix_outproj",
    )(x2, g1, oc, osel, ow, gl, ge, *od, *lse, ml, wn, wd, wo)


def _router_kernel(x_ref, sc_ref, sh_ref, g_ref, wr_ref, br_ref, h_ref, eid_ref, wts_ref):
    h = _norm_mod(x_ref[...], g_ref[...], sc_ref[...], sh_ref[...])
    n_sub = h.shape[1] // LANES
    for j in range(n_sub):
        h_ref[pl.ds(j, h.shape[0], stride=n_sub), :] = h[:, j * LANES:(j + 1) * LANES]
    logit = lax.dot_general(wr_ref[...], h, (((1,), (1,)), ((), ())), preferred_element_type=F32,
                            precision=lax.Precision.HIGHEST) + br_ref[...]
    grp = jnp.zeros((1, h.shape[0]), jnp.int32)
    best = logit[0:1]
    for k in range(1, N_EXPERT_GROUPS):
        better = logit[k:k + 1] > best
        grp = jnp.where(better, k, grp)
        best = jnp.where(better, logit[k:k + 1], best)
    den = jnp.zeros_like(best)
    for k in range(N_EXPERT_GROUPS):
        den = den + jnp.exp(logit[k:k + 1] - best)
    p_grp = 1.0 / den
    le = logit[SUBLANES:SUBLANES + EXPERTS_PER_GROUP]
    for k in range(1, N_EXPERT_GROUPS):
        lo = SUBLANES + k * EXPERTS_PER_GROUP
        le = jnp.where(grp == k, logit[lo:lo + EXPERTS_PER_GROUP], le)
    rowi = lax.broadcasted_iota(jnp.int32, le.shape, 0)
    v1 = jnp.max(le, axis=0, keepdims=True)
    i1 = jnp.min(jnp.where(le == v1, rowi, EXPERTS_PER_GROUP), axis=0, keepdims=True)
    rest = jnp.where(rowi == i1, -3e38, le)
    v2 = jnp.max(rest, axis=0, keepdims=True)
    i2 = jnp.min(jnp.where(rest == v2, rowi, EXPERTS_PER_GROUP), axis=0, keepdims=True)
    e2 = jnp.exp(v2 - v1)
    inv = p_grp / (1.0 + e2)
    eid_ref[...] = jnp.concatenate([grp * EXPERTS_PER_GROUP + i1, grp * EXPERTS_PER_GROUP + i2], axis=0)
    wts_ref[...] = jnp.concatenate([inv, e2 * inv], axis=0)


def _router(x2, sc, sh, g, wr_t, br, seq):
    t, d = x2.shape
    tm = TM_PROJ
    per_b = seq // tm
    row = lambda i: (i, 0)
    const = lambda i: (0, 0)
    return pl.pallas_call(
        _router_kernel,
        grid=(t // tm,),
        in_specs=[pl.BlockSpec((tm, d), row),
                  pl.BlockSpec((None, 1, d), lambda i: (i // per_b, 0, 0)),
                  pl.BlockSpec((None, 1, d), lambda i: (i // per_b, 0, 0)),
                  pl.BlockSpec((1, d), const),
                  pl.BlockSpec(wr_t.shape, const),
                  pl.BlockSpec(br.shape, const)],
        out_specs=[pl.BlockSpec((tm * (d // LANES), LANES), row),
                   pl.BlockSpec((2, tm), lambda i: (0, i)),
                   pl.BlockSpec((2, tm), lambda i: (0, i))],
        out_shape=[jax.ShapeDtypeStruct((t * (d // LANES), LANES), F32),
                   jax.ShapeDtypeStruct((2, t), jnp.int32),
                   jax.ShapeDtypeStruct((2, t), F32)],
        compiler_params=_cparams(("arbitrary",)),
        name="norm_router",
    )(x2, sc, sh, g, wr_t, br)


def _moe_kernel(cnt_ref, off_ref, tok_ref, wt_ref, h_ref, w1_ref, w3_ref, w2_ref, o_ref,
                xs_ref, os_ref, xb_ref, y3_ref):
    c = pl.program_id(0)
    e = pl.program_id(1)
    n_sub = xb_ref.shape[1] // LANES
    rb = xb_ref.shape[0]

    @pl.when(e == 0)
    def _():
        o_ref[...] = jnp.zeros_like(o_ref)

    n = cnt_ref[c, e]
    off = off_ref[c, e]

    def every_row(base, last, fn):
        def grp(g, _):
            slot0 = base + g * MOE_UNROLL
            row0 = pl.multiple_of(g * (MOE_UNROLL * n_sub), MOE_UNROLL * n_sub)
            for u in range(MOE_UNROLL):
                fn(jnp.minimum(slot0 + u, last), pl.ds(row0 + u * n_sub, n_sub))
            return 0

        lax.fori_loop(0, rb // MOE_UNROLL, grp, 0)

    def token_rows(slot):
        return pl.ds(pl.multiple_of(tok_ref[0, slot], n_sub), n_sub)

    def block(bi, _):
        base = off + bi * rb
        last = off + jnp.minimum(n, (bi + 1) * rb) - 1

        def gather(slot, rows):
            src = token_rows(slot)
            xs_ref[rows, :] = h_ref[src, :]
            os_ref[rows, :] = o_ref[src, :]

        every_row(base, last, gather)
        for j in range(n_sub):
            xb_ref[:, j * LANES:(j + 1) * LANES] = xs_ref[pl.ds(j, rb, stride=n_sub), :].astype(BF16)
        xb = xb_ref[...]
        a = jnp.dot(xb, w1_ref[...], preferred_element_type=F32)
        b = jnp.dot(xb, w3_ref[...], preferred_element_type=F32)
        mid = (a * (1.0 / (1.0 + jnp.exp(-a))) * b).astype(BF16)
        y = jnp.dot(mid, w2_ref[...], preferred_element_type=F32)
        for j in range(n_sub):
            y3_ref[pl.ds(j, rb, stride=n_sub), :] = y[:, j * LANES:(j + 1) * LANES]

        def scatter(slot, rows):
            o_ref[token_rows(slot), :] = os_ref[rows, :] + wt_ref[0, slot] * y3_ref[rows, :]

        every_row(base, last, scatter)
        return 0

    lax.fori_loop(0, (n + rb - 1) // rb, block, 0)


def _moe(h2, eid, wts, w1, w3, w2):
    n_sub = w1.shape[1] // LANES
    t, d = h2.shape[0] // n_sub, w1.shape[1]
    tc = min(MOE_CHUNK, t)
    n_chunks = t // tc
    slots = 2 * tc
    tok = jnp.arange(t, dtype=jnp.int32)
    key = ((tok // tc)[None, :] * N_EXPERTS + eid).reshape(-1)
    order = jnp.argsort(key)
    tok_sorted = (jnp.tile(tok % tc * n_sub, 2)[order]).reshape(n_chunks, slots)
    w_sorted = wts.reshape(-1)[order].reshape(n_chunks, slots)
    counts = jnp.zeros((n_chunks * N_EXPERTS,), jnp.int32).at[key].add(1).reshape(n_chunks, N_EXPERTS)
    starts = jnp.cumsum(counts, axis=1) - counts
    tok_sorted = tok_sorted.reshape(n_chunks, 1, slots)
    w_sorted = w_sorted.reshape(n_chunks, 1, slots)

    grid_spec = pltpu.PrefetchScalarGridSpec(
        num_scalar_prefetch=2,
        grid=(n_chunks, N_EXPERTS),
        in_specs=[pl.BlockSpec((None, 1, slots), lambda c, e, *_: (c, 0, 0), memory_space=pltpu.SMEM),
                  pl.BlockSpec((None, 1, slots), lambda c, e, *_: (c, 0, 0), memory_space=pltpu.SMEM),
                  pl.BlockSpec((tc * n_sub, LANES), lambda c, e, *_: (c, 0), pipeline_mode=pl.Buffered(1)),
                  pl.BlockSpec((None, d, D_EXPERT), lambda c, e, *_: (e, 0, 0)),
                  pl.BlockSpec((None, d, D_EXPERT), lambda c, e, *_: (e, 0, 0)),
                  pl.BlockSpec((None, D_EXPERT, d), lambda c, e, *_: (e, 0, 0))],
        out_specs=pl.BlockSpec((tc * n_sub, LANES), lambda c, e, *_: (c, 0), pipeline_mode=pl.Buffered(1)),
        scratch_shapes=[pltpu.VMEM((MOE_ROWS * n_sub, LANES), F32),
                        pltpu.VMEM((MOE_ROWS * n_sub, LANES), F32),
                        pltpu.VMEM((MOE_ROWS, d), BF16),
                        pltpu.VMEM((MOE_ROWS * n_sub, LANES), F32)],
    )
    out = pl.pallas_call(
        _moe_kernel,
        grid_spec=grid_spec,
        out_shape=jax.ShapeDtypeStruct((t * n_sub, LANES), F32),
        compiler_params=_cparams(("arbitrary", "arbitrary")),
        name="moe_experts",
    )(counts, starts, tok_sorted, w_sorted, h2, w1, w3, w2)
    return out


def _resid_kernel(x_ref, y_ref, g_ref, nf_ref, o_ref, *, final):
    tm, d = x_ref.shape
    n_sub = d // LANES
    y = jnp.concatenate([y_ref[pl.ds(j, tm, stride=n_sub), :] for j in range(n_sub)], axis=1)
    x = x_ref[...] + g_ref[...] * y
    if final:
        ms = jnp.mean(x * x, axis=-1, keepdims=True)
        x = x * lax.rsqrt(ms + RMS_EPS) * nf_ref[...]
    o_ref[...] = x


def _residual(x2, y2, g2, norm_f, seq, final):
    t, d = x2.shape
    tm = TM_PROJ
    per_b = seq // tm
    row = lambda i: (i, 0)
    return pl.pallas_call(
        functools.partial(_resid_kernel, final=final),
        grid=(t // tm,),
        in_specs=[pl.BlockSpec((tm, d), row), pl.BlockSpec((tm * (d // LANES), LANES), row),
                  pl.BlockSpec((None, 1, d), lambda i: (i // per_b, 0, 0)),
                  pl.BlockSpec((1, d), lambda i: (0, 0))],
        out_specs=pl.BlockSpec((tm, d), row),
        out_shape=jax.ShapeDtypeStruct((t, d), F32),
        compiler_params=_cparams(("arbitrary",)),
        name="residual_final" if final else "residual",
    )(x2, y2, g2, norm_f)


def _split_w_in(w_in, d):
    scale = HEAD_DIM ** -0.5 * LOG2E
    nq = NSA_HEADS * HEAD_DIM
    nkv = 3 * 2 * NSA_KV_HEADS * HEAD_DIM
    ngate = 3 * NSA_HEADS
    ndil = 3 * N_DIL_GROUPS * DIL_HEADS_PER_GROUP * HEAD_DIM
    o1, o2, o3 = nq, nq + nkv, nq + nkv + ngate
    o4 = o3 + ndil
    wq = (w_in[:, :o1] * scale).reshape(d, NSA_KV_HEADS, NSA_GROUP, HEAD_DIM)
    wq = wq.transpose(0, 2, 1, 3).reshape(d, nq)
    wkv = w_in[:, o1:o2]
    wg = jnp.pad(w_in[:, o2:o3], ((0, 0), (0, LANES - ngate)))
    gw = DIL_HEADS_PER_GROUP * HEAD_DIM
    per_which = N_DIL_GROUPS * gw
    wds = []
    for grp in range(N_DIL_GROUPS):
        parts = [w_in[:, o3 + which * per_which + grp * gw: o3 + which * per_which + (grp + 1) * gw]
                 for which in range(3)]
        parts[0] = parts[0] * scale
        wds.append(jnp.concatenate(parts, axis=1))
    wm = w_in[:, o4:]
    cast = lambda w: w.astype(BF16)
    return [cast(wq), cast(wkv), cast(wds[0]), cast(wds[1]), cast(wds[2]), cast(wm), cast(wg)]


def kernel(x, c, rel_bias, ada_w, ada_b, norm1, norm2, w_in, cmp_pos, cmp_w1, cmp_w2, w_up_nsa, w_up_dil, w_o,
           router_wg, router_bg, router_we, router_be, exp_w1, exp_w3, exp_w2, norm_f):
    batch, seq, d = x.shape
    depth = ada_w.shape[0]
    t = batch * seq
    n_cmp = seq // CMP_STRIDE
    n_sel = seq // SEL_BLOCK
    assert seq % TQ_NSA == 0 and n_sel <= HEAD_DIM and n_sel >= SEL_TOPN
    assert all(seq % (dil * TQ_DIL) == 0 for _, dil in DIL_PAIRS)

    mod = _modulation(c, ada_w, ada_b)

    nq_nsa = seq // TQ_NSA
    bias_cmp = _expand_bias(rel_bias, _cmp_buckets(seq, n_cmp), 0, NSA_HEADS, 8 * SUBLANES)
    sel_b = _toeplitz_buckets(nq_nsa, TQ_NSA, seq, 1)
    n_sel_bias = nq_nsa
    while n_sel_bias > 1 and (sel_b[n_sel_bias - 2:] == sel_b[n_sel_bias - 1, 0, 0]).all():
        n_sel_bias -= 1
    sel_b = sel_b[:n_sel_bias].reshape(n_sel_bias * TQ_NSA, TQ_NSA)
    bias_sel = _expand_bias(rel_bias, sel_b, 0, NSA_HEADS, TQ_NSA).reshape(NSA_HEADS, n_sel_bias, TQ_NSA, TQ_NSA)
    nb_win = min(-(-NSA_WINDOW // TQ_NSA), nq_nsa - 1)
    win_b = _toeplitz_buckets(nb_win + 1, TQ_NSA, NSA_WINDOW, 1, masked_tail=True).reshape(-1, TQ_NSA)
    bias_win = _expand_bias(rel_bias, win_b, 0, NSA_HEADS, TQ_NSA).reshape(NSA_HEADS, nb_win + 2, TQ_NSA, TQ_NSA)
    bias_dil, nb_dil = [], []
    for grp, (window, dil) in enumerate(DIL_PAIRS):
        nb = min(-(-(window // dil) // TQ_DIL), seq // dil // TQ_DIL - 1)
        bk = _toeplitz_buckets(nb + 1, TQ_DIL, window // dil, dil, masked_tail=True).reshape(-1, TQ_DIL)
        hb = NSA_HEADS + grp * DIL_HEADS_PER_GROUP
        bias_dil.append(_expand_bias(rel_bias, bk, hb, DIL_HEADS_PER_GROUP, TQ_DIL)
                        .reshape(DIL_HEADS_PER_GROUP, nb + 2, TQ_DIL, TQ_DIL))
        nb_dil.append(nb)
    ovl_t = jnp.asarray(_overlap_t(n_cmp, n_sel), BF16)
    onehot = jnp.asarray(_block_onehot(seq), BF16)
    gate_e = jnp.asarray(_gate_expand(), BF16)

    x2 = x.reshape(t, d)
    for l in range(depth):
        sh1, sc1, g1, sh2, sc2, g2 = [m.reshape(batch, 1, d) for m in jnp.split(mod[l], 6, axis=-1)]
        weights = _split_w_in(w_in[l], d)
        (q_n, kc_in, vc_in, k_sel, v_sel, k_win, v_win, qkv_d0, qkv_d1, qkv_d2, merge_l, gate_l) = _inproj(
            x2, sc1, sh1, norm1[l].reshape(1, d), onehot, weights, seq)

        kc = _compress(kc_in, cmp_pos[l, 0], cmp_w1[l, 0], cmp_w2[l, 0], batch, seq, transpose_out=False)
        vct = _compress(vc_in, cmp_pos[l, 1], cmp_w1[l, 1], cmp_w2[l, 1], batch, seq, transpose_out=True)
        o_c, selb = _cmp_select(q_n, kc, vct, bias_cmp, ovl_t, batch, seq)
        nsa_common = dict(batch=batch, length=seq, dil=1, tile=TQ_NSA, n_pairs=NSA_GROUP, q_blk=lambda r: 0,
                          k_blk=lambda r: 0, v_blk=lambda r: 0, v_width=LANES, per_pair_kv=False,
                          heads=tuple((g, NSA_GROUP + g) for g in range(NSA_GROUP)))
        q3 = q_n.reshape(batch, seq, 512)
        (o_s,) = _flash(q3, k_sel.reshape(batch, seq, 2 * LANES), v_sel.reshape(batch, seq, LANES), bias_sel,
                        n_back=None, k_width=2 * LANES, selb=selb.reshape(batch, seq, 2 * LANES), **nsa_common)
        (o_w,) = _flash(q3, k_win.reshape(batch, seq, LANES), v_win.reshape(batch, seq, LANES), bias_win,
                        n_back=nb_win, k_width=LANES, q_tiles=WIN_Q_TILES, **nsa_common)

        o_d, lse_d = [], []
        for grp, ((window, dil), qkv) in enumerate(zip(DIL_PAIRS, (qkv_d0, qkv_d1, qkv_d2))):
            length = seq // dil
            view = qkv.reshape(batch * dil, length, 768)
            o_g, lse_g = _flash(view, view, view, bias_dil[grp], batch=batch * dil, length=length, dil=1,
                                tile=TQ_DIL, n_back=nb_dil[grp], n_pairs=2, q_blk=lambda r: 0, k_blk=lambda r: 1,
                                k_width=2 * LANES, v_blk=lambda r: 2, v_width=2 * LANES,
                                per_pair_kv=True, heads=((0, 1), (2, 3)), with_lse=True,
                                q_tiles=math.gcd(DIL_Q_TILES, length // TQ_DIL))
            shape = (t, 256) if dil == 1 else (batch, dil, length, 256)
            o_d.append(o_g.reshape(shape))
            lse_d.append(lse_g.reshape(shape))

        wn = w_up_nsa[l].reshape(NSA_KV_HEADS, NSA_GROUP, HEAD_DIM, d).transpose(1, 0, 2, 3).reshape(512, d)
        x2 = _mix(x2, g1, o_c, o_s.reshape(t, 512), o_w.reshape(t, 512), gate_l, gate_e, o_d, lse_d, merge_l,
                  wn.astype(BF16), w_up_dil[l].astype(BF16), w_o[l].astype(BF16), seq)

        wr_t = jnp.concatenate([jnp.pad(router_wg[l], ((0, 0), (0, SUBLANES - N_EXPERT_GROUPS))),
                                router_we[l]], axis=1).T
        br = jnp.concatenate([jnp.pad(router_bg[l], (0, SUBLANES - N_EXPERT_GROUPS)),
                              router_be[l]]).reshape(-1, 1)
        h2, eid, wts = _router(x2, sc2, sh2, norm2[l].reshape(1, d), wr_t, br, seq)
        y = _moe(h2, eid, wts, exp_w1[l].astype(BF16), exp_w3[l].astype(BF16), exp_w2[l].astype(BF16))
        x2 = _residual(x2, y, g2, norm_f.reshape(1, d), seq, final=(l == depth - 1))
    return x2.reshape(batch, seq, d)
```

```python
import functools
import math

import numpy as np
import jax
import jax.numpy as jnp
from jax import lax
from jax.experimental import pallas as pl
from jax.experimental.pallas import tpu as pltpu

F32 = jnp.float32
BF16 = jnp.bfloat16

HEAD_DIM = 64
NSA_HEADS = 8
NSA_KV_HEADS = 2
NSA_GROUP = NSA_HEADS // NSA_KV_HEADS
CMP_LEN = 32
CMP_STRIDE = 16
CMP_HIDDEN = 256
SEL_BLOCK = 64
SEL_TOPN = 16
NSA_WINDOW = 512
FORCE_SCORE = 1e4
DIL_PAIRS = ((128, 1), (512, 4), (2048, 16))
N_DIL_GROUPS = 3
DIL_HEADS_PER_GROUP = 4
NUM_BUCKETS = 32
REL_MAX_DIST = 2048
N_EXPERT_GROUPS = 4
EXPERTS_PER_GROUP = 8
N_EXPERTS = N_EXPERT_GROUPS * EXPERTS_PER_GROUP
D_EXPERT = 512
RMS_EPS = 1e-6

LOG2E = math.log2(math.e)
LANES = 128
SUBLANES = 8
MASK_NEG = -1e30
SEL_NEG = -1e9
TQ_NSA = 256
TQ_DIL = 128
TM_PROJ = 512
WIN_Q_TILES = 2
DIL_Q_TILES = 4
FLASH_LOOKAHEAD = 4
MOE_CHUNK = 4096
MOE_ROWS = 128
MOE_UNROLL = 16
VMEM_LIMIT = 56 * 1024 * 1024


def _cparams(sem):
    return pltpu.CompilerParams(dimension_semantics=sem, vmem_limit_bytes=VMEM_LIMIT)


def _np_bucket(dist):
    dist = np.maximum(dist, 0)
    max_exact = NUM_BUCKETS // 2
    df = np.maximum(dist, 1).astype(np.float32)
    val = np.log(df / np.float32(max_exact)) / np.float32(math.log(REL_MAX_DIST / max_exact))
    large = max_exact + (val * np.float32(NUM_BUCKETS - max_exact)).astype(np.int32)
    large = np.minimum(large, NUM_BUCKETS - 1)
    return np.where(dist < max_exact, dist, large).astype(np.int32)


def _toeplitz_buckets(n_delta, tile, window, dist_scale, masked_tail=False):
    dd = np.arange(n_delta)[:, None, None]
    r = np.arange(tile)[None, :, None]
    c = np.arange(tile)[None, None, :]
    dist = dd * tile + c - r
    valid = (dist >= 0) & (dist <= window)
    out = np.where(valid, _np_bucket(dist * dist_scale), -1).astype(np.int32)
    if masked_tail:
        out = np.concatenate([out, np.full((1, tile, tile), -1, np.int32)], axis=0)
    return out


def _cmp_buckets(seq, n_cmp):
    c_end = np.arange(n_cmp)[:, None] * CMP_STRIDE + CMP_LEN - 1
    t = np.arange(seq)[None, :]
    dist = t - c_end
    return np.where(dist >= 0, _np_bucket(dist), -1).astype(np.int32)


def _overlap_t(n_cmp, n_sel):
    c_start = np.arange(n_cmp)[None, :] * CMP_STRIDE
    s_start = np.arange(n_sel)[:, None] * SEL_BLOCK
    ov = np.clip(np.minimum(c_start + CMP_LEN, s_start + SEL_BLOCK) - np.maximum(c_start, s_start), 0, None)
    out = np.zeros((LANES, n_cmp), np.float32)
    out[64:64 + n_sel] = ov.astype(np.float32) / CMP_STRIDE
    return out


def _block_onehot(seq):
    oh = np.zeros((seq, LANES), np.float32)
    blk = np.arange(seq) // SEL_BLOCK
    oh[np.arange(seq), blk] = 1.0
    oh[np.arange(seq), 64 + blk] = 1.0
    return oh


def _gate_expand():
    e = np.zeros((LANES, 3 * NSA_HEADS * HEAD_DIM), np.float32)
    for br in range(3):
        for g in range(NSA_GROUP):
            for ln in range(LANES):
                kv = ln // HEAD_DIM
                e[br * NSA_HEADS + kv * NSA_GROUP + g, br * 512 + g * LANES + ln] = 1.0
    return e


def _mod_kernel(c_ref, w_ref, b_ref, o_ref):
    c = c_ref[...]
    cond = c * (1.0 / (1.0 + jnp.exp(-c)))
    o_ref[...] = jnp.dot(cond, w_ref[...], preferred_element_type=F32,
                         precision=lax.Precision.HIGHEST) + b_ref[...]


def _modulation(c, ada_w, ada_b):
    depth, d, n = ada_w.shape
    b = c.shape[0]
    tn = 1536
    return pl.pallas_call(
        _mod_kernel,
        grid=(depth, n // tn),
        in_specs=[pl.BlockSpec((b, d), lambda l, j: (0, 0)),
                  pl.BlockSpec((None, d, tn), lambda l, j: (l, 0, j)),
                  pl.BlockSpec((None, 1, tn), lambda l, j: (l, 0, j))],
        out_specs=pl.BlockSpec((None, b, tn), lambda l, j: (l, 0, j)),
        out_shape=jax.ShapeDtypeStruct((depth, b, n), F32),
        compiler_params=_cparams(("arbitrary", "arbitrary")),
        name="adaln_modulation",
    )(c, ada_w, ada_b.reshape(depth, 1, n))


def _bias_kernel(tbl_ref, bkt_ref, o_ref, *, head_base):
    h = pl.program_id(0) + head_base
    bkt = bkt_ref[...]
    acc = jnp.full(bkt.shape, MASK_NEG, F32)
    for b in range(NUM_BUCKETS):
        acc = jnp.where(bkt == b, tbl_ref[b, h] * LOG2E, acc)
    o_ref[...] = acc


def _expand_bias(rel_bias, buckets, head_base, n_heads, row_tile):
    rows, cols = buckets.shape
    return pl.pallas_call(
        functools.partial(_bias_kernel, head_base=head_base),
        grid=(n_heads, rows // row_tile),
        in_specs=[pl.BlockSpec(memory_space=pltpu.SMEM),
                  pl.BlockSpec((row_tile, cols), lambda h, i: (i, 0))],
        out_specs=pl.BlockSpec((None, row_tile, cols), lambda h, i: (h, i, 0)),
        out_shape=jax.ShapeDtypeStruct((n_heads, rows, cols), F32),
        compiler_params=_cparams(("arbitrary", "arbitrary")),
        name="bias_expand",
    )(rel_bias, jnp.asarray(buckets))


def _norm_mod(x, g, sc, sh):
    ms = jnp.mean(x * x, axis=-1, keepdims=True)
    y = x * lax.rsqrt(ms + RMS_EPS) * g
    return y * (1.0 + sc) + sh


def _inproj_kernel(x_ref, sc_ref, sh_ref, g_ref, oh_ref, wq_ref, wkv_ref, wd0_ref, wd1_ref, wd2_ref, wm_ref, wg_ref,
                   q_ref, kc_ref, vc_ref, ks_ref, vs_ref, kw_ref, vw_ref, d0_ref, d1_ref, d2_ref, m_ref, gl_ref,
                   slab_ref):
    h = _norm_mod(x_ref[...], g_ref[...], sc_ref[...], sh_ref[...]).astype(BF16)
    tm = h.shape[0]

    def by_residue(w_ref, o_ref):
        dil = o_ref.shape[0]
        res = jnp.dot(h, w_ref[...], preferred_element_type=F32)
        n_slab = res.shape[1] // LANES
        for c in range(n_slab):
            slab_ref[c] = res[:, c * LANES:(c + 1) * LANES]
        for r in range(dil):
            for c in range(n_slab):
                o_ref[r, :, c * LANES:(c + 1) * LANES] = slab_ref[c, pl.ds(r, tm // dil, stride=dil), :].astype(BF16)

    def proj(w_ref):
        return jnp.dot(h, w_ref[...], preferred_element_type=F32)

    q_ref[...] = proj(wq_ref).astype(BF16)
    kv = proj(wkv_ref).astype(BF16)
    for k, ref in ((0, kc_ref), (1, vc_ref), (3, vs_ref), (4, kw_ref), (5, vw_ref)):
        ref[...] = kv[:, k * LANES:(k + 1) * LANES]
    k_sel = kv[:, 2 * LANES:3 * LANES]
    oh = oh_ref[...]
    lo = lax.broadcasted_iota(jnp.int32, k_sel.shape, 1) < HEAD_DIM
    ks_ref[:, :LANES] = jnp.where(lo, k_sel, oh)
    ks_ref[:, LANES:] = jnp.where(lo, oh, k_sel)
    d0_ref[...] = proj(wd0_ref).astype(BF16)
    by_residue(wd1_ref, d1_ref)
    by_residue(wd2_ref, d2_ref)
    m_ref[...] = proj(wm_ref).astype(BF16)
    gl_ref[...] = proj(wg_ref)


def _inproj(x2, sc, sh, g, onehot, weights, seq):
    t, d = x2.shape
    tm = TM_PROJ
    per_b = seq // tm
    widths = (512, 128, 128, 256, 128, 128, 128, 768, 768, 768, 2 * d, LANES)
    dtypes = (BF16,) * 11 + (F32,)
    row = lambda i: (i, 0)
    const = lambda i: (0, 0)
    in_specs = [pl.BlockSpec((tm, d), row),
                pl.BlockSpec((None, 1, d), lambda i: (i // per_b, 0, 0)),
                pl.BlockSpec((None, 1, d), lambda i: (i // per_b, 0, 0)),
                pl.BlockSpec((1, d), const),
                pl.BlockSpec((tm, LANES), lambda i: (i % per_b, 0))]
    in_specs += [pl.BlockSpec(w.shape, const) for w in weights]
    out_specs = [pl.BlockSpec((tm, w), row) for w in widths]
    out_shape = [jax.ShapeDtypeStruct((t, w), dt) for w, dt in zip(widths, dtypes)]
    for k, (_, dil) in zip((8, 9), DIL_PAIRS[1:]):
        out_specs[k] = pl.BlockSpec((None, dil, tm // dil, 768), lambda i: (i // per_b, 0, i % per_b, 0))
        out_shape[k] = jax.ShapeDtypeStruct((t // seq, dil, seq // dil, 768), BF16)
    return pl.pallas_call(
        _inproj_kernel,
        grid=(t // tm,),
        in_specs=in_specs,
        out_specs=out_specs,
        out_shape=out_shape,
        scratch_shapes=[pltpu.VMEM((768 // LANES, tm, LANES), F32)],
        compiler_params=_cparams(("arbitrary",)),
        name="norm_inproj",
    )(x2, sc, sh, g, onehot, *weights)


def _gelu_tanh(x):
    return 0.5 * x * (1.0 + jnp.tanh(math.sqrt(2.0 / math.pi) * (x + 0.044715 * (x * x * x))))


def _compress_kernel(r_ref, pt_ref, pb_ref, wt_ref, wb_ref, w2_ref, o_ref, *, transpose_out):
    r = r_ref[...].astype(F32)
    top = jnp.dot((r + pt_ref[...]).astype(BF16), wt_ref[...], preferred_element_type=F32)
    bot = jnp.dot((r + pb_ref[...]).astype(BF16), wb_ref[...], preferred_element_type=F32)
    n = bot.shape[0]
    hid = top + pltpu.roll(bot, n - 1, 0)
    act = _gelu_tanh(hid).astype(BF16)
    if transpose_out:
        o_ref[...] = lax.dot_general(w2_ref[...], act, (((1,), (1,)), ((), ())),
                                     preferred_element_type=F32).astype(BF16)
    else:
        o_ref[...] = jnp.dot(act, w2_ref[...], preferred_element_type=F32).astype(BF16)


def _compress(tok, pos, w1, w2, batch, seq, transpose_out):
    nc = seq // CMP_STRIDE
    half = CMP_LEN // 2
    eye = jnp.eye(NSA_KV_HEADS, dtype=F32)
    w1r = w1.reshape(CMP_LEN, HEAD_DIM, CMP_HIDDEN)
    blk = lambda w: jnp.einsum('ldn,hg->lhdgn', w, eye).reshape(half * LANES, 2 * CMP_HIDDEN).astype(BF16)
    wt, wb = blk(w1r[:half]), blk(w1r[half:])
    posr = lambda p: jnp.broadcast_to(p[:, None, :], (half, NSA_KV_HEADS, HEAD_DIM)).reshape(1, half * LANES)
    pt, pb = posr(pos[:half]), posr(pos[half:])
    w2b = jnp.einsum('nd,hg->hngd', w2, eye).reshape(2 * CMP_HIDDEN, LANES)
    if transpose_out:
        w2b = w2b.T
        out_block, out_shape = (None, LANES, nc), (batch, LANES, nc)
    else:
        out_block, out_shape = (None, nc, LANES), (batch, nc, LANES)
    w2b = w2b.astype(BF16)
    const = lambda b: (0, 0)
    return pl.pallas_call(
        functools.partial(_compress_kernel, transpose_out=transpose_out),
        grid=(batch,),
        in_specs=[pl.BlockSpec((None, nc, half * LANES), lambda b: (b, 0, 0)),
                  pl.BlockSpec(pt.shape, const), pl.BlockSpec(pb.shape, const),
                  pl.BlockSpec(wt.shape, const), pl.BlockSpec(wb.shape, const),
                  pl.BlockSpec(w2b.shape, const)],
        out_specs=pl.BlockSpec(out_block, lambda b: (b, 0, 0)),
        out_shape=jax.ShapeDtypeStruct(out_shape, BF16),
        compiler_params=_cparams(("arbitrary",)),
        name="nsa_compress",
    )(tok.reshape(batch, nc, half * LANES), pt, pb, wt, wb, w2b)


def _cmp_select_kernel(q_ref, kc_ref, vct_ref, bias_ref, ovl_ref, o_ref, selb_ref, *, n_sel, n_top):
    tq = q_ref.shape[0]
    nc = kc_ref.shape[0]
    qs = pl.program_id(1) * tq
    kc = kc_ref[...]
    vct = vct_ref[...]
    lane_q = lax.broadcasted_iota(jnp.int32, (tq, LANES), 1)
    row_o = lax.broadcasted_iota(jnp.int32, (LANES, tq), 0)
    heads = [(g, kv) for g in range(NSA_GROUP) for kv in range(NSA_KV_HEADS)]

    def scores(g, kv):
        qt = q_ref[:, g * LANES:(g + 1) * LANES]
        mine = (lane_q < HEAD_DIM) if kv == 0 else (lane_q >= HEAD_DIM)
        qm = jnp.where(mine, qt, jnp.zeros_like(qt))
        s = lax.dot_general(kc, qm, (((1,), (1,)), ((), ())), preferred_element_type=F32)
        return s + bias_ref[kv * NSA_GROUP + g]

    pc_sum = [jnp.zeros((nc, tq), F32) for _ in range(NSA_KV_HEADS)]
    outs = {}
    ahead = [scores(*hd) for hd in heads[:FLASH_LOOKAHEAD]]
    for n, (g, kv) in enumerate(heads):
        s = ahead.pop(0)
        if n + FLASH_LOOKAHEAD < len(heads):
            ahead.append(scores(*heads[n + FLASH_LOOKAHEAD]))
        m = jnp.max(s, axis=0, keepdims=True)
        m = jnp.where(m < 0.5 * MASK_NEG, 0.0, m)
        p = jnp.exp2(s - m)
        den = jnp.sum(p, axis=0, keepdims=True)
        pc = p * (1.0 / jnp.where(den > 0.0, den, 1.0))
        pc_sum[kv] = pc_sum[kv] + pc
        outs[kv] = jnp.dot(vct, pc.astype(BF16), preferred_element_type=F32)
        if kv == NSA_KV_HEADS - 1:
            o_t = jnp.where(row_o < HEAD_DIM, outs[0], outs[1])
            o_ref[:, g * LANES:(g + 1) * LANES] = o_t.T.astype(BF16)

    ovl = ovl_ref[...]
    imps = []
    for kv in range(NSA_KV_HEADS):
        hi = pc_sum[kv].astype(BF16)
        lo = (pc_sum[kv] - hi.astype(F32)).astype(BF16)
        imps.append((jnp.dot(ovl, hi, preferred_element_type=F32)
                     + jnp.dot(ovl, lo, preferred_element_type=F32))[HEAD_DIM:])
    imp = jnp.concatenate(imps, axis=1)
    wide = (HEAD_DIM, NSA_KV_HEADS * tq)
    rowj = lax.broadcasted_iota(jnp.int32, wide, 0)
    col = lax.broadcasted_iota(jnp.int32, wide, 1)
    t = qs + jnp.where(col >= tq, col - tq, col)
    jq = jnp.right_shift(t, SEL_BLOCK.bit_length() - 1)
    forced = (rowj == 0) | (rowj == jq) | (rowj == jq - 1)
    score = jnp.where(forced, FORCE_SCORE, jnp.where(rowj > jq, -1.0, imp))
    rem = jnp.where(rowj < n_sel, score, -3e38)
    sel = jnp.zeros(wide, F32)
    for _ in range(n_top):
        m = jnp.max(rem, axis=0, keepdims=True)
        idx = jnp.min(jnp.where(rem == m, rowj, HEAD_DIM), axis=0, keepdims=True)
        pick = rowj == idx
        sel = jnp.where(pick, 1.0, sel)
        rem = jnp.where(pick, -3e38, rem)
    sb = jnp.where(sel > 0.5, 0.0, SEL_NEG)
    zero = jnp.zeros((HEAD_DIM, tq), F32)
    selb_ref[:, :LANES] = jnp.concatenate([zero, sb[:, :tq]], axis=0).T.astype(BF16)
    selb_ref[:, LANES:] = jnp.concatenate([sb[:, tq:], zero], axis=0).T.astype(BF16)


def _cmp_select(q, kc, vct, bias_c, ovl_t, batch, seq):
    t = q.shape[0]
    tq = TQ_NSA
    nq = seq // tq
    nc = seq // CMP_STRIDE
    n_sel = seq // SEL_BLOCK
    n_top = min(SEL_TOPN, n_sel)
    return pl.pallas_call(
        functools.partial(_cmp_select_kernel, n_sel=n_sel, n_top=n_top),
        grid=(batch, nq),
        in_specs=[pl.BlockSpec((tq, 512), lambda b, i: (b * nq + i, 0)),
                  pl.BlockSpec((None, nc, LANES), lambda b, i: (b, 0, 0)),
                  pl.BlockSpec((None, LANES, nc), lambda b, i: (b, 0, 0)),
                  pl.BlockSpec((NSA_HEADS, nc, tq), lambda b, i: (0, 0, i)),
                  pl.BlockSpec((LANES, nc), lambda b, i: (0, 0))],
        out_specs=[pl.BlockSpec((tq, 512), lambda b, i: (b * nq + i, 0)),
                   pl.BlockSpec((tq, 2 * LANES), lambda b, i: (b * nq + i, 0))],
        out_shape=[jax.ShapeDtypeStruct((t, 512), BF16),
                   jax.ShapeDtypeStruct((t, 2 * LANES), BF16)],
        compiler_params=_cparams(("arbitrary", "arbitrary")),
        name="nsa_cmp_select",
    )(q, kc, vct, bias_c, ovl_t)


def _flash_kernel(*refs, tile, q_tiles, n_pairs, heads, per_pair_kv, n_back, n_bias, with_sel, with_lse):
    it = iter(refs)
    q_ref, k_ref, v_ref, b_ref = (next(it) for _ in range(4))
    selb_ref = next(it) if with_sel else None
    o_ref = next(it)
    lse_ref = next(it) if with_lse else None
    qa_ref = next(it)
    acc_ref = next(it) if n_back is None else None

    i = pl.program_id(1)
    n_heads = 2 * n_pairs
    lo = lax.broadcasted_iota(jnp.int32, (tile, LANES), 1) < HEAD_DIM
    tr = lambda a: a.astype(F32).T.astype(BF16)
    for u in range(q_tiles):
        rows = slice(u * tile, (u + 1) * tile)
        for p in range(n_pairs):
            q = q_ref[rows, p * LANES:(p + 1) * LANES]
            if with_sel:
                qa_ref[u * n_heads + 2 * p] = tr(jnp.where(lo, q, selb_ref[rows, :LANES]))
                qa_ref[u * n_heads + 2 * p + 1] = tr(jnp.where(lo, selb_ref[rows, LANES:], q))
            else:
                zero = jnp.zeros_like(q)
                qa_ref[u * n_heads + 2 * p] = tr(jnp.where(lo, q, zero))
                qa_ref[u * n_heads + 2 * p + 1] = tr(jnp.where(lo, zero, q))
    nt = (((1,), (1,)), ((), ()))
    tn = (((0,), (0,)), ((), ()))
    top = lax.broadcasted_iota(jnp.int32, (LANES, tile), 0) < HEAD_DIM

    def finish(u, p, acc0, l0, m0, acc1, l1, m1):
        rows = slice(u * tile, (u + 1) * tile)
        o_t = jnp.where(top, acc0 * (1.0 / l0), acc1 * (1.0 / l1))
        o_ref[rows, p * LANES:(p + 1) * LANES] = o_t.T.astype(o_ref.dtype)
        if with_lse:
            lse_t = jnp.where(top, m0 + jnp.log2(l0), m1 + jnp.log2(l1))
            lse_ref[rows, p * LANES:(p + 1) * LANES] = lse_t.T

    def scores(unit, tiles):
        hd = unit % n_heads
        p, half = divmod(hd, 2)
        kc = half if with_sel else (p if per_pair_kv else 0)
        parts = []
        for j, bidx in tiles:
            kt = k_ref[pl.ds(pl.multiple_of(j * tile, tile), tile), kc * LANES:(kc + 1) * LANES]
            s = jnp.dot(kt, qa_ref[unit], preferred_element_type=F32)
            parts.append(s + b_ref[heads[p][half], bidx])
        return parts

    def group(unit_tiles, state):
        res = []
        n_units = len(unit_tiles)
        ahead = [scores(n, unit_tiles[n]) for n in range(min(FLASH_LOOKAHEAD, n_units))]
        for hd in range(n_units):
            tiles = unit_tiles[hd]
            parts = ahead.pop(0)
            if hd + FLASH_LOOKAHEAD < n_units:
                ahead.append(scores(hd + FLASH_LOOKAHEAD, unit_tiles[hd + FLASH_LOOKAHEAD]))
            m_new = functools.reduce(jnp.maximum, [jnp.max(s, axis=0, keepdims=True) for s in parts])
            if state is not None:
                m, l = state[hd]
                m_new = jnp.maximum(m, m_new)
                alpha = jnp.exp2(m - m_new)
            prs = [jnp.exp2(s - m_new) for s in parts]
            l_new = functools.reduce(jnp.add, [jnp.sum(pr, axis=0, keepdims=True) for pr in prs])
            vc = (hd % n_heads // 2) if per_pair_kv else 0
            pv = None
            for (j, _), pr in zip(tiles, prs):
                vt = v_ref[pl.ds(pl.multiple_of(j * tile, tile), tile), vc * LANES:(vc + 1) * LANES]
                d = lax.dot_general(vt, pr.astype(BF16), tn, preferred_element_type=F32)
                pv = d if pv is None else pv + d
            if state is not None:
                l_new = alpha * l + l_new
                pv = alpha * acc_ref[hd] + pv
            res.append((m_new, l_new, pv))
        return res

    if n_back is not None:
        unit_tiles = []
        for u in range(q_tiles):
            iu = i * q_tiles + u
            tiles = [(jnp.maximum(iu - k, 0), jnp.where(iu >= k, k, n_bias - 1)) for k in range(n_back + 1)]
            unit_tiles += [tiles] * n_heads
        res = group(unit_tiles, None)
        for u in range(q_tiles):
            for p in range(n_pairs):
                (m0, l0, a0), (m1, l1, a1) = res[u * n_heads + 2 * p], res[u * n_heads + 2 * p + 1]
                finish(u, p, a0, l0, m0, a1, l1, m1)
        return

    def store(res):
        for hd, (_, _, acc) in enumerate(res):
            acc_ref[hd] = acc
        return tuple(x for m, l, _ in res for x in (m, l))

    def unpack(carry):
        return [(carry[2 * hd], carry[2 * hd + 1]) for hd in range(n_heads)]

    bias_of = lambda back: jnp.minimum(back, n_bias - 1)
    carry = store(group([[(i, 0)]] * n_heads, None))

    def pair_body(step, carry):
        back = 2 * step + 1
        tiles = [(i - back, bias_of(back)), (i - back - 1, bias_of(back + 1))]
        return store(group([tiles] * n_heads, unpack(carry)))

    carry = lax.fori_loop(0, i // 2, pair_body, carry)

    def last_body(_, carry):
        return store(group([[(0, bias_of(i))]] * n_heads, unpack(carry)))

    carry = lax.fori_loop(0, i % 2, last_body, carry)
    fin = unpack(carry)
    for p in range(n_pairs):
        finish(0, p, acc_ref[2 * p], fin[2 * p][1], fin[2 * p][0], acc_ref[2 * p + 1], fin[2 * p + 1][1], fin[2 * p + 1][0])


def _flash(q_arr, k_arr, v_arr, bias, *, batch, length, dil, tile, n_back, n_pairs, q_blk, k_blk, k_width,
           v_blk, v_width, per_pair_kv, heads, selb=None, with_lse=False, q_tiles=1):
    nq = length // tile
    with_sel = selb is not None
    width = n_pairs * LANES

    in_specs = [pl.BlockSpec((None, q_tiles * tile, width), lambda br, i: (br // dil, i, q_blk(br % dil))),
                pl.BlockSpec((None, length, k_width), lambda br, i: (br // dil, 0, k_blk(br % dil))),
                pl.BlockSpec((None, length, v_width), lambda br, i: (br // dil, 0, v_blk(br % dil))),
                pl.BlockSpec(bias.shape, lambda br, i: (0, 0, 0, 0))]
    args = [q_arr, k_arr, v_arr, bias]
    if with_sel:
        in_specs.append(pl.BlockSpec((None, q_tiles * tile, 2 * LANES), lambda br, i: (br, i, 0)))
        args.append(selb)
    o_map = lambda br, i: (br // dil, i, br % dil)
    out_specs = [pl.BlockSpec((None, q_tiles * tile, width), o_map)]
    out_shape = [jax.ShapeDtypeStruct((batch, length, dil * width), BF16)]
    if with_lse:
        out_specs.append(pl.BlockSpec((None, q_tiles * tile, width), o_map))
        out_shape.append(jax.ShapeDtypeStruct((batch, length, dil * width), F32))
    return pl.pallas_call(
        functools.partial(_flash_kernel, tile=tile, q_tiles=q_tiles, n_pairs=n_pairs, heads=heads, per_pair_kv=per_pair_kv,
                          n_back=n_back, n_bias=bias.shape[1], with_sel=with_sel, with_lse=with_lse),
        grid=(batch * dil, nq // q_tiles),
        in_specs=in_specs,
        out_specs=out_specs,
        out_shape=out_shape,
        scratch_shapes=[pltpu.VMEM((q_tiles * 2 * n_pairs, LANES, tile), BF16)]
        + ([pltpu.VMEM((2 * n_pairs, LANES, tile), F32)] if n_back is None else []),
        compiler_params=_cparams(("arbitrary", "arbitrary")),
        name="flash_sel" if with_sel else ("flash_dil" if with_lse else "flash_win"),
    )(*args)


def _mix_kernel(x_ref, g1_ref, oc_ref, os_ref, ow_ref, gl_ref, ge_ref,
                od0_ref, od1_ref, od2_ref, l0_ref, l1_ref, l2_ref, ml_ref,
                wn_ref, wd_ref, wo_ref, o_ref, slab_ref):
    tm, d = x_ref.shape

    def in_token_order(ref):
        if len(ref.shape) == 2:
            return ref[...].astype(F32)
        dil = ref.shape[0]
        n_slab = ref.shape[2] // LANES
        for r in range(dil):
            for c in range(n_slab):
                slab_ref[c, pl.ds(r, tm // dil, stride=dil), :] = ref[r, :, c * LANES:(c + 1) * LANES].astype(F32)
        return jnp.concatenate([slab_ref[c] for c in range(n_slab)], axis=1)

    sig = 1.0 / (1.0 + jnp.exp(-gl_ref[...]))
    hi = sig.astype(BF16)
    lo = (sig - hi.astype(F32)).astype(BF16)
    ge = ge_ref[...]
    gates = jnp.dot(hi, ge, preferred_element_type=F32) + jnp.dot(lo, ge, preferred_element_type=F32)
    o_nsa = (gates[:, 0:512] * oc_ref[...].astype(F32)
             + gates[:, 512:1024] * os_ref[...].astype(F32)
             + gates[:, 1024:1536] * ow_ref[...].astype(F32))
    u_nsa = jnp.dot(o_nsa.astype(BF16), wn_ref[...], preferred_element_type=F32)

    lses = [in_token_order(ref) for ref in (l0_ref, l1_ref, l2_ref)]
    mx = jnp.maximum(jnp.maximum(lses[0], lses[1]), lses[2])
    es = [jnp.exp2(l - mx) for l in lses]
    inv = 1.0 / (es[0] + es[1] + es[2])
    o_dil = (es[0] * in_token_order(od0_ref) + es[1] * in_token_order(od1_ref)
             + es[2] * in_token_order(od2_ref)) * inv
    u_dil = jnp.dot(o_dil.astype(BF16), wd_ref[...], preferred_element_type=F32)

    gm = 1.0 / (1.0 + jnp.exp(-ml_ref[...].astype(F32)))
    merged = gm[:, :d] * u_nsa + gm[:, d:] * u_dil
    y = jnp.dot(merged.astype(BF16), wo_ref[...], preferred_element_type=F32)
    o_ref[...] = x_ref[...] + g1_ref[...] * y


def _mix(x2, g1, oc, osel, ow, gl, ge, od, lse, ml, wn, wd, wo, seq):
    t, d = x2.shape
    tm = TM_PROJ
    per_b = seq // tm
    row = lambda i: (i, 0)
    const = lambda i: (0, 0)
    full = lambda a: pl.BlockSpec(a.shape, const)

    def rows(a):
        if a.ndim == 2:
            return pl.BlockSpec((tm, a.shape[1]), row)
        dil = a.shape[1]
        return pl.BlockSpec((None, dil, tm // dil, a.shape[3]), lambda i: (i // per_b, 0, i % per_b, 0))

    return pl.pallas_call(
        _mix_kernel,
        grid=(t // tm,),
        in_specs=[rows(x2), pl.BlockSpec((None, 1, d), lambda i: (i // per_b, 0, 0)),
                  rows(oc), rows(osel), rows(ow), rows(gl), full(ge),
                  rows(od[0]), rows(od[1]), rows(od[2]), rows(lse[0]), rows(lse[1]), rows(lse[2]), rows(ml),
                  full(wn), full(wd), full(wo)],
        out_specs=pl.BlockSpec((tm, d), row),
        out_shape=jax.ShapeDtypeStruct((t, d), F32),
        scratch_shapes=[pltpu.VMEM((2, tm, LANES), F32)],
        compiler_params=_cparams(("arbitrary",)),
        name="mix_outproj",
    )(x2, g1, oc, osel, ow, gl, ge, *od, *lse, ml, wn, wd, wo)


def _router_kernel(x_ref, sc_ref, sh_ref, g_ref, wr_ref, br_ref, h_ref, eid_ref, wts_ref):
    h = _norm_mod(x_ref[...], g_ref[...], sc_ref[...], sh_ref[...])
    n_sub = h.shape[1] // LANES
    for j in range(n_sub):
        h_ref[pl.ds(j, h.shape[0], stride=n_sub), :] = h[:, j * LANES:(j + 1) * LANES]
    nt = (((1,), (1,)), ((), ()))
    w = wr_ref[...]
    w_hi, h_hi = w.astype(BF16), h.astype(BF16)
    w_lo, h_lo = (w - w_hi.astype(F32)).astype(BF16), (h - h_hi.astype(F32)).astype(BF16)
    logit = (lax.dot_general(w_hi, h_hi, nt, preferred_element_type=F32)
             + lax.dot_general(w_hi, h_lo, nt, preferred_element_type=F32)
             + lax.dot_general(w_lo, h_hi, nt, preferred_element_type=F32)) + br_ref[...]
    grp = jnp.zeros((1, h.shape[0]), jnp.int32)
    best = logit[0:1]
    for k in range(1, N_EXPERT_GROUPS):
        better = logit[k:k + 1] > best
        grp = jnp.where(better, k, grp)
        best = jnp.where(better, logit[k:k + 1], best)
    den = jnp.zeros_like(best)
    for k in range(N_EXPERT_GROUPS):
        den = den + jnp.exp(logit[k:k + 1] - best)
    p_grp = 1.0 / den
    le = logit[SUBLANES:SUBLANES + EXPERTS_PER_GROUP]
    for k in range(1, N_EXPERT_GROUPS):
        lo = SUBLANES + k * EXPERTS_PER_GROUP
        le = jnp.where(grp == k, logit[lo:lo + EXPERTS_PER_GROUP], le)
    rowi = lax.broadcasted_iota(jnp.int32, le.shape, 0)
    v1 = jnp.max(le, axis=0, keepdims=True)
    i1 = jnp.min(jnp.where(le == v1, rowi, EXPERTS_PER_GROUP), axis=0, keepdims=True)
    rest = jnp.where(rowi == i1, -3e38, le)
    v2 = jnp.max(rest, axis=0, keepdims=True)
    i2 = jnp.min(jnp.where(rest == v2, rowi, EXPERTS_PER_GROUP), axis=0, keepdims=True)
    e2 = jnp.exp(v2 - v1)
    inv = p_grp / (1.0 + e2)
    eid_ref[...] = jnp.concatenate([grp * EXPERTS_PER_GROUP + i1, grp * EXPERTS_PER_GROUP + i2], axis=0)
    wts_ref[...] = jnp.concatenate([inv, e2 * inv], axis=0)


def _router(x2, sc, sh, g, wr_t, br, seq):
    t, d = x2.shape
    tm = TM_PROJ
    per_b = seq // tm
    row = lambda i: (i, 0)
    const = lambda i: (0, 0)
    return pl.pallas_call(
        _router_kernel,
        grid=(t // tm,),
        in_specs=[pl.BlockSpec((tm, d), row),
                  pl.BlockSpec((None, 1, d), lambda i: (i // per_b, 0, 0)),
                  pl.BlockSpec((None, 1, d), lambda i: (i // per_b, 0, 0)),
                  pl.BlockSpec((1, d), const),
                  pl.BlockSpec(wr_t.shape, const),
                  pl.BlockSpec(br.shape, const)],
        out_specs=[pl.BlockSpec((tm * (d // LANES), LANES), row),
                   pl.BlockSpec((2, tm), lambda i: (0, i)),
                   pl.BlockSpec((2, tm), lambda i: (0, i))],
        out_shape=[jax.ShapeDtypeStruct((t * (d // LANES), LANES), F32),
                   jax.ShapeDtypeStruct((2, t), jnp.int32),
                   jax.ShapeDtypeStruct((2, t), F32)],
        compiler_params=_cparams(("arbitrary",)),
        name="norm_router",
    )(x2, sc, sh, g, wr_t, br)


def _moe_kernel(cnt_ref, off_ref, tok_ref, wt_ref, h_ref, w1_ref, w3_ref, w2_ref, o_ref,
                xs_ref, os_ref, xb_ref, y3_ref):
    c = pl.program_id(0)
    e = pl.program_id(1)
    n_sub = xb_ref.shape[1] // LANES
    rb = xb_ref.shape[0]

    @pl.when(e == 0)
    def _():
        o_ref[...] = jnp.zeros_like(o_ref)

    n = cnt_ref[c, e]
    off = off_ref[c, e]

    def every_row(base, last, fn):
        def grp(g, _):
            slot0 = base + g * MOE_UNROLL
            row0 = pl.multiple_of(g * (MOE_UNROLL * n_sub), MOE_UNROLL * n_sub)
            for u in range(MOE_UNROLL):
                fn(jnp.minimum(slot0 + u, last), pl.ds(row0 + u * n_sub, n_sub))
            return 0

        lax.fori_loop(0, rb // MOE_UNROLL, grp, 0)

    def token_rows(slot):
        return pl.ds(pl.multiple_of(tok_ref[0, slot], n_sub), n_sub)

    def block(bi, _):
        base = off + bi * rb
        last = off + jnp.minimum(n, (bi + 1) * rb) - 1

        def gather(slot, rows):
            src = token_rows(slot)
            xs_ref[rows, :] = h_ref[src, :]
            os_ref[rows, :] = o_ref[src, :]

        every_row(base, last, gather)
        for j in range(n_sub):
            xb_ref[:, j * LANES:(j + 1) * LANES] = xs_ref[pl.ds(j, rb, stride=n_sub), :].astype(BF16)
        xb = xb_ref[...]
        a = jnp.dot(xb, w1_ref[...], preferred_element_type=F32)
        b = jnp.dot(xb, w3_ref[...], preferred_element_type=F32)
        mid = (a * (1.0 / (1.0 + jnp.exp(-a))) * b).astype(BF16)
        y = jnp.dot(mid, w2_ref[...], preferred_element_type=F32)
        for j in range(n_sub):
            y3_ref[pl.ds(j, rb, stride=n_sub), :] = y[:, j * LANES:(j + 1) * LANES]

        def scatter(slot, rows):
            o_ref[token_rows(slot), :] = os_ref[rows, :] + wt_ref[0, slot] * y3_ref[rows, :]

        every_row(base, last, scatter)
        return 0

    lax.fori_loop(0, (n + rb - 1) // rb, block, 0)


def _moe(h2, eid, wts, w1, w3, w2, layer):
    d = w1.shape[2]
    n_sub = d // LANES
    t = h2.shape[0] // n_sub
    tc = min(MOE_CHUNK, t)
    n_chunks = t // tc
    slots = 2 * tc
    tok = jnp.arange(t, dtype=jnp.int32)
    key = ((tok // tc)[None, :] * N_EXPERTS + eid).reshape(-1)
    order = jnp.argsort(key)
    tok_sorted = (jnp.tile(tok % tc * n_sub, 2)[order]).reshape(n_chunks, slots)
    w_sorted = wts.reshape(-1)[order].reshape(n_chunks, slots)
    counts = jnp.zeros((n_chunks * N_EXPERTS,), jnp.int32).at[key].add(1).reshape(n_chunks, N_EXPERTS)
    starts = jnp.cumsum(counts, axis=1) - counts
    tok_sorted = tok_sorted.reshape(n_chunks, 1, slots)
    w_sorted = w_sorted.reshape(n_chunks, 1, slots)

    grid_spec = pltpu.PrefetchScalarGridSpec(
        num_scalar_prefetch=2,
        grid=(n_chunks, N_EXPERTS),
        in_specs=[pl.BlockSpec((None, 1, slots), lambda c, e, *_: (c, 0, 0), memory_space=pltpu.SMEM),
                  pl.BlockSpec((None, 1, slots), lambda c, e, *_: (c, 0, 0), memory_space=pltpu.SMEM),
                  pl.BlockSpec((tc * n_sub, LANES), lambda c, e, *_: (c, 0), pipeline_mode=pl.Buffered(1)),
                  pl.BlockSpec((None, None, d, D_EXPERT), lambda c, e, *_: (layer, e, 0, 0)),
                  pl.BlockSpec((None, None, d, D_EXPERT), lambda c, e, *_: (layer, e, 0, 0)),
                  pl.BlockSpec((None, None, D_EXPERT, d), lambda c, e, *_: (layer, e, 0, 0))],
        out_specs=pl.BlockSpec((tc * n_sub, LANES), lambda c, e, *_: (c, 0), pipeline_mode=pl.Buffered(1)),
        scratch_shapes=[pltpu.VMEM((MOE_ROWS * n_sub, LANES), F32),
                        pltpu.VMEM((MOE_ROWS * n_sub, LANES), F32),
                        pltpu.VMEM((MOE_ROWS, d), BF16),
                        pltpu.VMEM((MOE_ROWS * n_sub, LANES), F32)],
    )
    out = pl.pallas_call(
        _moe_kernel,
        grid_spec=grid_spec,
        out_shape=jax.ShapeDtypeStruct((t * n_sub, LANES), F32),
        compiler_params=_cparams(("arbitrary", "arbitrary")),
        name="moe_experts",
    )(counts, starts, tok_sorted, w_sorted, h2, w1, w3, w2)
    return out


def _resid_kernel(x_ref, y_ref, g_ref, nf_ref, o_ref, *, final):
    tm, d = x_ref.shape
    n_sub = d // LANES
    y = jnp.concatenate([y_ref[pl.ds(j, tm, stride=n_sub), :] for j in range(n_sub)], axis=1)
    x = x_ref[...] + g_ref[...] * y
    if final:
        ms = jnp.mean(x * x, axis=-1, keepdims=True)
        x = x * lax.rsqrt(ms + RMS_EPS) * nf_ref[...]
    o_ref[...] = x


def _residual(x2, y2, g2, norm_f, seq, final):
    t, d = x2.shape
    tm = TM_PROJ
    per_b = seq // tm
    row = lambda i: (i, 0)
    return pl.pallas_call(
        functools.partial(_resid_kernel, final=final),
        grid=(t // tm,),
        in_specs=[pl.BlockSpec((tm, d), row), pl.BlockSpec((tm * (d // LANES), LANES), row),
                  pl.BlockSpec((None, 1, d), lambda i: (i // per_b, 0, 0)),
                  pl.BlockSpec((1, d), lambda i: (0, 0))],
        out_specs=pl.BlockSpec((tm, d), row),
        out_shape=jax.ShapeDtypeStruct((t, d), F32),
        compiler_params=_cparams(("arbitrary",)),
        name="residual_final" if final else "residual",
    )(x2, y2, g2, norm_f)


def _split_w_in(w_in, d):
    scale = HEAD_DIM ** -0.5 * LOG2E
    nq = NSA_HEADS * HEAD_DIM
    nkv = 3 * 2 * NSA_KV_HEADS * HEAD_DIM
    ngate = 3 * NSA_HEADS
    ndil = 3 * N_DIL_GROUPS * DIL_HEADS_PER_GROUP * HEAD_DIM
    o1, o2, o3 = nq, nq + nkv, nq + nkv + ngate
    o4 = o3 + ndil
    wq = (w_in[:, :o1] * scale).reshape(d, NSA_KV_HEADS, NSA_GROUP, HEAD_DIM)
    wq = wq.transpose(0, 2, 1, 3).reshape(d, nq)
    wkv = w_in[:, o1:o2]
    wg = jnp.pad(w_in[:, o2:o3], ((0, 0), (0, LANES - ngate)))
    gw = DIL_HEADS_PER_GROUP * HEAD_DIM
    per_which = N_DIL_GROUPS * gw
    wds = []
    for grp in range(N_DIL_GROUPS):
        parts = [w_in[:, o3 + which * per_which + grp * gw: o3 + which * per_which + (grp + 1) * gw]
                 for which in range(3)]
        parts[0] = parts[0] * scale
        wds.append(jnp.concatenate(parts, axis=1))
    wm = w_in[:, o4:]
    cast = lambda w: w.astype(BF16)
    return [cast(wq), cast(wkv), cast(wds[0]), cast(wds[1]), cast(wds[2]), cast(wm), cast(wg)]


def kernel(x, c, rel_bias, ada_w, ada_b, norm1, norm2, w_in, cmp_pos, cmp_w1, cmp_w2, w_up_nsa, w_up_dil, w_o,
           router_wg, router_bg, router_we, router_be, exp_w1, exp_w3, exp_w2, norm_f):
    batch, seq, d = x.shape
    depth = ada_w.shape[0]
    t = batch * seq
    n_cmp = seq // CMP_STRIDE
    n_sel = seq // SEL_BLOCK
    assert seq % TQ_NSA == 0 and n_sel <= HEAD_DIM and n_sel >= SEL_TOPN
    assert all(seq % (dil * TQ_DIL) == 0 for _, dil in DIL_PAIRS)

    mod = _modulation(c, ada_w, ada_b)

    nq_nsa = seq // TQ_NSA
    bias_cmp = _expand_bias(rel_bias, _cmp_buckets(seq, n_cmp), 0, NSA_HEADS, 8 * SUBLANES)
    sel_b = _toeplitz_buckets(nq_nsa, TQ_NSA, seq, 1)
    n_sel_bias = nq_nsa
    while n_sel_bias > 1 and (sel_b[n_sel_bias - 2:] == sel_b[n_sel_bias - 1, 0, 0]).all():
        n_sel_bias -= 1
    sel_b = sel_b[:n_sel_bias].reshape(n_sel_bias * TQ_NSA, TQ_NSA)
    bias_sel = _expand_bias(rel_bias, sel_b, 0, NSA_HEADS, TQ_NSA).reshape(NSA_HEADS, n_sel_bias, TQ_NSA, TQ_NSA)
    nb_win = min(-(-NSA_WINDOW // TQ_NSA), nq_nsa - 1)
    win_b = _toeplitz_buckets(nb_win + 1, TQ_NSA, NSA_WINDOW, 1, masked_tail=True).reshape(-1, TQ_NSA)
    bias_win = _expand_bias(rel_bias, win_b, 0, NSA_HEADS, TQ_NSA).reshape(NSA_HEADS, nb_win + 2, TQ_NSA, TQ_NSA)
    bias_dil, nb_dil = [], []
    for grp, (window, dil) in enumerate(DIL_PAIRS):
        nb = min(-(-(window // dil) // TQ_DIL), seq // dil // TQ_DIL - 1)
        bk = _toeplitz_buckets(nb + 1, TQ_DIL, window // dil, dil, masked_tail=True).reshape(-1, TQ_DIL)
        hb = NSA_HEADS + grp * DIL_HEADS_PER_GROUP
        bias_dil.append(_expand_bias(rel_bias, bk, hb, DIL_HEADS_PER_GROUP, TQ_DIL)
                        .reshape(DIL_HEADS_PER_GROUP, nb + 2, TQ_DIL, TQ_DIL))
        nb_dil.append(nb)
    ovl_t = jnp.asarray(_overlap_t(n_cmp, n_sel), BF16)
    onehot = jnp.asarray(_block_onehot(seq), BF16)
    gate_e = jnp.asarray(_gate_expand(), BF16)

    expert_w = [w.astype(BF16) for w in (exp_w1, exp_w3, exp_w2)]
    x2 = x.reshape(t, d)
    for l in range(depth):
        sh1, sc1, g1, sh2, sc2, g2 = [m.reshape(batch, 1, d) for m in jnp.split(mod[l], 6, axis=-1)]
        weights = _split_w_in(w_in[l], d)
        (q_n, kc_in, vc_in, k_sel, v_sel, k_win, v_win, qkv_d0, qkv_d1, qkv_d2, merge_l, gate_l) = _inproj(
            x2, sc1, sh1, norm1[l].reshape(1, d), onehot, weights, seq)

        kc = _compress(kc_in, cmp_pos[l, 0], cmp_w1[l, 0], cmp_w2[l, 0], batch, seq, transpose_out=False)
        vct = _compress(vc_in, cmp_pos[l, 1], cmp_w1[l, 1], cmp_w2[l, 1], batch, seq, transpose_out=True)
        o_c, selb = _cmp_select(q_n, kc, vct, bias_cmp, ovl_t, batch, seq)
        nsa_common = dict(batch=batch, length=seq, dil=1, tile=TQ_NSA, n_pairs=NSA_GROUP, q_blk=lambda r: 0,
                          k_blk=lambda r: 0, v_blk=lambda r: 0, v_width=LANES, per_pair_kv=False,
                          heads=tuple((g, NSA_GROUP + g) for g in range(NSA_GROUP)))
        q3 = q_n.reshape(batch, seq, 512)
        (o_s,) = _flash(q3, k_sel.reshape(batch, seq, 2 * LANES), v_sel.reshape(batch, seq, LANES), bias_sel,
                        n_back=None, k_width=2 * LANES, selb=selb.reshape(batch, seq, 2 * LANES), **nsa_common)
        (o_w,) = _flash(q3, k_win.reshape(batch, seq, LANES), v_win.reshape(batch, seq, LANES), bias_win,
                        n_back=nb_win, k_width=LANES, q_tiles=WIN_Q_TILES, **nsa_common)

        o_d, lse_d = [], []
        for grp, ((window, dil), qkv) in enumerate(zip(DIL_PAIRS, (qkv_d0, qkv_d1, qkv_d2))):
            length = seq // dil
            view = qkv.reshape(batch * dil, length, 768)
            o_g, lse_g = _flash(view, view, view, bias_dil[grp], batch=batch * dil, length=length, dil=1,
                                tile=TQ_DIL, n_back=nb_dil[grp], n_pairs=2, q_blk=lambda r: 0, k_blk=lambda r: 1,
                                k_width=2 * LANES, v_blk=lambda r: 2, v_width=2 * LANES,
                                per_pair_kv=True, heads=((0, 1), (2, 3)), with_lse=True,
                                q_tiles=math.gcd(DIL_Q_TILES, length // TQ_DIL))
            shape = (t, 256) if dil == 1 else (batch, dil, length, 256)
            o_d.append(o_g.reshape(shape))
            lse_d.append(lse_g.reshape(shape))

        wn = w_up_nsa[l].reshape(NSA_KV_HEADS, NSA_GROUP, HEAD_DIM, d).transpose(1, 0, 2, 3).reshape(512, d)
        x2 = _mix(x2, g1, o_c, o_s.reshape(t, 512), o_w.reshape(t, 512), gate_l, gate_e, o_d, lse_d, merge_l,
                  wn.astype(BF16), w_up_dil[l].astype(BF16), w_o[l].astype(BF16), seq)

        wr_t = jnp.concatenate([jnp.pad(router_wg[l], ((0, 0), (0, SUBLANES - N_EXPERT_GROUPS))),
                                router_we[l]], axis=1).T
        br = jnp.concatenate([jnp.pad(router_bg[l], (0, SUBLANES - N_EXPERT_GROUPS)),
                              router_be[l]]).reshape(-1, 1)
        h2, eid, wts = _router(x2, sc2, sh2, norm2[l].reshape(1, d), wr_t, br, seq)
        y = _moe(h2, eid, wts, *expert_w, layer=l)
        x2 = _residual(x2, y, g2, norm_f.reshape(1, d), seq, final=(l == depth - 1))
    return x2.reshape(batch, seq, d)
```

```python
import functools
import math

import numpy as np
import jax
import jax.numpy as jnp
from jax import lax
from jax.experimental import pallas as pl
from jax.experimental.pallas import tpu as pltpu

F32 = jnp.float32
BF16 = jnp.bfloat16

HEAD_DIM = 64
NSA_HEADS = 8
NSA_KV_HEADS = 2
NSA_GROUP = NSA_HEADS // NSA_KV_HEADS
CMP_LEN = 32
CMP_STRIDE = 16
CMP_HIDDEN = 256
SEL_BLOCK = 64
SEL_TOPN = 16
NSA_WINDOW = 512
FORCE_SCORE = 1e4
DIL_PAIRS = ((128, 1), (512, 4), (2048, 16))
N_DIL_GROUPS = 3
DIL_HEADS_PER_GROUP = 4
NUM_BUCKETS = 32
REL_MAX_DIST = 2048
N_EXPERT_GROUPS = 4
EXPERTS_PER_GROUP = 8
N_EXPERTS = N_EXPERT_GROUPS * EXPERTS_PER_GROUP
D_EXPERT = 512
RMS_EPS = 1e-6

LOG2E = math.log2(math.e)
LANES = 128
SUBLANES = 8
MASK_NEG = -1e30
SEL_NEG = -1e9
TQ_NSA = 256
TQ_DIL = 128
TM_PROJ = 512
WIN_Q_TILES = 2
DIL_Q_TILES = 4
FLASH_LOOKAHEAD = 4
MOE_CHUNK = 4096
MOE_ROWS = 128
MOE_UNROLL = 16
VMEM_LIMIT = 56 * 1024 * 1024


def _cparams(sem):
    return pltpu.CompilerParams(dimension_semantics=sem, vmem_limit_bytes=VMEM_LIMIT)


def _np_bucket(dist):
    dist = np.maximum(dist, 0)
    max_exact = NUM_BUCKETS // 2
    df = np.maximum(dist, 1).astype(np.float32)
    val = np.log(df / np.float32(max_exact)) / np.float32(math.log(REL_MAX_DIST / max_exact))
    large = max_exact + (val * np.float32(NUM_BUCKETS - max_exact)).astype(np.int32)
    large = np.minimum(large, NUM_BUCKETS - 1)
    return np.where(dist < max_exact, dist, large).astype(np.int32)


def _toeplitz_buckets(n_delta, tile, window, dist_scale, masked_tail=False):
    dd = np.arange(n_delta)[:, None, None]
    r = np.arange(tile)[None, :, None]
    c = np.arange(tile)[None, None, :]
    dist = dd * tile + c - r
    valid = (dist >= 0) & (dist <= window)
    out = np.where(valid, _np_bucket(dist * dist_scale), -1).astype(np.int32)
    if masked_tail:
        out = np.concatenate([out, np.full((1, tile, tile), -1, np.int32)], axis=0)
    return out


def _cmp_buckets(seq, n_cmp):
    c_end = np.arange(n_cmp)[:, None] * CMP_STRIDE + CMP_LEN - 1
    t = np.arange(seq)[None, :]
    dist = t - c_end
    return np.where(dist >= 0, _np_bucket(dist), -1).astype(np.int32)


def _overlap_t(n_cmp, n_sel):
    c_start = np.arange(n_cmp)[None, :] * CMP_STRIDE
    s_start = np.arange(n_sel)[:, None] * SEL_BLOCK
    ov = np.clip(np.minimum(c_start + CMP_LEN, s_start + SEL_BLOCK) - np.maximum(c_start, s_start), 0, None)
    out = np.zeros((LANES, n_cmp), np.float32)
    out[64:64 + n_sel] = ov.astype(np.float32) / CMP_STRIDE
    return out


def _block_onehot(seq):
    oh = np.zeros((seq, LANES), np.float32)
    blk = np.arange(seq) // SEL_BLOCK
    oh[np.arange(seq), blk] = 1.0
    oh[np.arange(seq), 64 + blk] = 1.0
    return oh


def _gate_expand():
    e = np.zeros((LANES, 3 * NSA_HEADS * HEAD_DIM), np.float32)
    for br in range(3):
        for g in range(NSA_GROUP):
            for ln in range(LANES):
                kv = ln // HEAD_DIM
                e[br * NSA_HEADS + kv * NSA_GROUP + g, br * 512 + g * LANES + ln] = 1.0
    return e


def _mod_kernel(c_ref, w_ref, b_ref, o_ref):
    c = c_ref[...]
    cond = c * (1.0 / (1.0 + jnp.exp(-c)))
    o_ref[...] = jnp.dot(cond, w_ref[...], preferred_element_type=F32,
                         precision=lax.Precision.HIGHEST) + b_ref[...]


def _modulation(c, ada_w, ada_b):
    depth, d, n = ada_w.shape
    b = c.shape[0]
    tn = 1536
    return pl.pallas_call(
        _mod_kernel,
        grid=(depth, n // tn),
        in_specs=[pl.BlockSpec((b, d), lambda l, j: (0, 0)),
                  pl.BlockSpec((None, d, tn), lambda l, j: (l, 0, j)),
                  pl.BlockSpec((None, 1, tn), lambda l, j: (l, 0, j))],
        out_specs=pl.BlockSpec((None, b, tn), lambda l, j: (l, 0, j)),
        out_shape=jax.ShapeDtypeStruct((depth, b, n), F32),
        compiler_params=_cparams(("arbitrary", "arbitrary")),
        name="adaln_modulation",
    )(c, ada_w, ada_b.reshape(depth, 1, n))


def _bias_kernel(tbl_ref, bkt_ref, o_ref, *, head_base):
    h = pl.program_id(0) + head_base
    bkt = bkt_ref[...]
    acc = jnp.full(bkt.shape, MASK_NEG, F32)
    for b in range(NUM_BUCKETS):
        acc = jnp.where(bkt == b, tbl_ref[b, h] * LOG2E, acc)
    o_ref[...] = acc


def _expand_bias(rel_bias, buckets, head_base, n_heads, row_tile):
    rows, cols = buckets.shape
    return pl.pallas_call(
        functools.partial(_bias_kernel, head_base=head_base),
        grid=(n_heads, rows // row_tile),
        in_specs=[pl.BlockSpec(memory_space=pltpu.SMEM),
                  pl.BlockSpec((row_tile, cols), lambda h, i: (i, 0))],
        out_specs=pl.BlockSpec((None, row_tile, cols), lambda h, i: (h, i, 0)),
        out_shape=jax.ShapeDtypeStruct((n_heads, rows, cols), F32),
        compiler_params=_cparams(("arbitrary", "arbitrary")),
        name="bias_expand",
    )(rel_bias, jnp.asarray(buckets))


def _norm_mod(x, g, sc, sh):
    ms = jnp.mean(x * x, axis=-1, keepdims=True)
    y = x * lax.rsqrt(ms + RMS_EPS) * g
    return y * (1.0 + sc) + sh


def _tile_rows(y_ref, tm, d):
    n_sub = d // LANES
    return jnp.concatenate([y_ref[pl.ds(j, tm, stride=n_sub), :] for j in range(n_sub)], axis=1)


def _inproj_kernel(*refs, fused):
    it = iter(refs)
    x_ref = next(it)
    y_ref, g2_ref = (next(it), next(it)) if fused else (None, None)
    sc_ref, sh_ref, g_ref, oh_ref, wq_ref, wkv_ref, wd0_ref, wd1_ref, wd2_ref, wm_ref, wg_ref = (
        next(it) for _ in range(11))
    q_ref, kc_ref, vc_ref, ks_ref, vs_ref, kw_ref, vw_ref, d0_ref, d1_ref, d2_ref, m_ref, gl_ref = (
        next(it) for _ in range(12))
    xo_ref = next(it) if fused else None
    slab_ref = next(it)
    x = x_ref[...]
    tm = x.shape[0]
    if fused:
        x = x + g2_ref[...] * _tile_rows(y_ref, tm, x.shape[1])
        xo_ref[...] = x
    h = _norm_mod(x, g_ref[...], sc_ref[...], sh_ref[...]).astype(BF16)

    def by_residue(w_ref, o_ref):
        dil = o_ref.shape[0]
        res = jnp.dot(h, w_ref[...], preferred_element_type=F32)
        n_slab = res.shape[1] // LANES
        for c in range(n_slab):
            slab_ref[c] = res[:, c * LANES:(c + 1) * LANES]
        for r in range(dil):
            for c in range(n_slab):
                o_ref[r, :, c * LANES:(c + 1) * LANES] = slab_ref[c, pl.ds(r, tm // dil, stride=dil), :].astype(BF16)

    def proj(w_ref):
        return jnp.dot(h, w_ref[...], preferred_element_type=F32)

    q_ref[...] = proj(wq_ref).astype(BF16)
    kv = proj(wkv_ref).astype(BF16)
    for k, ref in ((0, kc_ref), (1, vc_ref), (3, vs_ref), (4, kw_ref), (5, vw_ref)):
        ref[...] = kv[:, k * LANES:(k + 1) * LANES]
    k_sel = kv[:, 2 * LANES:3 * LANES]
    oh = oh_ref[...]
    lo = lax.broadcasted_iota(jnp.int32, k_sel.shape, 1) < HEAD_DIM
    ks_ref[:, :LANES] = jnp.where(lo, k_sel, oh)
    ks_ref[:, LANES:] = jnp.where(lo, oh, k_sel)
    d0_ref[...] = proj(wd0_ref).astype(BF16)
    by_residue(wd1_ref, d1_ref)
    by_residue(wd2_ref, d2_ref)
    m_ref[...] = proj(wm_ref).astype(BF16)
    gl_ref[...] = proj(wg_ref)


def _inproj(x2, sc, sh, g, onehot, weights, seq, pending=None):
    t, d = x2.shape
    tm = TM_PROJ
    per_b = seq // tm
    widths = (512, 128, 128, 256, 128, 128, 128, 768, 768, 768, 2 * d, LANES)
    dtypes = (BF16,) * 11 + (F32,)
    row = lambda i: (i, 0)
    const = lambda i: (0, 0)
    per_batch = pl.BlockSpec((None, 1, d), lambda i: (i // per_b, 0, 0))
    in_specs = [pl.BlockSpec((tm, d), row)]
    args = [x2]
    if pending is not None:
        in_specs += [pl.BlockSpec((tm * (d // LANES), LANES), row), per_batch]
        args += list(pending)
    in_specs += [per_batch, per_batch, pl.BlockSpec((1, d), const),
                 pl.BlockSpec((tm, LANES), lambda i: (i % per_b, 0))]
    in_specs += [pl.BlockSpec(w.shape, const) for w in weights]
    args += [sc, sh, g, onehot, *weights]
    out_specs = [pl.BlockSpec((tm, w), row) for w in widths]
    out_shape = [jax.ShapeDtypeStruct((t, w), dt) for w, dt in zip(widths, dtypes)]
    for k, (_, dil) in zip((8, 9), DIL_PAIRS[1:]):
        out_specs[k] = pl.BlockSpec((None, dil, tm // dil, 768), lambda i: (i // per_b, 0, i % per_b, 0))
        out_shape[k] = jax.ShapeDtypeStruct((t // seq, dil, seq // dil, 768), BF16)
    if pending is not None:
        out_specs.append(pl.BlockSpec((tm, d), row))
        out_shape.append(jax.ShapeDtypeStruct((t, d), F32))
    return pl.pallas_call(
        functools.partial(_inproj_kernel, fused=pending is not None),
        grid=(t // tm,),
        in_specs=in_specs,
        out_specs=out_specs,
        out_shape=out_shape,
        scratch_shapes=[pltpu.VMEM((768 // LANES, tm, LANES), F32)],
        compiler_params=_cparams(("arbitrary",)),
        name="norm_inproj",
    )(*args)


def _gelu_tanh(x):
    return 0.5 * x * (1.0 + jnp.tanh(math.sqrt(2.0 / math.pi) * (x + 0.044715 * (x * x * x))))


def _compress_kernel(r_ref, pt_ref, pb_ref, wt_ref, wb_ref, w2_ref, o_ref, *, transpose_out):
    r = r_ref[...].astype(F32)
    top = jnp.dot((r + pt_ref[...]).astype(BF16), wt_ref[...], preferred_element_type=F32)
    bot = jnp.dot((r + pb_ref[...]).astype(BF16), wb_ref[...], preferred_element_type=F32)
    n = bot.shape[0]
    hid = top + pltpu.roll(bot, n - 1, 0)
    act = _gelu_tanh(hid).astype(BF16)
    if transpose_out:
        o_ref[...] = lax.dot_general(w2_ref[...], act, (((1,), (1,)), ((), ())),
                                     preferred_element_type=F32).astype(BF16)
    else:
        o_ref[...] = jnp.dot(act, w2_ref[...], preferred_element_type=F32).astype(BF16)


def _compress(tok, pos, w1, w2, batch, seq, transpose_out):
    nc = seq // CMP_STRIDE
    half = CMP_LEN // 2
    eye = jnp.eye(NSA_KV_HEADS, dtype=F32)
    w1r = w1.reshape(CMP_LEN, HEAD_DIM, CMP_HIDDEN)
    blk = lambda w: jnp.einsum('ldn,hg->lhdgn', w, eye).reshape(half * LANES, 2 * CMP_HIDDEN).astype(BF16)
    wt, wb = blk(w1r[:half]), blk(w1r[half:])
    posr = lambda p: jnp.broadcast_to(p[:, None, :], (half, NSA_KV_HEADS, HEAD_DIM)).reshape(1, half * LANES)
    pt, pb = posr(pos[:half]), posr(pos[half:])
    w2b = jnp.einsum('nd,hg->hngd', w2, eye).reshape(2 * CMP_HIDDEN, LANES)
    if transpose_out:
        w2b = w2b.T
        out_block, out_shape = (None, LANES, nc), (batch, LANES, nc)
    else:
        out_block, out_shape = (None, nc, LANES), (batch, nc, LANES)
    w2b = w2b.astype(BF16)
    const = lambda b: (0, 0)
    return pl.pallas_call(
        functools.partial(_compress_kernel, transpose_out=transpose_out),
        grid=(batch,),
        in_specs=[pl.BlockSpec((None, nc, half * LANES), lambda b: (b, 0, 0)),
                  pl.BlockSpec(pt.shape, const), pl.BlockSpec(pb.shape, const),
                  pl.BlockSpec(wt.shape, const), pl.BlockSpec(wb.shape, const),
                  pl.BlockSpec(w2b.shape, const)],
        out_specs=pl.BlockSpec(out_block, lambda b: (b, 0, 0)),
        out_shape=jax.ShapeDtypeStruct(out_shape, BF16),
        compiler_params=_cparams(("arbitrary",)),
        name="nsa_compress",
    )(tok.reshape(batch, nc, half * LANES), pt, pb, wt, wb, w2b)


def _cmp_select_kernel(q_ref, kc_ref, vct_ref, bias_ref, ovl_ref, o_ref, selb_ref, *, n_sel, n_top):
    tq = q_ref.shape[0]
    nc = kc_ref.shape[0]
    qs = pl.program_id(1) * tq
    kc = kc_ref[...]
    vct = vct_ref[...]
    lane_q = lax.broadcasted_iota(jnp.int32, (tq, LANES), 1)
    row_o = lax.broadcasted_iota(jnp.int32, (LANES, tq), 0)
    heads = [(g, kv) for g in range(NSA_GROUP) for kv in range(NSA_KV_HEADS)]

    def scores(g, kv):
        qt = q_ref[:, g * LANES:(g + 1) * LANES]
        mine = (lane_q < HEAD_DIM) if kv == 0 else (lane_q >= HEAD_DIM)
        qm = jnp.where(mine, qt, jnp.zeros_like(qt))
        s = lax.dot_general(kc, qm, (((1,), (1,)), ((), ())), preferred_element_type=F32)
        return s + bias_ref[kv * NSA_GROUP + g]

    pc_sum = [jnp.zeros((nc, tq), F32) for _ in range(NSA_KV_HEADS)]
    outs = {}
    ahead = [scores(*hd) for hd in heads[:FLASH_LOOKAHEAD]]
    for n, (g, kv) in enumerate(heads):
        s = ahead.pop(0)
        if n + FLASH_LOOKAHEAD < len(heads):
            ahead.append(scores(*heads[n + FLASH_LOOKAHEAD]))
        m = jnp.max(s, axis=0, keepdims=True)
        m = jnp.where(m < 0.5 * MASK_NEG, 0.0, m)
        p = jnp.exp2(s - m)
        den = jnp.sum(p, axis=0, keepdims=True)
        pc = p * (1.0 / jnp.where(den > 0.0, den, 1.0))
        pc_sum[kv] = pc_sum[kv] + pc
        outs[kv] = jnp.dot(vct, pc.astype(BF16), preferred_element_type=F32)
        if kv == NSA_KV_HEADS - 1:
            o_t = jnp.where(row_o < HEAD_DIM, outs[0], outs[1])
            o_ref[:, g * LANES:(g + 1) * LANES] = o_t.T.astype(BF16)

    ovl = ovl_ref[...]
    imps = []
    for kv in range(NSA_KV_HEADS):
        hi = pc_sum[kv].astype(BF16)
        lo = (pc_sum[kv] - hi.astype(F32)).astype(BF16)
        imps.append((jnp.dot(ovl, hi, preferred_element_type=F32)
                     + jnp.dot(ovl, lo, preferred_element_type=F32))[HEAD_DIM:])
    imp = jnp.concatenate(imps, axis=1)
    wide = (HEAD_DIM, NSA_KV_HEADS * tq)
    rowj = lax.broadcasted_iota(jnp.int32, wide, 0)
    col = lax.broadcasted_iota(jnp.int32, wide, 1)
    t = qs + jnp.where(col >= tq, col - tq, col)
    jq = jnp.right_shift(t, SEL_BLOCK.bit_length() - 1)
    forced = (rowj == 0) | (rowj == jq) | (rowj == jq - 1)
    score = jnp.where(rowj > jq, -1.0, imp)
    rem = jnp.where(forced | (rowj >= n_sel), -3e38, score)
    sel = jnp.where(forced, 1.0, 0.0)
    for _ in range(n_top - 3):
        m = jnp.max(rem, axis=0, keepdims=True)
        idx = jnp.min(jnp.where(rem == m, rowj, HEAD_DIM), axis=0, keepdims=True)
        pick = rowj == idx
        sel = jnp.where(pick, 1.0, sel)
        rem = jnp.where(pick, -3e38, rem)
    sb = jnp.where(sel > 0.5, 0.0, SEL_NEG)
    zero = jnp.zeros((HEAD_DIM, tq), F32)
    selb_ref[:, :LANES] = jnp.concatenate([zero, sb[:, :tq]], axis=0).T.astype(BF16)
    selb_ref[:, LANES:] = jnp.concatenate([sb[:, tq:], zero], axis=0).T.astype(BF16)


def _cmp_select(q, kc, vct, bias_c, ovl_t, batch, seq):
    t = q.shape[0]
    tq = TQ_NSA
    nq = seq // tq
    nc = seq // CMP_STRIDE
    n_sel = seq // SEL_BLOCK
    n_top = min(SEL_TOPN, n_sel)
    return pl.pallas_call(
        functools.partial(_cmp_select_kernel, n_sel=n_sel, n_top=n_top),
        grid=(batch, nq),
        in_specs=[pl.BlockSpec((tq, 512), lambda b, i: (b * nq + i, 0)),
                  pl.BlockSpec((None, nc, LANES), lambda b, i: (b, 0, 0)),
                  pl.BlockSpec((None, LANES, nc), lambda b, i: (b, 0, 0)),
                  pl.BlockSpec((NSA_HEADS, nc, tq), lambda b, i: (0, 0, i)),
                  pl.BlockSpec((LANES, nc), lambda b, i: (0, 0))],
        out_specs=[pl.BlockSpec((tq, 512), lambda b, i: (b * nq + i, 0)),
                   pl.BlockSpec((tq, 2 * LANES), lambda b, i: (b * nq + i, 0))],
        out_shape=[jax.ShapeDtypeStruct((t, 512), BF16),
                   jax.ShapeDtypeStruct((t, 2 * LANES), BF16)],
        compiler_params=_cparams(("arbitrary", "arbitrary")),
        name="nsa_cmp_select",
    )(q, kc, vct, bias_c, ovl_t)


def _flash_kernel(*refs, tile, q_tiles, n_pairs, heads, per_pair_kv, n_back, n_bias, with_sel, with_lse):
    it = iter(refs)
    q_ref, k_ref, v_ref, b_ref = (next(it) for _ in range(4))
    selb_ref = next(it) if with_sel else None
    o_ref = next(it)
    lse_ref = next(it) if with_lse else None
    qa_ref = next(it)
    acc_ref = next(it) if n_back is None else None

    i = pl.program_id(1)
    n_heads = 2 * n_pairs
    lo = lax.broadcasted_iota(jnp.int32, (tile, LANES), 1) < HEAD_DIM
    tr = lambda a: a.astype(F32).T.astype(BF16)
    for u in range(q_tiles):
        rows = slice(u * tile, (u + 1) * tile)
        for p in range(n_pairs):
            q = q_ref[rows, p * LANES:(p + 1) * LANES]
            if with_sel:
                qa_ref[u * n_heads + 2 * p] = tr(jnp.where(lo, q, selb_ref[rows, :LANES]))
                qa_ref[u * n_heads + 2 * p + 1] = tr(jnp.where(lo, selb_ref[rows, LANES:], q))
            else:
                zero = jnp.zeros_like(q)
                qa_ref[u * n_heads + 2 * p] = tr(jnp.where(lo, q, zero))
                qa_ref[u * n_heads + 2 * p + 1] = tr(jnp.where(lo, zero, q))
    nt = (((1,), (1,)), ((), ()))
    tn = (((0,), (0,)), ((), ()))
    top = lax.broadcasted_iota(jnp.int32, (LANES, tile), 0) < HEAD_DIM

    def finish(u, p, acc0, l0, m0, acc1, l1, m1):
        rows = slice(u * tile, (u + 1) * tile)
        o_t = jnp.where(top, acc0 * (1.0 / l0), acc1 * (1.0 / l1))
        o_ref[rows, p * LANES:(p + 1) * LANES] = o_t.T.astype(o_ref.dtype)
        if with_lse:
            lse_t = jnp.where(top, m0 + jnp.log2(l0), m1 + jnp.log2(l1))
            lse_ref[rows, p * LANES:(p + 1) * LANES] = lse_t.T

    def scores(unit, tiles):
        hd = unit % n_heads
        p, half = divmod(hd, 2)
        kc = half if with_sel else (p if per_pair_kv else 0)
        parts = []
        for j, bidx in tiles:
            kt = k_ref[pl.ds(pl.multiple_of(j * tile, tile), tile), kc * LANES:(kc + 1) * LANES]
            s = jnp.dot(kt, qa_ref[unit], preferred_element_type=F32)
            parts.append(s + b_ref[heads[p][half], bidx])
        return parts

    def group(unit_tiles, state):
        res = []
        n_units = len(unit_tiles)
        ahead = [scores(n, unit_tiles[n]) for n in range(min(FLASH_LOOKAHEAD, n_units))]
        for hd in range(n_units):
            tiles = unit_tiles[hd]
            parts = ahead.pop(0)
            if hd + FLASH_LOOKAHEAD < n_units:
                ahead.append(scores(hd + FLASH_LOOKAHEAD, unit_tiles[hd + FLASH_LOOKAHEAD]))
            m_new = functools.reduce(jnp.maximum, [jnp.max(s, axis=0, keepdims=True) for s in parts])
            if state is not None:
                m, l = state[hd]
                m_new = jnp.maximum(m, m_new)
                alpha = jnp.exp2(m - m_new)
            prs = [jnp.exp2(s - m_new) for s in parts]
            l_new = functools.reduce(jnp.add, [jnp.sum(pr, axis=0, keepdims=True) for pr in prs])
            vc = (hd % n_heads // 2) if per_pair_kv else 0
            pv = None
            for (j, _), pr in zip(tiles, prs):
                vt = v_ref[pl.ds(pl.multiple_of(j * tile, tile), tile), vc * LANES:(vc + 1) * LANES]
                d = lax.dot_general(vt, pr.astype(BF16), tn, preferred_element_type=F32)
                pv = d if pv is None else pv + d
            if state is not None:
                l_new = alpha * l + l_new
                pv = alpha * acc_ref[hd] + pv
            res.append((m_new, l_new, pv))
        return res

    if n_back is not None:
        unit_tiles = []
        for u in range(q_tiles):
            iu = i * q_tiles + u
            tiles = [(jnp.maximum(iu - k, 0), jnp.where(iu >= k, k, n_bias - 1)) for k in range(n_back + 1)]
            unit_tiles += [tiles] * n_heads
        res = group(unit_tiles, None)
        for u in range(q_tiles):
            for p in range(n_pairs):
                (m0, l0, a0), (m1, l1, a1) = res[u * n_heads + 2 * p], res[u * n_heads + 2 * p + 1]
                finish(u, p, a0, l0, m0, a1, l1, m1)
        return

    def store(res):
        for hd, (_, _, acc) in enumerate(res):
            acc_ref[hd] = acc
        return tuple(x for m, l, _ in res for x in (m, l))

    def unpack(carry):
        return [(carry[2 * hd], carry[2 * hd + 1]) for hd in range(n_heads)]

    bias_of = lambda back: jnp.minimum(back, n_bias - 1)
    carry = store(group([[(i, 0)]] * n_heads, None))

    def pair_body(step, carry):
        back = 2 * step + 1
        tiles = [(i - back, bias_of(back)), (i - back - 1, bias_of(back + 1))]
        return store(group([tiles] * n_heads, unpack(carry)))

    carry = lax.fori_loop(0, i // 2, pair_body, carry)

    def last_body(_, carry):
        return store(group([[(0, bias_of(i))]] * n_heads, unpack(carry)))

    carry = lax.fori_loop(0, i % 2, last_body, carry)
    fin = unpack(carry)
    for p in range(n_pairs):
        finish(0, p, acc_ref[2 * p], fin[2 * p][1], fin[2 * p][0], acc_ref[2 * p + 1], fin[2 * p + 1][1], fin[2 * p + 1][0])


def _flash(q_arr, k_arr, v_arr, bias, *, batch, length, dil, tile, n_back, n_pairs, q_blk, k_blk, k_width,
           v_blk, v_width, per_pair_kv, heads, selb=None, with_lse=False, q_tiles=1):
    nq = length // tile
    with_sel = selb is not None
    width = n_pairs * LANES

    in_specs = [pl.BlockSpec((None, q_tiles * tile, width), lambda br, i: (br // dil, i, q_blk(br % dil))),
                pl.BlockSpec((None, length, k_width), lambda br, i: (br // dil, 0, k_blk(br % dil))),
                pl.BlockSpec((None, length, v_width), lambda br, i: (br // dil, 0, v_blk(br % dil))),
                pl.BlockSpec(bias.shape, lambda br, i: (0, 0, 0, 0))]
    args = [q_arr, k_arr, v_arr, bias]
    if with_sel:
        in_specs.append(pl.BlockSpec((None, q_tiles * tile, 2 * LANES), lambda br, i: (br, i, 0)))
        args.append(selb)
    o_map = lambda br, i: (br // dil, i, br % dil)
    out_specs = [pl.BlockSpec((None, q_tiles * tile, width), o_map)]
    out_shape = [jax.ShapeDtypeStruct((batch, length, dil * width), BF16)]
    if with_lse:
        out_specs.append(pl.BlockSpec((None, q_tiles * tile, width), o_map))
        out_shape.append(jax.ShapeDtypeStruct((batch, length, dil * width), F32))
    return pl.pallas_call(
        functools.partial(_flash_kernel, tile=tile, q_tiles=q_tiles, n_pairs=n_pairs, heads=heads, per_pair_kv=per_pair_kv,
                          n_back=n_back, n_bias=bias.shape[1], with_sel=with_sel, with_lse=with_lse),
        grid=(batch * dil, nq // q_tiles),
        in_specs=in_specs,
        out_specs=out_specs,
        out_shape=out_shape,
        scratch_shapes=[pltpu.VMEM((q_tiles * 2 * n_pairs, LANES, tile), BF16)]
        + ([pltpu.VMEM((2 * n_pairs, LANES, tile), F32)] if n_back is None else []),
        compiler_params=_cparams(("arbitrary", "arbitrary")),
        name="flash_sel" if with_sel else ("flash_dil" if with_lse else "flash_win"),
    )(*args)


def _mix_kernel(x_ref, g1_ref, oc_ref, os_ref, ow_ref, gl_ref, ge_ref,
                od0_ref, od1_ref, od2_ref, l0_ref, l1_ref, l2_ref, ml_ref,
                wn_ref, wd_ref, wo_ref, o_ref, slab_ref):
    tm, d = x_ref.shape

    def in_token_order(ref):
        if len(ref.shape) == 2:
            return ref[...].astype(F32)
        dil = ref.shape[0]
        n_slab = ref.shape[2] // LANES
        for r in range(dil):
            for c in range(n_slab):
                slab_ref[c, pl.ds(r, tm // dil, stride=dil), :] = ref[r, :, c * LANES:(c + 1) * LANES].astype(F32)
        return jnp.concatenate([slab_ref[c] for c in range(n_slab)], axis=1)

    sig = 1.0 / (1.0 + jnp.exp(-gl_ref[...]))
    hi = sig.astype(BF16)
    lo = (sig - hi.astype(F32)).astype(BF16)
    ge = ge_ref[...]
    gates = jnp.dot(hi, ge, preferred_element_type=F32) + jnp.dot(lo, ge, preferred_element_type=F32)
    o_nsa = (gates[:, 0:512] * oc_ref[...].astype(F32)
             + gates[:, 512:1024] * os_ref[...].astype(F32)
             + gates[:, 1024:1536] * ow_ref[...].astype(F32))
    u_nsa = jnp.dot(o_nsa.astype(BF16), wn_ref[...], preferred_element_type=F32)

    lses = [in_token_order(ref) for ref in (l0_ref, l1_ref, l2_ref)]
    mx = jnp.maximum(jnp.maximum(lses[0], lses[1]), lses[2])
    es = [jnp.exp2(l - mx) for l in lses]
    inv = 1.0 / (es[0] + es[1] + es[2])
    o_dil = (es[0] * in_token_order(od0_ref) + es[1] * in_token_order(od1_ref)
             + es[2] * in_token_order(od2_ref)) * inv
    u_dil = jnp.dot(o_dil.astype(BF16), wd_ref[...], preferred_element_type=F32)

    gm = 1.0 / (1.0 + jnp.exp(-ml_ref[...].astype(F32)))
    merged = gm[:, :d] * u_nsa + gm[:, d:] * u_dil
    y = jnp.dot(merged.astype(BF16), wo_ref[...], preferred_element_type=F32)
    o_ref[...] = x_ref[...] + g1_ref[...] * y


def _mix(x2, g1, oc, osel, ow, gl, ge, od, lse, ml, wn, wd, wo, seq):
    t, d = x2.shape
    tm = TM_PROJ
    per_b = seq // tm
    row = lambda i: (i, 0)
    const = lambda i: (0, 0)
    full = lambda a: pl.BlockSpec(a.shape, const)

    def rows(a):
        if a.ndim == 2:
            return pl.BlockSpec((tm, a.shape[1]), row)
        dil = a.shape[1]
        return pl.BlockSpec((None, dil, tm // dil, a.shape[3]), lambda i: (i // per_b, 0, i % per_b, 0))

    return pl.pallas_call(
        _mix_kernel,
        grid=(t // tm,),
        in_specs=[rows(x2), pl.BlockSpec((None, 1, d), lambda i: (i // per_b, 0, 0)),
                  rows(oc), rows(osel), rows(ow), rows(gl), full(ge),
                  rows(od[0]), rows(od[1]), rows(od[2]), rows(lse[0]), rows(lse[1]), rows(lse[2]), rows(ml),
                  full(wn), full(wd), full(wo)],
        out_specs=pl.BlockSpec((tm, d), row),
        out_shape=jax.ShapeDtypeStruct((t, d), F32),
        scratch_shapes=[pltpu.VMEM((2, tm, LANES), F32)],
        compiler_params=_cparams(("arbitrary",)),
        name="mix_outproj",
    )(x2, g1, oc, osel, ow, gl, ge, *od, *lse, ml, wn, wd, wo)


def _router_kernel(x_ref, sc_ref, sh_ref, g_ref, wr_ref, br_ref, h_ref, eid_ref, wts_ref):
    h = _norm_mod(x_ref[...], g_ref[...], sc_ref[...], sh_ref[...])
    n_sub = h.shape[1] // LANES
    for j in range(n_sub):
        h_ref[pl.ds(j, h.shape[0], stride=n_sub), :] = h[:, j * LANES:(j + 1) * LANES]
    nt = (((1,), (1,)), ((), ()))
    w = wr_ref[...]
    w_hi, h_hi = w.astype(BF16), h.astype(BF16)
    w_lo, h_lo = (w - w_hi.astype(F32)).astype(BF16), (h - h_hi.astype(F32)).astype(BF16)
    logit = (lax.dot_general(w_hi, h_hi, nt, preferred_element_type=F32)
             + lax.dot_general(w_hi, h_lo, nt, preferred_element_type=F32)
             + lax.dot_general(w_lo, h_hi, nt, preferred_element_type=F32)) + br_ref[...]
    grp = jnp.zeros((1, h.shape[0]), jnp.int32)
    best = logit[0:1]
    for k in range(1, N_EXPERT_GROUPS):
        better = logit[k:k + 1] > best
        grp = jnp.where(better, k, grp)
        best = jnp.where(better, logit[k:k + 1], best)
    den = jnp.zeros_like(best)
    for k in range(N_EXPERT_GROUPS):
        den = den + jnp.exp(logit[k:k + 1] - best)
    p_grp = 1.0 / den
    le = logit[SUBLANES:SUBLANES + EXPERTS_PER_GROUP]
    for k in range(1, N_EXPERT_GROUPS):
        lo = SUBLANES + k * EXPERTS_PER_GROUP
        le = jnp.where(grp == k, logit[lo:lo + EXPERTS_PER_GROUP], le)
    rowi = lax.broadcasted_iota(jnp.int32, le.shape, 0)
    v1 = jnp.max(le, axis=0, keepdims=True)
    i1 = jnp.min(jnp.where(le == v1, rowi, EXPERTS_PER_GROUP), axis=0, keepdims=True)
    rest = jnp.where(rowi == i1, -3e38, le)
    v2 = jnp.max(rest, axis=0, keepdims=True)
    i2 = jnp.min(jnp.where(rest == v2, rowi, EXPERTS_PER_GROUP), axis=0, keepdims=True)
    e2 = jnp.exp(v2 - v1)
    inv = p_grp / (1.0 + e2)
    eid_ref[...] = jnp.concatenate([grp * EXPERTS_PER_GROUP + i1, grp * EXPERTS_PER_GROUP + i2], axis=0)
    wts_ref[...] = jnp.concatenate([inv, e2 * inv], axis=0)


def _router(x2, sc, sh, g, wr_t, br, seq):
    t, d = x2.shape
    tm = TM_PROJ
    per_b = seq // tm
    row = lambda i: (i, 0)
    const = lambda i: (0, 0)
    return pl.pallas_call(
        _router_kernel,
        grid=(t // tm,),
        in_specs=[pl.BlockSpec((tm, d), row),
                  pl.BlockSpec((None, 1, d), lambda i: (i // per_b, 0, 0)),
                  pl.BlockSpec((None, 1, d), lambda i: (i // per_b, 0, 0)),
                  pl.BlockSpec((1, d), const),
                  pl.BlockSpec(wr_t.shape, const),
                  pl.BlockSpec(br.shape, const)],
        out_specs=[pl.BlockSpec((tm * (d // LANES), LANES), row),
                   pl.BlockSpec((2, tm), lambda i: (0, i)),
                   pl.BlockSpec((2, tm), lambda i: (0, i))],
        out_shape=[jax.ShapeDtypeStruct((t * (d // LANES), LANES), F32),
                   jax.ShapeDtypeStruct((2, t), jnp.int32),
                   jax.ShapeDtypeStruct((2, t), F32)],
        compiler_params=_cparams(("arbitrary",)),
        name="norm_router",
    )(x2, sc, sh, g, wr_t, br)


def _moe_kernel(cnt_ref, off_ref, tok_ref, wt_ref, h_ref, w1_ref, w3_ref, w2_ref, o_ref,
                xs_ref, os_ref, xb_ref, y3_ref):
    c = pl.program_id(0)
    e = pl.program_id(1)
    n_sub = xb_ref.shape[1] // LANES
    rb = xb_ref.shape[0]

    @pl.when(e == 0)
    def _():
        o_ref[...] = jnp.zeros_like(o_ref)

    n = cnt_ref[c, e]
    off = off_ref[c, e]

    def every_row(base, last, fn):
        def grp(g, _):
            slot0 = base + g * MOE_UNROLL
            row0 = pl.multiple_of(g * (MOE_UNROLL * n_sub), MOE_UNROLL * n_sub)
            for u in range(MOE_UNROLL):
                fn(jnp.minimum(slot0 + u, last), pl.ds(row0 + u * n_sub, n_sub))
            return 0

        lax.fori_loop(0, rb // MOE_UNROLL, grp, 0)

    def token_rows(slot):
        return pl.ds(pl.multiple_of(tok_ref[0, slot], n_sub), n_sub)

    def block(bi, _):
        base = off + bi * rb
        last = off + jnp.minimum(n, (bi + 1) * rb) - 1

        def gather(slot, rows):
            src = token_rows(slot)
            xs_ref[rows, :] = h_ref[src, :]
            os_ref[rows, :] = o_ref[src, :]

        every_row(base, last, gather)
        for j in range(n_sub):
            xb_ref[:, j * LANES:(j + 1) * LANES] = xs_ref[pl.ds(j, rb, stride=n_sub), :].astype(BF16)
        xb = xb_ref[...]
        a = jnp.dot(xb, w1_ref[...], preferred_element_type=F32)
        b = jnp.dot(xb, w3_ref[...], preferred_element_type=F32)
        mid = (a * (1.0 / (1.0 + jnp.exp(-a))) * b).astype(BF16)
        y = jnp.dot(mid, w2_ref[...], preferred_element_type=F32)
        for j in range(n_sub):
            y3_ref[pl.ds(j, rb, stride=n_sub), :] = y[:, j * LANES:(j + 1) * LANES]

        def scatter(slot, rows):
            o_ref[token_rows(slot), :] = os_ref[rows, :] + wt_ref[0, slot] * y3_ref[rows, :]

        every_row(base, last, scatter)
        return 0

    lax.fori_loop(0, (n + rb - 1) // rb, block, 0)


def _moe(h2, eid, wts, w1, w3, w2, layer):
    d = w1.shape[2]
    n_sub = d // LANES
    t = h2.shape[0] // n_sub
    tc = min(MOE_CHUNK, t)
    n_chunks = t // tc
    slots = 2 * tc
    tok = jnp.arange(t, dtype=jnp.int32)
    key = ((tok // tc)[None, :] * N_EXPERTS + eid).reshape(-1)
    order = jnp.argsort(key)
    tok_sorted = (jnp.tile(tok % tc * n_sub, 2)[order]).reshape(n_chunks, slots)
    w_sorted = wts.reshape(-1)[order].reshape(n_chunks, slots)
    counts = jnp.zeros((n_chunks * N_EXPERTS,), jnp.int32).at[key].add(1).reshape(n_chunks, N_EXPERTS)
    starts = jnp.cumsum(counts, axis=1) - counts
    tok_sorted = tok_sorted.reshape(n_chunks, 1, slots)
    w_sorted = w_sorted.reshape(n_chunks, 1, slots)

    grid_spec = pltpu.PrefetchScalarGridSpec(
        num_scalar_prefetch=2,
        grid=(n_chunks, N_EXPERTS),
        in_specs=[pl.BlockSpec((None, 1, slots), lambda c, e, *_: (c, 0, 0), memory_space=pltpu.SMEM),
                  pl.BlockSpec((None, 1, slots), lambda c, e, *_: (c, 0, 0), memory_space=pltpu.SMEM),
                  pl.BlockSpec((tc * n_sub, LANES), lambda c, e, *_: (c, 0), pipeline_mode=pl.Buffered(1)),
                  pl.BlockSpec((None, None, d, D_EXPERT), lambda c, e, *_: (layer, e, 0, 0)),
                  pl.BlockSpec((None, None, d, D_EXPERT), lambda c, e, *_: (layer, e, 0, 0)),
                  pl.BlockSpec((None, None, D_EXPERT, d), lambda c, e, *_: (layer, e, 0, 0))],
        out_specs=pl.BlockSpec((tc * n_sub, LANES), lambda c, e, *_: (c, 0), pipeline_mode=pl.Buffered(1)),
        scratch_shapes=[pltpu.VMEM((MOE_ROWS * n_sub, LANES), F32),
                        pltpu.VMEM((MOE_ROWS * n_sub, LANES), F32),
                        pltpu.VMEM((MOE_ROWS, d), BF16),
                        pltpu.VMEM((MOE_ROWS * n_sub, LANES), F32)],
    )
    out = pl.pallas_call(
        _moe_kernel,
        grid_spec=grid_spec,
        out_shape=jax.ShapeDtypeStruct((t * n_sub, LANES), F32),
        compiler_params=_cparams(("arbitrary", "arbitrary")),
        name="moe_experts",
    )(counts, starts, tok_sorted, w_sorted, h2, w1, w3, w2)
    return out


def _resid_kernel(x_ref, y_ref, g_ref, nf_ref, o_ref):
    tm, d = x_ref.shape
    x = x_ref[...] + g_ref[...] * _tile_rows(y_ref, tm, d)
    ms = jnp.mean(x * x, axis=-1, keepdims=True)
    o_ref[...] = x * lax.rsqrt(ms + RMS_EPS) * nf_ref[...]


def _residual(x2, y2, g2, norm_f, seq):
    t, d = x2.shape
    tm = TM_PROJ
    per_b = seq // tm
    row = lambda i: (i, 0)
    return pl.pallas_call(
        _resid_kernel,
        grid=(t // tm,),
        in_specs=[pl.BlockSpec((tm, d), row), pl.BlockSpec((tm * (d // LANES), LANES), row),
                  pl.BlockSpec((None, 1, d), lambda i: (i // per_b, 0, 0)),
                  pl.BlockSpec((1, d), lambda i: (0, 0))],
        out_specs=pl.BlockSpec((tm, d), row),
        out_shape=jax.ShapeDtypeStruct((t, d), F32),
        compiler_params=_cparams(("arbitrary",)),
        name="residual_final",
    )(x2, y2, g2, norm_f)


def _split_w_in(w_in, d):
    scale = HEAD_DIM ** -0.5 * LOG2E
    nq = NSA_HEADS * HEAD_DIM
    nkv = 3 * 2 * NSA_KV_HEADS * HEAD_DIM
    ngate = 3 * NSA_HEADS
    ndil = 3 * N_DIL_GROUPS * DIL_HEADS_PER_GROUP * HEAD_DIM
    o1, o2, o3 = nq, nq + nkv, nq + nkv + ngate
    o4 = o3 + ndil
    wq = (w_in[:, :o1] * scale).reshape(d, NSA_KV_HEADS, NSA_GROUP, HEAD_DIM)
    wq = wq.transpose(0, 2, 1, 3).reshape(d, nq)
    wkv = w_in[:, o1:o2]
    wg = jnp.pad(w_in[:, o2:o3], ((0, 0), (0, LANES - ngate)))
    gw = DIL_HEADS_PER_GROUP * HEAD_DIM
    per_which = N_DIL_GROUPS * gw
    wds = []
    for grp in range(N_DIL_GROUPS):
        parts = [w_in[:, o3 + which * per_which + grp * gw: o3 + which * per_which + (grp + 1) * gw]
                 for which in range(3)]
        parts[0] = parts[0] * scale
        wds.append(jnp.concatenate(parts, axis=1))
    wm = w_in[:, o4:]
    cast = lambda w: w.astype(BF16)
    return [cast(wq), cast(wkv), cast(wds[0]), cast(wds[1]), cast(wds[2]), cast(wm), cast(wg)]


def kernel(x, c, rel_bias, ada_w, ada_b, norm1, norm2, w_in, cmp_pos, cmp_w1, cmp_w2, w_up_nsa, w_up_dil, w_o,
           router_wg, router_bg, router_we, router_be, exp_w1, exp_w3, exp_w2, norm_f):
    batch, seq, d = x.shape
    depth = ada_w.shape[0]
    t = batch * seq
    n_cmp = seq // CMP_STRIDE
    n_sel = seq // SEL_BLOCK
    assert seq % TQ_NSA == 0 and n_sel <= HEAD_DIM and n_sel >= SEL_TOPN
    assert all(seq % (dil * TQ_DIL) == 0 for _, dil in DIL_PAIRS)

    mod = _modulation(c, ada_w, ada_b)

    nq_nsa = seq // TQ_NSA
    bias_cmp = _expand_bias(rel_bias, _cmp_buckets(seq, n_cmp), 0, NSA_HEADS, 8 * SUBLANES)
    sel_b = _toeplitz_buckets(nq_nsa, TQ_NSA, seq, 1)
    n_sel_bias = nq_nsa
    while n_sel_bias > 1 and (sel_b[n_sel_bias - 2:] == sel_b[n_sel_bias - 1, 0, 0]).all():
        n_sel_bias -= 1
    sel_b = sel_b[:n_sel_bias].reshape(n_sel_bias * TQ_NSA, TQ_NSA)
    bias_sel = _expand_bias(rel_bias, sel_b, 0, NSA_HEADS, TQ_NSA).reshape(NSA_HEADS, n_sel_bias, TQ_NSA, TQ_NSA)
    nb_win = min(-(-NSA_WINDOW // TQ_NSA), nq_nsa - 1)
    win_b = _toeplitz_buckets(nb_win + 1, TQ_NSA, NSA_WINDOW, 1, masked_tail=True).reshape(-1, TQ_NSA)
    bias_win = _expand_bias(rel_bias, win_b, 0, NSA_HEADS, TQ_NSA).reshape(NSA_HEADS, nb_win + 2, TQ_NSA, TQ_NSA)
    bias_dil, nb_dil = [], []
    for grp, (window, dil) in enumerate(DIL_PAIRS):
        nb = min(-(-(window // dil) // TQ_DIL), seq // dil // TQ_DIL - 1)
        bk = _toeplitz_buckets(nb + 1, TQ_DIL, window // dil, dil, masked_tail=True).reshape(-1, TQ_DIL)
        hb = NSA_HEADS + grp * DIL_HEADS_PER_GROUP
        bias_dil.append(_expand_bias(rel_bias, bk, hb, DIL_HEADS_PER_GROUP, TQ_DIL)
                        .reshape(DIL_HEADS_PER_GROUP, nb + 2, TQ_DIL, TQ_DIL))
        nb_dil.append(nb)
    ovl_t = jnp.asarray(_overlap_t(n_cmp, n_sel), BF16)
    onehot = jnp.asarray(_block_onehot(seq), BF16)
    gate_e = jnp.asarray(_gate_expand(), BF16)

    expert_w = [w.astype(BF16) for w in (exp_w1, exp_w3, exp_w2)]
    x2 = x.reshape(t, d)
    pending = None
    for l in range(depth):
        sh1, sc1, g1, sh2, sc2, g2 = [m.reshape(batch, 1, d) for m in jnp.split(mod[l], 6, axis=-1)]
        weights = _split_w_in(w_in[l], d)
        outs = list(_inproj(x2, sc1, sh1, norm1[l].reshape(1, d), onehot, weights, seq, pending))
        if pending is not None:
            x2 = outs.pop()
        (q_n, kc_in, vc_in, k_sel, v_sel, k_win, v_win, qkv_d0, qkv_d1, qkv_d2, merge_l, gate_l) = outs

        kc = _compress(kc_in, cmp_pos[l, 0], cmp_w1[l, 0], cmp_w2[l, 0], batch, seq, transpose_out=False)
        vct = _compress(vc_in, cmp_pos[l, 1], cmp_w1[l, 1], cmp_w2[l, 1], batch, seq, transpose_out=True)
        o_c, selb = _cmp_select(q_n, kc, vct, bias_cmp, ovl_t, batch, seq)
        nsa_common = dict(batch=batch, length=seq, dil=1, tile=TQ_NSA, n_pairs=NSA_GROUP, q_blk=lambda r: 0,
                          k_blk=lambda r: 0, v_blk=lambda r: 0, v_width=LANES, per_pair_kv=False,
                          heads=tuple((g, NSA_GROUP + g) for g in range(NSA_GROUP)))
        q3 = q_n.reshape(batch, seq, 512)
        (o_s,) = _flash(q3, k_sel.reshape(batch, seq, 2 * LANES), v_sel.reshape(batch, seq, LANES), bias_sel,
                        n_back=None, k_width=2 * LANES, selb=selb.reshape(batch, seq, 2 * LANES), **nsa_common)
        (o_w,) = _flash(q3, k_win.reshape(batch, seq, LANES), v_win.reshape(batch, seq, LANES), bias_win,
                        n_back=nb_win, k_width=LANES, q_tiles=WIN_Q_TILES, **nsa_common)

        o_d, lse_d = [], []
        for grp, ((window, dil), qkv) in enumerate(zip(DIL_PAIRS, (qkv_d0, qkv_d1, qkv_d2))):
            length = seq // dil
            view = qkv.reshape(batch * dil, length, 768)
            o_g, lse_g = _flash(view, view, view, bias_dil[grp], batch=batch * dil, length=length, dil=1,
                                tile=TQ_DIL, n_back=nb_dil[grp], n_pairs=2, q_blk=lambda r: 0, k_blk=lambda r: 1,
                                k_width=2 * LANES, v_blk=lambda r: 2, v_width=2 * LANES,
                                per_pair_kv=True, heads=((0, 1), (2, 3)), with_lse=True,
                                q_tiles=math.gcd(DIL_Q_TILES, length // TQ_DIL))
            shape = (t, 256) if dil == 1 else (batch, dil, length, 256)
            o_d.append(o_g.reshape(shape))
            lse_d.append(lse_g.reshape(shape))

        wn = w_up_nsa[l].reshape(NSA_KV_HEADS, NSA_GROUP, HEAD_DIM, d).transpose(1, 0, 2, 3).reshape(512, d)
        x2 = _mix(x2, g1, o_c, o_s.reshape(t, 512), o_w.reshape(t, 512), gate_l, gate_e, o_d, lse_d, merge_l,
                  wn.astype(BF16), w_up_dil[l].astype(BF16), w_o[l].astype(BF16), seq)

        wr_t = jnp.concatenate([jnp.pad(router_wg[l], ((0, 0), (0, SUBLANES - N_EXPERT_GROUPS))),
                                router_we[l]], axis=1).T
        br = jnp.concatenate([jnp.pad(router_bg[l], (0, SUBLANES - N_EXPERT_GROUPS)),
                              router_be[l]]).reshape(-1, 1)
        h2, eid, wts = _router(x2, sc2, sh2, norm2[l].reshape(1, d), wr_t, br, seq)
        y = _moe(h2, eid, wts, *expert_w, layer=l)
        if l == depth - 1:
            x2 = _residual(x2, y, g2, norm_f.reshape(1, d), seq)
        else:
            pending = (y, g2)
    return x2.reshape(batch, seq, d)
```

```python
import functools
import math

import numpy as np
import jax
import jax.numpy as jnp
from jax import lax
from jax.experimental import pallas as pl
from jax.experimental.pallas import tpu as pltpu

F32 = jnp.float32
BF16 = jnp.bfloat16

HEAD_DIM = 64
NSA_HEADS = 8
NSA_KV_HEADS = 2
NSA_GROUP = NSA_HEADS // NSA_KV_HEADS
CMP_LEN = 32
CMP_STRIDE = 16
CMP_HIDDEN = 256
SEL_BLOCK = 64
SEL_TOPN = 16
NSA_WINDOW = 512
FORCE_SCORE = 1e4
DIL_PAIRS = ((128, 1), (512, 4), (2048, 16))
N_DIL_GROUPS = 3
DIL_HEADS_PER_GROUP = 4
NUM_BUCKETS = 32
REL_MAX_DIST = 2048
N_EXPERT_GROUPS = 4
EXPERTS_PER_GROUP = 8
N_EXPERTS = N_EXPERT_GROUPS * EXPERTS_PER_GROUP
D_EXPERT = 512
RMS_EPS = 1e-6

LOG2E = math.log2(math.e)
LANES = 128
SUBLANES = 8
MASK_NEG = -1e30
SEL_NEG = -1e9
TQ_NSA = 256
TQ_DIL = 128
TM_PROJ = 512
WIN_Q_TILES = 4
DIL_Q_TILES = 4
FLASH_LOOKAHEAD = 4
MOE_CHUNK = 4096
MOE_ROWS = 128
MOE_UNROLL = 32
VMEM_LIMIT = 56 * 1024 * 1024


def _cparams(sem):
    return pltpu.CompilerParams(dimension_semantics=sem, vmem_limit_bytes=VMEM_LIMIT)


def _np_bucket(dist):
    dist = np.maximum(dist, 0)
    max_exact = NUM_BUCKETS // 2
    df = np.maximum(dist, 1).astype(np.float32)
    val = np.log(df / np.float32(max_exact)) / np.float32(math.log(REL_MAX_DIST / max_exact))
    large = max_exact + (val * np.float32(NUM_BUCKETS - max_exact)).astype(np.int32)
    large = np.minimum(large, NUM_BUCKETS - 1)
    return np.where(dist < max_exact, dist, large).astype(np.int32)


def _toeplitz_buckets(n_delta, tile, window, dist_scale, masked_tail=False):
    dd = np.arange(n_delta)[:, None, None]
    r = np.arange(tile)[None, :, None]
    c = np.arange(tile)[None, None, :]
    dist = dd * tile + c - r
    valid = (dist >= 0) & (dist <= window)
    out = np.where(valid, _np_bucket(dist * dist_scale), -1).astype(np.int32)
    if masked_tail:
        out = np.concatenate([out, np.full((1, tile, tile), -1, np.int32)], axis=0)
    return out


def _cmp_buckets(seq, n_cmp):
    c_end = np.arange(n_cmp)[:, None] * CMP_STRIDE + CMP_LEN - 1
    t = np.arange(seq)[None, :]
    dist = t - c_end
    return np.where(dist >= 0, _np_bucket(dist), -1).astype(np.int32)


def _overlap_t(n_cmp, n_sel):
    c_start = np.arange(n_cmp)[None, :] * CMP_STRIDE
    s_start = np.arange(n_sel)[:, None] * SEL_BLOCK
    ov = np.clip(np.minimum(c_start + CMP_LEN, s_start + SEL_BLOCK) - np.maximum(c_start, s_start), 0, None)
    out = np.zeros((LANES, n_cmp), np.float32)
    out[64:64 + n_sel] = ov.astype(np.float32) / CMP_STRIDE
    return out


def _block_onehot(seq):
    oh = np.zeros((seq, LANES), np.float32)
    blk = np.arange(seq) // SEL_BLOCK
    oh[np.arange(seq), blk] = 1.0
    oh[np.arange(seq), 64 + blk] = 1.0
    return oh


def _gate_expand():
    e = np.zeros((LANES, 3 * NSA_HEADS * HEAD_DIM), np.float32)
    for br in range(3):
        for g in range(NSA_GROUP):
            for ln in range(LANES):
                kv = ln // HEAD_DIM
                e[br * NSA_HEADS + kv * NSA_GROUP + g, br * 512 + g * LANES + ln] = 1.0
    return e


def _mod_kernel(c_ref, w_ref, b_ref, o_ref):
    c = c_ref[...]
    cond = c * (1.0 / (1.0 + jnp.exp(-c)))
    o_ref[...] = jnp.dot(cond, w_ref[...], preferred_element_type=F32,
                         precision=lax.Precision.HIGHEST) + b_ref[...]


def _modulation(c, ada_w, ada_b):
    depth, d, n = ada_w.shape
    b = c.shape[0]
    tn = 1536
    return pl.pallas_call(
        _mod_kernel,
        grid=(depth, n // tn),
        in_specs=[pl.BlockSpec((b, d), lambda l, j: (0, 0)),
                  pl.BlockSpec((None, d, tn), lambda l, j: (l, 0, j)),
                  pl.BlockSpec((None, 1, tn), lambda l, j: (l, 0, j))],
        out_specs=pl.BlockSpec((None, b, tn), lambda l, j: (l, 0, j)),
        out_shape=jax.ShapeDtypeStruct((depth, b, n), F32),
        compiler_params=_cparams(("arbitrary", "arbitrary")),
        name="adaln_modulation",
    )(c, ada_w, ada_b.reshape(depth, 1, n))


def _bias_kernel(tbl_ref, bkt_ref, o_ref, *, head_base):
    h = pl.program_id(0) + head_base
    bkt = bkt_ref[...]
    acc = jnp.full(bkt.shape, MASK_NEG, F32)
    for b in range(NUM_BUCKETS):
        acc = jnp.where(bkt == b, tbl_ref[b, h] * LOG2E, acc)
    o_ref[...] = acc


def _expand_bias(rel_bias, buckets, head_base, n_heads, row_tile):
    rows, cols = buckets.shape
    return pl.pallas_call(
        functools.partial(_bias_kernel, head_base=head_base),
        grid=(n_heads, rows // row_tile),
        in_specs=[pl.BlockSpec(memory_space=pltpu.SMEM),
                  pl.BlockSpec((row_tile, cols), lambda h, i: (i, 0))],
        out_specs=pl.BlockSpec((None, row_tile, cols), lambda h, i: (h, i, 0)),
        out_shape=jax.ShapeDtypeStruct((n_heads, rows, cols), F32),
        compiler_params=_cparams(("arbitrary", "arbitrary")),
        name="bias_expand",
    )(rel_bias, jnp.asarray(buckets))


def _norm_mod(x, g, sc, sh):
    ms = jnp.mean(x * x, axis=-1, keepdims=True)
    y = x * lax.rsqrt(ms + RMS_EPS) * g
    return y * (1.0 + sc) + sh


def _tile_rows(y_ref, tm, d):
    n_sub = d // LANES
    return jnp.concatenate([y_ref[pl.ds(j, tm, stride=n_sub), :] for j in range(n_sub)], axis=1)


def _inproj_kernel(*refs, fused):
    it = iter(refs)
    x_ref = next(it)
    y_ref, g2_ref = (next(it), next(it)) if fused else (None, None)
    sc_ref, sh_ref, g_ref, oh_ref, wq_ref, wkv_ref, wd0_ref, wd1_ref, wd2_ref, wm_ref, wg_ref = (
        next(it) for _ in range(11))
    q_ref, kc_ref, vc_ref, ks_ref, vs_ref, kw_ref, vw_ref, d0_ref, d1_ref, d2_ref, m_ref, gl_ref = (
        next(it) for _ in range(12))
    xo_ref = next(it) if fused else None
    slab_ref = next(it)
    x = x_ref[...]
    tm = x.shape[0]
    if fused:
        x = x + g2_ref[...] * _tile_rows(y_ref, tm, x.shape[1])
        xo_ref[...] = x
    h = _norm_mod(x, g_ref[...], sc_ref[...], sh_ref[...]).astype(BF16)

    def by_residue(w_ref, o_ref):
        dil = o_ref.shape[0]
        res = jnp.dot(h, w_ref[...], preferred_element_type=F32)
        n_slab = res.shape[1] // LANES
        for c in range(n_slab):
            slab_ref[c] = res[:, c * LANES:(c + 1) * LANES]
        for r in range(dil):
            for c in range(n_slab):
                o_ref[r, :, c * LANES:(c + 1) * LANES] = slab_ref[c, pl.ds(r, tm // dil, stride=dil), :].astype(BF16)

    def proj(w_ref):
        return jnp.dot(h, w_ref[...], preferred_element_type=F32)

    q_ref[...] = proj(wq_ref).astype(BF16)
    kv = proj(wkv_ref).astype(BF16)
    for k, ref in ((0, kc_ref), (1, vc_ref), (3, vs_ref), (4, kw_ref), (5, vw_ref)):
        ref[...] = kv[:, k * LANES:(k + 1) * LANES]
    k_sel = kv[:, 2 * LANES:3 * LANES]
    oh = oh_ref[...]
    lo = lax.broadcasted_iota(jnp.int32, k_sel.shape, 1) < HEAD_DIM
    ks_ref[:, :LANES] = jnp.where(lo, k_sel, oh)
    ks_ref[:, LANES:] = jnp.where(lo, oh, k_sel)
    d0_ref[...] = proj(wd0_ref).astype(BF16)
    by_residue(wd1_ref, d1_ref)
    by_residue(wd2_ref, d2_ref)
    m_ref[...] = proj(wm_ref).astype(BF16)
    gl_ref[...] = proj(wg_ref)


def _inproj(x2, sc, sh, g, onehot, weights, seq, pending=None):
    t, d = x2.shape
    tm = TM_PROJ
    per_b = seq // tm
    widths = (512, 128, 128, 256, 128, 128, 128, 768, 768, 768, 2 * d, LANES)
    dtypes = (BF16,) * 11 + (F32,)
    row = lambda i: (i, 0)
    const = lambda i: (0, 0)
    per_batch = pl.BlockSpec((None, 1, d), lambda i: (i // per_b, 0, 0))
    in_specs = [pl.BlockSpec((tm, d), row)]
    args = [x2]
    if pending is not None:
        in_specs += [pl.BlockSpec((tm * (d // LANES), LANES), row), per_batch]
        args += list(pending)
    in_specs += [per_batch, per_batch, pl.BlockSpec((1, d), const),
                 pl.BlockSpec((tm, LANES), lambda i: (i % per_b, 0))]
    in_specs += [pl.BlockSpec(w.shape, const) for w in weights]
    args += [sc, sh, g, onehot, *weights]
    out_specs = [pl.BlockSpec((tm, w), row) for w in widths]
    out_shape = [jax.ShapeDtypeStruct((t, w), dt) for w, dt in zip(widths, dtypes)]
    for k, (_, dil) in zip((8, 9), DIL_PAIRS[1:]):
        out_specs[k] = pl.BlockSpec((None, dil, tm // dil, 768), lambda i: (i // per_b, 0, i % per_b, 0))
        out_shape[k] = jax.ShapeDtypeStruct((t // seq, dil, seq // dil, 768), BF16)
    if pending is not None:
        out_specs.append(pl.BlockSpec((tm, d), row))
        out_shape.append(jax.ShapeDtypeStruct((t, d), F32))
    return pl.pallas_call(
        functools.partial(_inproj_kernel, fused=pending is not None),
        grid=(t // tm,),
        in_specs=in_specs,
        out_specs=out_specs,
        out_shape=out_shape,
        scratch_shapes=[pltpu.VMEM((768 // LANES, tm, LANES), F32)],
        compiler_params=_cparams(("arbitrary",)),
        name="norm_inproj",
    )(*args)


def _gelu_tanh(x):
    return 0.5 * x * (1.0 + jnp.tanh(math.sqrt(2.0 / math.pi) * (x + 0.044715 * (x * x * x))))


def _compress_kernel(r_ref, pt_ref, pb_ref, wt_ref, wb_ref, w2_ref, o_ref, *, transpose_out):
    r = r_ref[...].astype(F32)
    top = jnp.dot((r + pt_ref[...]).astype(BF16), wt_ref[...], preferred_element_type=F32)
    bot = jnp.dot((r + pb_ref[...]).astype(BF16), wb_ref[...], preferred_element_type=F32)
    n = bot.shape[0]
    hid = top + pltpu.roll(bot, n - 1, 0)
    act = _gelu_tanh(hid).astype(BF16)
    if transpose_out:
        o_ref[...] = lax.dot_general(w2_ref[...], act, (((1,), (1,)), ((), ())),
                                     preferred_element_type=F32).astype(BF16)
    else:
        o_ref[...] = jnp.dot(act, w2_ref[...], preferred_element_type=F32).astype(BF16)


def _compress(tok, pos, w1, w2, batch, seq, transpose_out):
    nc = seq // CMP_STRIDE
    half = CMP_LEN // 2
    eye = jnp.eye(NSA_KV_HEADS, dtype=F32)
    w1r = w1.reshape(CMP_LEN, HEAD_DIM, CMP_HIDDEN)
    blk = lambda w: jnp.einsum('ldn,hg->lhdgn', w, eye).reshape(half * LANES, 2 * CMP_HIDDEN).astype(BF16)
    wt, wb = blk(w1r[:half]), blk(w1r[half:])
    posr = lambda p: jnp.broadcast_to(p[:, None, :], (half, NSA_KV_HEADS, HEAD_DIM)).reshape(1, half * LANES)
    pt, pb = posr(pos[:half]), posr(pos[half:])
    w2b = jnp.einsum('nd,hg->hngd', w2, eye).reshape(2 * CMP_HIDDEN, LANES)
    if transpose_out:
        w2b = w2b.T
        out_block, out_shape = (None, LANES, nc), (batch, LANES, nc)
    else:
        out_block, out_shape = (None, nc, LANES), (batch, nc, LANES)
    w2b = w2b.astype(BF16)
    const = lambda b: (0, 0)
    return pl.pallas_call(
        functools.partial(_compress_kernel, transpose_out=transpose_out),
        grid=(batch,),
        in_specs=[pl.BlockSpec((None, nc, half * LANES), lambda b: (b, 0, 0)),
                  pl.BlockSpec(pt.shape, const), pl.BlockSpec(pb.shape, const),
                  pl.BlockSpec(wt.shape, const), pl.BlockSpec(wb.shape, const),
                  pl.BlockSpec(w2b.shape, const)],
        out_specs=pl.BlockSpec(out_block, lambda b: (b, 0, 0)),
        out_shape=jax.ShapeDtypeStruct(out_shape, BF16),
        compiler_params=_cparams(("arbitrary",)),
        name="nsa_compress",
    )(tok.reshape(batch, nc, half * LANES), pt, pb, wt, wb, w2b)


def _cmp_select_kernel(q_ref, kc_ref, vct_ref, bias_ref, ovl_ref, o_ref, selb_ref, *, n_sel, n_top):
    tq = q_ref.shape[0]
    nc = kc_ref.shape[0]
    qs = pl.program_id(1) * tq
    kc = kc_ref[...]
    vct = vct_ref[...]
    lane_q = lax.broadcasted_iota(jnp.int32, (tq, LANES), 1)
    row_o = lax.broadcasted_iota(jnp.int32, (LANES, tq), 0)
    heads = [(g, kv) for g in range(NSA_GROUP) for kv in range(NSA_KV_HEADS)]

    def scores(g, kv):
        qt = q_ref[:, g * LANES:(g + 1) * LANES]
        mine = (lane_q < HEAD_DIM) if kv == 0 else (lane_q >= HEAD_DIM)
        qm = jnp.where(mine, qt, jnp.zeros_like(qt))
        s = lax.dot_general(kc, qm, (((1,), (1,)), ((), ())), preferred_element_type=F32)
        return s + bias_ref[kv * NSA_GROUP + g]

    pc_sum = [jnp.zeros((nc, tq), F32) for _ in range(NSA_KV_HEADS)]
    outs = {}
    ahead = [scores(*hd) for hd in heads[:FLASH_LOOKAHEAD]]
    for n, (g, kv) in enumerate(heads):
        s = ahead.pop(0)
        if n + FLASH_LOOKAHEAD < len(heads):
            ahead.append(scores(*heads[n + FLASH_LOOKAHEAD]))
        m = jnp.max(s, axis=0, keepdims=True)
        m = jnp.where(m < 0.5 * MASK_NEG, 0.0, m)
        p = jnp.exp2(s - m)
        den = jnp.sum(p, axis=0, keepdims=True)
        pc = p * (1.0 / jnp.where(den > 0.0, den, 1.0))
        pc_sum[kv] = pc_sum[kv] + pc
        outs[kv] = jnp.dot(vct, pc.astype(BF16), preferred_element_type=F32)
        if kv == NSA_KV_HEADS - 1:
            o_t = jnp.where(row_o < HEAD_DIM, outs[0], outs[1])
            o_ref[:, g * LANES:(g + 1) * LANES] = o_t.T.astype(BF16)

    ovl = ovl_ref[...]
    imps = []
    for kv in range(NSA_KV_HEADS):
        hi = pc_sum[kv].astype(BF16)
        lo = (pc_sum[kv] - hi.astype(F32)).astype(BF16)
        imps.append((jnp.dot(ovl, hi, preferred_element_type=F32)
                     + jnp.dot(ovl, lo, preferred_element_type=F32))[HEAD_DIM:])
    imp = jnp.concatenate(imps, axis=1)
    wide = (HEAD_DIM, NSA_KV_HEADS * tq)
    rowj = lax.broadcasted_iota(jnp.int32, wide, 0)
    col = lax.broadcasted_iota(jnp.int32, wide, 1)
    t = qs + jnp.where(col >= tq, col - tq, col)
    jq = jnp.right_shift(t, SEL_BLOCK.bit_length() - 1)
    forced = (rowj == 0) | (rowj == jq) | (rowj == jq - 1)
    score = jnp.where(rowj > jq, -1.0, imp)
    rem = jnp.where(forced | (rowj >= n_sel), -3e38, score)
    sel = jnp.where(forced, 1.0, 0.0)
    for _ in range(n_top - 3):
        m = jnp.max(rem, axis=0, keepdims=True)
        idx = jnp.min(jnp.where(rem == m, rowj, HEAD_DIM), axis=0, keepdims=True)
        pick = rowj == idx
        sel = jnp.where(pick, 1.0, sel)
        rem = jnp.where(pick, -3e38, rem)
    sb = jnp.where(sel > 0.5, 0.0, SEL_NEG)
    zero = jnp.zeros((HEAD_DIM, tq), F32)
    selb_ref[:, :LANES] = jnp.concatenate([zero, sb[:, :tq]], axis=0).T.astype(BF16)
    selb_ref[:, LANES:] = jnp.concatenate([sb[:, tq:], zero], axis=0).T.astype(BF16)


def _cmp_select(q, kc, vct, bias_c, ovl_t, batch, seq):
    t = q.shape[0]
    tq = TQ_NSA
    nq = seq // tq
    nc = seq // CMP_STRIDE
    n_sel = seq // SEL_BLOCK
    n_top = min(SEL_TOPN, n_sel)
    return pl.pallas_call(
        functools.partial(_cmp_select_kernel, n_sel=n_sel, n_top=n_top),
        grid=(batch, nq),
        in_specs=[pl.BlockSpec((tq, 512), lambda b, i: (b * nq + i, 0)),
                  pl.BlockSpec((None, nc, LANES), lambda b, i: (b, 0, 0)),
                  pl.BlockSpec((None, LANES, nc), lambda b, i: (b, 0, 0)),
                  pl.BlockSpec((NSA_HEADS, nc, tq), lambda b, i: (0, 0, i)),
                  pl.BlockSpec((LANES, nc), lambda b, i: (0, 0))],
        out_specs=[pl.BlockSpec((tq, 512), lambda b, i: (b * nq + i, 0)),
                   pl.BlockSpec((tq, 2 * LANES), lambda b, i: (b * nq + i, 0))],
        out_shape=[jax.ShapeDtypeStruct((t, 512), BF16),
                   jax.ShapeDtypeStruct((t, 2 * LANES), BF16)],
        compiler_params=_cparams(("arbitrary", "arbitrary")),
        name="nsa_cmp_select",
    )(q, kc, vct, bias_c, ovl_t)


def _flash_kernel(*refs, tile, q_tiles, n_pairs, heads, per_pair_kv, n_back, n_bias, with_sel, with_lse):
    it = iter(refs)
    q_ref, k_ref, v_ref, b_ref = (next(it) for _ in range(4))
    selb_ref = next(it) if with_sel else None
    o_ref = next(it)
    lse_ref = next(it) if with_lse else None
    qa_ref = next(it)
    acc_ref = next(it) if n_back is None else None

    i = pl.program_id(1)
    n_heads = 2 * n_pairs
    lo = lax.broadcasted_iota(jnp.int32, (tile, LANES), 1) < HEAD_DIM
    tr = lambda a: a.astype(F32).T.astype(BF16)
    for u in range(q_tiles):
        rows = slice(u * tile, (u + 1) * tile)
        for p in range(n_pairs):
            q = q_ref[rows, p * LANES:(p + 1) * LANES]
            if with_sel:
                qa_ref[u * n_heads + 2 * p] = tr(jnp.where(lo, q, selb_ref[rows, :LANES]))
                qa_ref[u * n_heads + 2 * p + 1] = tr(jnp.where(lo, selb_ref[rows, LANES:], q))
            else:
                zero = jnp.zeros_like(q)
                qa_ref[u * n_heads + 2 * p] = tr(jnp.where(lo, q, zero))
                qa_ref[u * n_heads + 2 * p + 1] = tr(jnp.where(lo, zero, q))
    nt = (((1,), (1,)), ((), ()))
    tn = (((0,), (0,)), ((), ()))
    top = lax.broadcasted_iota(jnp.int32, (LANES, tile), 0) < HEAD_DIM

    def finish(u, p, acc0, l0, m0, acc1, l1, m1):
        rows = slice(u * tile, (u + 1) * tile)
        o_t = jnp.where(top, acc0 * (1.0 / l0), acc1 * (1.0 / l1))
        o_ref[rows, p * LANES:(p + 1) * LANES] = o_t.T.astype(o_ref.dtype)
        if with_lse:
            lse_t = jnp.where(top, m0 + jnp.log2(l0), m1 + jnp.log2(l1))
            lse_ref[rows, p * LANES:(p + 1) * LANES] = lse_t.T

    def scores(unit, tiles):
        hd = unit % n_heads
        p, half = divmod(hd, 2)
        kc = half if with_sel else (p if per_pair_kv else 0)
        parts = []
        for j, bidx in tiles:
            kt = k_ref[pl.ds(pl.multiple_of(j * tile, tile), tile), kc * LANES:(kc + 1) * LANES]
            s = jnp.dot(kt, qa_ref[unit], preferred_element_type=F32)
            parts.append(s + b_ref[heads[p][half], bidx])
        return parts

    def group(unit_tiles, state):
        res = []
        n_units = len(unit_tiles)
        ahead = [scores(n, unit_tiles[n]) for n in range(min(FLASH_LOOKAHEAD, n_units))]
        for hd in range(n_units):
            tiles = unit_tiles[hd]
            parts = ahead.pop(0)
            if hd + FLASH_LOOKAHEAD < n_units:
                ahead.append(scores(hd + FLASH_LOOKAHEAD, unit_tiles[hd + FLASH_LOOKAHEAD]))
            m_new = functools.reduce(jnp.maximum, [jnp.max(s, axis=0, keepdims=True) for s in parts])
            if state is not None:
                m, l = state[hd]
                m_new = jnp.maximum(m, m_new)
                alpha = jnp.exp2(m - m_new)
            prs = [jnp.exp2(s - m_new) for s in parts]
            l_new = functools.reduce(jnp.add, [jnp.sum(pr, axis=0, keepdims=True) for pr in prs])
            vc = (hd % n_heads // 2) if per_pair_kv else 0
            pv = None
            for (j, _), pr in zip(tiles, prs):
                vt = v_ref[pl.ds(pl.multiple_of(j * tile, tile), tile), vc * LANES:(vc + 1) * LANES]
                d = lax.dot_general(vt, pr.astype(BF16), tn, preferred_element_type=F32)
                pv = d if pv is None else pv + d
            if state is not None:
                l_new = alpha * l + l_new
                pv = alpha * acc_ref[hd] + pv
            res.append((m_new, l_new, pv))
        return res

    if n_back is not None:
        unit_tiles = []
        for u in range(q_tiles):
            iu = i * q_tiles + u
            tiles = [(jnp.maximum(iu - k, 0), jnp.where(iu >= k, k, n_bias - 1)) for k in range(n_back + 1)]
            unit_tiles += [tiles] * n_heads
        res = group(unit_tiles, None)
        for u in range(q_tiles):
            for p in range(n_pairs):
                (m0, l0, a0), (m1, l1, a1) = res[u * n_heads + 2 * p], res[u * n_heads + 2 * p + 1]
                finish(u, p, a0, l0, m0, a1, l1, m1)
        return

    def store(res):
        for hd, (_, _, acc) in enumerate(res):
            acc_ref[hd] = acc
        return tuple(x for m, l, _ in res for x in (m, l))

    def unpack(carry):
        return [(carry[2 * hd], carry[2 * hd + 1]) for hd in range(n_heads)]

    bias_of = lambda back: jnp.minimum(back, n_bias - 1)
    carry = store(group([[(i, 0)]] * n_heads, None))

    def pair_body(step, carry):
        back = 2 * step + 1
        tiles = [(i - back, bias_of(back)), (i - back - 1, bias_of(back + 1))]
        return store(group([tiles] * n_heads, unpack(carry)))

    carry = lax.fori_loop(0, i // 2, pair_body, carry)

    def last_body(_, carry):
        return store(group([[(0, bias_of(i))]] * n_heads, unpack(carry)))

    carry = lax.fori_loop(0, i % 2, last_body, carry)
    fin = unpack(carry)
    for p in range(n_pairs):
        finish(0, p, acc_ref[2 * p], fin[2 * p][1], fin[2 * p][0], acc_ref[2 * p + 1], fin[2 * p + 1][1], fin[2 * p + 1][0])


def _flash(q_arr, k_arr, v_arr, bias, *, batch, length, dil, tile, n_back, n_pairs, q_blk, k_blk, k_width,
           v_blk, v_width, per_pair_kv, heads, selb=None, with_lse=False, q_tiles=1):
    nq = length // tile
    with_sel = selb is not None
    width = n_pairs * LANES

    in_specs = [pl.BlockSpec((None, q_tiles * tile, width), lambda br, i: (br // dil, i, q_blk(br % dil))),
                pl.BlockSpec((None, length, k_width), lambda br, i: (br // dil, 0, k_blk(br % dil))),
                pl.BlockSpec((None, length, v_width), lambda br, i: (br // dil, 0, v_blk(br % dil))),
                pl.BlockSpec(bias.shape, lambda br, i: (0, 0, 0, 0))]
    args = [q_arr, k_arr, v_arr, bias]
    if with_sel:
        in_specs.append(pl.BlockSpec((None, q_tiles * tile, 2 * LANES), lambda br, i: (br, i, 0)))
        args.append(selb)
    o_map = lambda br, i: (br // dil, i, br % dil)
    out_specs = [pl.BlockSpec((None, q_tiles * tile, width), o_map)]
    out_shape = [jax.ShapeDtypeStruct((batch, length, dil * width), BF16)]
    if with_lse:
        out_specs.append(pl.BlockSpec((None, q_tiles * tile, width), o_map))
        out_shape.append(jax.ShapeDtypeStruct((batch, length, dil * width), F32))
    return pl.pallas_call(
        functools.partial(_flash_kernel, tile=tile, q_tiles=q_tiles, n_pairs=n_pairs, heads=heads, per_pair_kv=per_pair_kv,
                          n_back=n_back, n_bias=bias.shape[1], with_sel=with_sel, with_lse=with_lse),
        grid=(batch * dil, nq // q_tiles),
        in_specs=in_specs,
        out_specs=out_specs,
        out_shape=out_shape,
        scratch_shapes=[pltpu.VMEM((q_tiles * 2 * n_pairs, LANES, tile), BF16)]
        + ([pltpu.VMEM((2 * n_pairs, LANES, tile), F32)] if n_back is None else []),
        compiler_params=_cparams(("arbitrary", "arbitrary")),
        name="flash_sel" if with_sel else ("flash_dil" if with_lse else "flash_win"),
    )(*args)


def _mix_kernel(x_ref, g1_ref, oc_ref, os_ref, ow_ref, gl_ref, ge_ref,
                od0_ref, od1_ref, od2_ref, l0_ref, l1_ref, l2_ref, ml_ref,
                wn_ref, wd_ref, wo_ref, o_ref, slab_ref):
    tm, d = x_ref.shape

    def in_token_order(ref):
        if len(ref.shape) == 2:
            return ref[...].astype(F32)
        dil = ref.shape[0]
        n_slab = ref.shape[2] // LANES
        for r in range(dil):
            for c in range(n_slab):
                slab_ref[c, pl.ds(r, tm // dil, stride=dil), :] = ref[r, :, c * LANES:(c + 1) * LANES].astype(F32)
        return jnp.concatenate([slab_ref[c] for c in range(n_slab)], axis=1)

    sig = 1.0 / (1.0 + jnp.exp(-gl_ref[...]))
    hi = sig.astype(BF16)
    lo = (sig - hi.astype(F32)).astype(BF16)
    ge = ge_ref[...]
    gates = jnp.dot(hi, ge, preferred_element_type=F32) + jnp.dot(lo, ge, preferred_element_type=F32)
    o_nsa = (gates[:, 0:512] * oc_ref[...].astype(F32)
             + gates[:, 512:1024] * os_ref[...].astype(F32)
             + gates[:, 1024:1536] * ow_ref[...].astype(F32))
    u_nsa = jnp.dot(o_nsa.astype(BF16), wn_ref[...], preferred_element_type=F32)

    lses = [in_token_order(ref) for ref in (l0_ref, l1_ref, l2_ref)]
    mx = jnp.maximum(jnp.maximum(lses[0], lses[1]), lses[2])
    es = [jnp.exp2(l - mx) for l in lses]
    inv = 1.0 / (es[0] + es[1] + es[2])
    o_dil = (es[0] * in_token_order(od0_ref) + es[1] * in_token_order(od1_ref)
             + es[2] * in_token_order(od2_ref)) * inv
    u_dil = jnp.dot(o_dil.astype(BF16), wd_ref[...], preferred_element_type=F32)

    gm = 1.0 / (1.0 + jnp.exp(-ml_ref[...].astype(F32)))
    merged = gm[:, :d] * u_nsa + gm[:, d:] * u_dil
    y = jnp.dot(merged.astype(BF16), wo_ref[...], preferred_element_type=F32)
    o_ref[...] = x_ref[...] + g1_ref[...] * y


def _mix(x2, g1, oc, osel, ow, gl, ge, od, lse, ml, wn, wd, wo, seq):
    t, d = x2.shape
    tm = TM_PROJ
    per_b = seq // tm
    row = lambda i: (i, 0)
    const = lambda i: (0, 0)
    full = lambda a: pl.BlockSpec(a.shape, const)

    def rows(a):
        if a.ndim == 2:
            return pl.BlockSpec((tm, a.shape[1]), row)
        dil = a.shape[1]
        return pl.BlockSpec((None, dil, tm // dil, a.shape[3]), lambda i: (i // per_b, 0, i % per_b, 0))

    return pl.pallas_call(
        _mix_kernel,
        grid=(t // tm,),
        in_specs=[rows(x2), pl.BlockSpec((None, 1, d), lambda i: (i // per_b, 0, 0)),
                  rows(oc), rows(osel), rows(ow), rows(gl), full(ge),
                  rows(od[0]), rows(od[1]), rows(od[2]), rows(lse[0]), rows(lse[1]), rows(lse[2]), rows(ml),
                  full(wn), full(wd), full(wo)],
        out_specs=pl.BlockSpec((tm, d), row),
        out_shape=jax.ShapeDtypeStruct((t, d), F32),
        scratch_shapes=[pltpu.VMEM((2, tm, LANES), F32)],
        compiler_params=_cparams(("arbitrary",)),
        name="mix_outproj",
    )(x2, g1, oc, osel, ow, gl, ge, *od, *lse, ml, wn, wd, wo)


def _router_kernel(x_ref, sc_ref, sh_ref, g_ref, wr_ref, br_ref, h_ref, eid_ref, wts_ref):
    h = _norm_mod(x_ref[...], g_ref[...], sc_ref[...], sh_ref[...])
    n_sub = h.shape[1] // LANES
    for j in range(n_sub):
        h_ref[pl.ds(j, h.shape[0], stride=n_sub), :] = h[:, j * LANES:(j + 1) * LANES]
    nt = (((1,), (1,)), ((), ()))
    w = wr_ref[...]
    w_hi, h_hi = w.astype(BF16), h.astype(BF16)
    w_lo, h_lo = (w - w_hi.astype(F32)).astype(BF16), (h - h_hi.astype(F32)).astype(BF16)
    logit = (lax.dot_general(w_hi, h_hi, nt, preferred_element_type=F32)
             + lax.dot_general(w_hi, h_lo, nt, preferred_element_type=F32)
             + lax.dot_general(w_lo, h_hi, nt, preferred_element_type=F32)) + br_ref[...]
    grp = jnp.zeros((1, h.shape[0]), jnp.int32)
    best = logit[0:1]
    for k in range(1, N_EXPERT_GROUPS):
        better = logit[k:k + 1] > best
        grp = jnp.where(better, k, grp)
        best = jnp.where(better, logit[k:k + 1], best)
    den = jnp.zeros_like(best)
    for k in range(N_EXPERT_GROUPS):
        den = den + jnp.exp(logit[k:k + 1] - best)
    p_grp = 1.0 / den
    le = logit[SUBLANES:SUBLANES + EXPERTS_PER_GROUP]
    for k in range(1, N_EXPERT_GROUPS):
        lo = SUBLANES + k * EXPERTS_PER_GROUP
        le = jnp.where(grp == k, logit[lo:lo + EXPERTS_PER_GROUP], le)
    rowi = lax.broadcasted_iota(jnp.int32, le.shape, 0)
    v1 = jnp.max(le, axis=0, keepdims=True)
    i1 = jnp.min(jnp.where(le == v1, rowi, EXPERTS_PER_GROUP), axis=0, keepdims=True)
    rest = jnp.where(rowi == i1, -3e38, le)
    v2 = jnp.max(rest, axis=0, keepdims=True)
    i2 = jnp.min(jnp.where(rest == v2, rowi, EXPERTS_PER_GROUP), axis=0, keepdims=True)
    e2 = jnp.exp(v2 - v1)
    inv = p_grp / (1.0 + e2)
    eid_ref[...] = jnp.concatenate([grp * EXPERTS_PER_GROUP + i1, grp * EXPERTS_PER_GROUP + i2], axis=0)
    wts_ref[...] = jnp.concatenate([inv, e2 * inv], axis=0)


def _router(x2, sc, sh, g, wr_t, br, seq):
    t, d = x2.shape
    tm = TM_PROJ
    per_b = seq // tm
    row = lambda i: (i, 0)
    const = lambda i: (0, 0)
    return pl.pallas_call(
        _router_kernel,
        grid=(t // tm,),
        in_specs=[pl.BlockSpec((tm, d), row),
                  pl.BlockSpec((None, 1, d), lambda i: (i // per_b, 0, 0)),
                  pl.BlockSpec((None, 1, d), lambda i: (i // per_b, 0, 0)),
                  pl.BlockSpec((1, d), const),
                  pl.BlockSpec(wr_t.shape, const),
                  pl.BlockSpec(br.shape, const)],
        out_specs=[pl.BlockSpec((tm * (d // LANES), LANES), row),
                   pl.BlockSpec((2, tm), lambda i: (0, i)),
                   pl.BlockSpec((2, tm), lambda i: (0, i))],
        out_shape=[jax.ShapeDtypeStruct((t * (d // LANES), LANES), F32),
                   jax.ShapeDtypeStruct((2, t), jnp.int32),
                   jax.ShapeDtypeStruct((2, t), F32)],
        compiler_params=_cparams(("arbitrary",)),
        name="norm_router",
    )(x2, sc, sh, g, wr_t, br)


def _moe_kernel(cnt_ref, off_ref, tok_ref, wt_ref, h_ref, w1_ref, w3_ref, w2_ref, o_ref,
                xs_ref, os_ref, xb_ref, y3_ref):
    c = pl.program_id(0)
    e = pl.program_id(1)
    n_sub = xb_ref.shape[1] // LANES
    rb = xb_ref.shape[0]

    @pl.when(e == 0)
    def _():
        o_ref[...] = jnp.zeros_like(o_ref)

    n = cnt_ref[c, e]
    off = off_ref[c, e]

    def every_row(base, last, fn):
        def grp(g, _):
            slot0 = base + g * MOE_UNROLL
            row0 = pl.multiple_of(g * (MOE_UNROLL * n_sub), MOE_UNROLL * n_sub)
            for u in range(MOE_UNROLL):
                fn(jnp.minimum(slot0 + u, last), pl.ds(row0 + u * n_sub, n_sub))
            return 0

        lax.fori_loop(0, rb // MOE_UNROLL, grp, 0)

    def token_rows(slot):
        return pl.ds(pl.multiple_of(tok_ref[0, slot], n_sub), n_sub)

    def block(bi, _):
        base = off + bi * rb
        last = off + jnp.minimum(n, (bi + 1) * rb) - 1

        def gather(slot, rows):
            src = token_rows(slot)
            xs_ref[rows, :] = h_ref[src, :]
            os_ref[rows, :] = o_ref[src, :]

        every_row(base, last, gather)
        for j in range(n_sub):
            xb_ref[:, j * LANES:(j + 1) * LANES] = xs_ref[pl.ds(j, rb, stride=n_sub), :].astype(BF16)
        xb = xb_ref[...]
        a = jnp.dot(xb, w1_ref[...], preferred_element_type=F32)
        b = jnp.dot(xb, w3_ref[...], preferred_element_type=F32)
        mid = (a * (1.0 / (1.0 + jnp.exp(-a))) * b).astype(BF16)
        y = jnp.dot(mid, w2_ref[...], preferred_element_type=F32)
        for j in range(n_sub):
            y3_ref[pl.ds(j, rb, stride=n_sub), :] = y[:, j * LANES:(j + 1) * LANES]

        def scatter(slot, rows):
            o_ref[token_rows(slot), :] = os_ref[rows, :] + wt_ref[0, slot] * y3_ref[rows, :]

        every_row(base, last, scatter)
        return 0

    lax.fori_loop(0, (n + rb - 1) // rb, block, 0)


def _moe(h2, eid, wts, w1, w3, w2, layer):
    d = w1.shape[2]
    n_sub = d // LANES
    t = h2.shape[0] // n_sub
    tc = min(MOE_CHUNK, t)
    n_chunks = t // tc
    slots = 2 * tc
    tok = jnp.arange(t, dtype=jnp.int32)
    key = ((tok // tc)[None, :] * N_EXPERTS + eid).reshape(-1)
    order = jnp.argsort(key)
    tok_sorted = (jnp.tile(tok % tc * n_sub, 2)[order]).reshape(n_chunks, slots)
    w_sorted = wts.reshape(-1)[order].reshape(n_chunks, slots)
    counts = jnp.zeros((n_chunks * N_EXPERTS,), jnp.int32).at[key].add(1).reshape(n_chunks, N_EXPERTS)
    starts = jnp.cumsum(counts, axis=1) - counts
    tok_sorted = tok_sorted.reshape(n_chunks, 1, slots)
    w_sorted = w_sorted.reshape(n_chunks, 1, slots)

    grid_spec = pltpu.PrefetchScalarGridSpec(
        num_scalar_prefetch=2,
        grid=(n_chunks, N_EXPERTS),
        in_specs=[pl.BlockSpec((None, 1, slots), lambda c, e, *_: (c, 0, 0), memory_space=pltpu.SMEM),
                  pl.BlockSpec((None, 1, slots), lambda c, e, *_: (c, 0, 0), memory_space=pltpu.SMEM),
                  pl.BlockSpec((tc * n_sub, LANES), lambda c, e, *_: (c, 0), pipeline_mode=pl.Buffered(1)),
                  pl.BlockSpec((None, None, d, D_EXPERT), lambda c, e, *_: (layer, e, 0, 0)),
                  pl.BlockSpec((None, None, d, D_EXPERT), lambda c, e, *_: (layer, e, 0, 0)),
                  pl.BlockSpec((None, None, D_EXPERT, d), lambda c, e, *_: (layer, e, 0, 0))],
        out_specs=pl.BlockSpec((tc * n_sub, LANES), lambda c, e, *_: (c, 0), pipeline_mode=pl.Buffered(1)),
        scratch_shapes=[pltpu.VMEM((MOE_ROWS * n_sub, LANES), F32),
                        pltpu.VMEM((MOE_ROWS * n_sub, LANES), F32),
                        pltpu.VMEM((MOE_ROWS, d), BF16),
                        pltpu.VMEM((MOE_ROWS * n_sub, LANES), F32)],
    )
    out = pl.pallas_call(
        _moe_kernel,
        grid_spec=grid_spec,
        out_shape=jax.ShapeDtypeStruct((t * n_sub, LANES), F32),
        compiler_params=_cparams(("arbitrary", "arbitrary")),
        name="moe_experts",
    )(counts, starts, tok_sorted, w_sorted, h2, w1, w3, w2)
    return out


def _resid_kernel(x_ref, y_ref, g_ref, nf_ref, o_ref):
    tm, d = x_ref.shape
    x = x_ref[...] + g_ref[...] * _tile_rows(y_ref, tm, d)
    ms = jnp.mean(x * x, axis=-1, keepdims=True)
    o_ref[...] = x * lax.rsqrt(ms + RMS_EPS) * nf_ref[...]


def _residual(x2, y2, g2, norm_f, seq):
    t, d = x2.shape
    tm = TM_PROJ
    per_b = seq // tm
    row = lambda i: (i, 0)
    return pl.pallas_call(
        _resid_kernel,
        grid=(t // tm,),
        in_specs=[pl.BlockSpec((tm, d), row), pl.BlockSpec((tm * (d // LANES), LANES), row),
                  pl.BlockSpec((None, 1, d), lambda i: (i // per_b, 0, 0)),
                  pl.BlockSpec((1, d), lambda i: (0, 0))],
        out_specs=pl.BlockSpec((tm, d), row),
        out_shape=jax.ShapeDtypeStruct((t, d), F32),
        compiler_params=_cparams(("arbitrary",)),
        name="residual_final",
    )(x2, y2, g2, norm_f)


def _split_w_in(w_in, d):
    scale = HEAD_DIM ** -0.5 * LOG2E
    nq = NSA_HEADS * HEAD_DIM
    nkv = 3 * 2 * NSA_KV_HEADS * HEAD_DIM
    ngate = 3 * NSA_HEADS
    ndil = 3 * N_DIL_GROUPS * DIL_HEADS_PER_GROUP * HEAD_DIM
    o1, o2, o3 = nq, nq + nkv, nq + nkv + ngate
    o4 = o3 + ndil
    wq = (w_in[:, :o1] * scale).reshape(d, NSA_KV_HEADS, NSA_GROUP, HEAD_DIM)
    wq = wq.transpose(0, 2, 1, 3).reshape(d, nq)
    wkv = w_in[:, o1:o2]
    wg = jnp.pad(w_in[:, o2:o3], ((0, 0), (0, LANES - ngate)))
    gw = DIL_HEADS_PER_GROUP * HEAD_DIM
    per_which = N_DIL_GROUPS * gw
    wds = []
    for grp in range(N_DIL_GROUPS):
        parts = [w_in[:, o3 + which * per_which + grp * gw: o3 + which * per_which + (grp + 1) * gw]
                 for which in range(3)]
        parts[0] = parts[0] * scale
        wds.append(jnp.concatenate(parts, axis=1))
    wm = w_in[:, o4:]
    cast = lambda w: w.astype(BF16)
    return [cast(wq), cast(wkv), cast(wds[0]), cast(wds[1]), cast(wds[2]), cast(wm), cast(wg)]


def kernel(x, c, rel_bias, ada_w, ada_b, norm1, norm2, w_in, cmp_pos, cmp_w1, cmp_w2, w_up_nsa, w_up_dil, w_o,
           router_wg, router_bg, router_we, router_be, exp_w1, exp_w3, exp_w2, norm_f):
    batch, seq, d = x.shape
    depth = ada_w.shape[0]
    t = batch * seq
    n_cmp = seq // CMP_STRIDE
    n_sel = seq // SEL_BLOCK
    assert seq % TQ_NSA == 0 and n_sel <= HEAD_DIM and n_sel >= SEL_TOPN
    assert all(seq % (dil * TQ_DIL) == 0 for _, dil in DIL_PAIRS)

    mod = _modulation(c, ada_w, ada_b)

    nq_nsa = seq // TQ_NSA
    bias_cmp = _expand_bias(rel_bias, _cmp_buckets(seq, n_cmp), 0, NSA_HEADS, 8 * SUBLANES)
    sel_b = _toeplitz_buckets(nq_nsa, TQ_NSA, seq, 1)
    n_sel_bias = nq_nsa
    while n_sel_bias > 1 and (sel_b[n_sel_bias - 2:] == sel_b[n_sel_bias - 1, 0, 0]).all():
        n_sel_bias -= 1
    sel_b = sel_b[:n_sel_bias].reshape(n_sel_bias * TQ_NSA, TQ_NSA)
    bias_sel = _expand_bias(rel_bias, sel_b, 0, NSA_HEADS, TQ_NSA).reshape(NSA_HEADS, n_sel_bias, TQ_NSA, TQ_NSA)
    nb_win = min(-(-NSA_WINDOW // TQ_NSA), nq_nsa - 1)
    win_b = _toeplitz_buckets(nb_win + 1, TQ_NSA, NSA_WINDOW, 1, masked_tail=True).reshape(-1, TQ_NSA)
    bias_win = _expand_bias(rel_bias, win_b, 0, NSA_HEADS, TQ_NSA).reshape(NSA_HEADS, nb_win + 2, TQ_NSA, TQ_NSA)
    bias_dil, nb_dil = [], []
    for grp, (window, dil) in enumerate(DIL_PAIRS):
        nb = min(-(-(window // dil) // TQ_DIL), seq // dil // TQ_DIL - 1)
        bk = _toeplitz_buckets(nb + 1, TQ_DIL, window // dil, dil, masked_tail=True).reshape(-1, TQ_DIL)
        hb = NSA_HEADS + grp * DIL_HEADS_PER_GROUP
        bias_dil.append(_expand_bias(rel_bias, bk, hb, DIL_HEADS_PER_GROUP, TQ_DIL)
                        .reshape(DIL_HEADS_PER_GROUP, nb + 2, TQ_DIL, TQ_DIL))
        nb_dil.append(nb)
    ovl_t = jnp.asarray(_overlap_t(n_cmp, n_sel), BF16)
    onehot = jnp.asarray(_block_onehot(seq), BF16)
    gate_e = jnp.asarray(_gate_expand(), BF16)

    expert_w = [w.astype(BF16) for w in (exp_w1, exp_w3, exp_w2)]
    x2 = x.reshape(t, d)
    pending = None
    for l in range(depth):
        sh1, sc1, g1, sh2, sc2, g2 = [m.reshape(batch, 1, d) for m in jnp.split(mod[l], 6, axis=-1)]
        weights = _split_w_in(w_in[l], d)
        outs = list(_inproj(x2, sc1, sh1, norm1[l].reshape(1, d), onehot, weights, seq, pending))
        if pending is not None:
            x2 = outs.pop()
        (q_n, kc_in, vc_in, k_sel, v_sel, k_win, v_win, qkv_d0, qkv_d1, qkv_d2, merge_l, gate_l) = outs

        kc = _compress(kc_in, cmp_pos[l, 0], cmp_w1[l, 0], cmp_w2[l, 0], batch, seq, transpose_out=False)
        vct = _compress(vc_in, cmp_pos[l, 1], cmp_w1[l, 1], cmp_w2[l, 1], batch, seq, transpose_out=True)
        o_c, selb = _cmp_select(q_n, kc, vct, bias_cmp, ovl_t, batch, seq)
        nsa_common = dict(batch=batch, length=seq, dil=1, tile=TQ_NSA, n_pairs=NSA_GROUP, q_blk=lambda r: 0,
                          k_blk=lambda r: 0, v_blk=lambda r: 0, v_width=LANES, per_pair_kv=False,
                          heads=tuple((g, NSA_GROUP + g) for g in range(NSA_GROUP)))
        q3 = q_n.reshape(batch, seq, 512)
        (o_s,) = _flash(q3, k_sel.reshape(batch, seq, 2 * LANES), v_sel.reshape(batch, seq, LANES), bias_sel,
                        n_back=None, k_width=2 * LANES, selb=selb.reshape(batch, seq, 2 * LANES), **nsa_common)
        (o_w,) = _flash(q3, k_win.reshape(batch, seq, LANES), v_win.reshape(batch, seq, LANES), bias_win,
                        n_back=nb_win, k_width=LANES, q_tiles=WIN_Q_TILES, **nsa_common)

        o_d, lse_d = [], []
        for grp, ((window, dil), qkv) in enumerate(zip(DIL_PAIRS, (qkv_d0, qkv_d1, qkv_d2))):
            length = seq // dil
            view = qkv.reshape(batch * dil, length, 768)
            o_g, lse_g = _flash(view, view, view, bias_dil[grp], batch=batch * dil, length=length, dil=1,
                                tile=TQ_DIL, n_back=nb_dil[grp], n_pairs=2, q_blk=lambda r: 0, k_blk=lambda r: 1,
                                k_width=2 * LANES, v_blk=lambda r: 2, v_width=2 * LANES,
                                per_pair_kv=True, heads=((0, 1), (2, 3)), with_lse=True,
                                q_tiles=math.gcd(DIL_Q_TILES, length // TQ_DIL))
            shape = (t, 256) if dil == 1 else (batch, dil, length, 256)
            o_d.append(o_g.reshape(shape))
            lse_d.append(lse_g.reshape(shape))

        wn = w_up_nsa[l].reshape(NSA_KV_HEADS, NSA_GROUP, HEAD_DIM, d).transpose(1, 0, 2, 3).reshape(512, d)
        x2 = _mix(x2, g1, o_c, o_s.reshape(t, 512), o_w.reshape(t, 512), gate_l, gate_e, o_d, lse_d, merge_l,
                  wn.astype(BF16), w_up_dil[l].astype(BF16), w_o[l].astype(BF16), seq)

        wr_t = jnp.concatenate([jnp.pad(router_wg[l], ((0, 0), (0, SUBLANES - N_EXPERT_GROUPS))),
                                router_we[l]], axis=1).T
        br = jnp.concatenate([jnp.pad(router_bg[l], (0, SUBLANES - N_EXPERT_GROUPS)),
                              router_be[l]]).reshape(-1, 1)
        h2, eid, wts = _router(x2, sc2, sh2, norm2[l].reshape(1, d), wr_t, br, seq)
        y = _moe(h2, eid, wts, *expert_w, layer=l)
        if l == depth - 1:
            x2 = _residual(x2, y, g2, norm_f.reshape(1, d), seq)
        else:
            pending = (y, g2)
    return x2.reshape(batch, seq, d)
```

```python
import functools
import math

import numpy as np
import jax
import jax.numpy as jnp
from jax import lax
from jax.experimental import pallas as pl
from jax.experimental.pallas import tpu as pltpu

F32 = jnp.float32
BF16 = jnp.bfloat16

HEAD_DIM = 64
NSA_HEADS = 8
NSA_KV_HEADS = 2
NSA_GROUP = NSA_HEADS // NSA_KV_HEADS
CMP_LEN = 32
CMP_STRIDE = 16
CMP_HIDDEN = 256
SEL_BLOCK = 64
SEL_TOPN = 16
NSA_WINDOW = 512
FORCE_SCORE = 1e4
DIL_PAIRS = ((128, 1), (512, 4), (2048, 16))
N_DIL_GROUPS = 3
DIL_HEADS_PER_GROUP = 4
NUM_BUCKETS = 32
REL_MAX_DIST = 2048
N_EXPERT_GROUPS = 4
EXPERTS_PER_GROUP = 8
N_EXPERTS = N_EXPERT_GROUPS * EXPERTS_PER_GROUP
D_EXPERT = 512
RMS_EPS = 1e-6

LOG2E = math.log2(math.e)
LANES = 128
SUBLANES = 8
MASK_NEG = -1e30
SEL_NEG = -1e9
TQ_NSA = 256
TQ_DIL = 128
TM_PROJ = 512
WIN_Q_TILES = 4
DIL_Q_TILES = 8
FLASH_LOOKAHEAD = 4
MOE_CHUNK = 4096
MOE_ROWS = 128
MOE_UNROLL = 32
VMEM_LIMIT = 56 * 1024 * 1024


def _cparams(sem):
    return pltpu.CompilerParams(dimension_semantics=sem, vmem_limit_bytes=VMEM_LIMIT)


def _np_bucket(dist):
    dist = np.maximum(dist, 0)
    max_exact = NUM_BUCKETS // 2
    df = np.maximum(dist, 1).astype(np.float32)
    val = np.log(df / np.float32(max_exact)) / np.float32(math.log(REL_MAX_DIST / max_exact))
    large = max_exact + (val * np.float32(NUM_BUCKETS - max_exact)).astype(np.int32)
    large = np.minimum(large, NUM_BUCKETS - 1)
    return np.where(dist < max_exact, dist, large).astype(np.int32)


def _toeplitz_buckets(n_delta, tile, window, dist_scale, masked_tail=False):
    dd = np.arange(n_delta)[:, None, None]
    r = np.arange(tile)[None, :, None]
    c = np.arange(tile)[None, None, :]
    dist = dd * tile + c - r
    valid = (dist >= 0) & (dist <= window)
    out = np.where(valid, _np_bucket(dist * dist_scale), -1).astype(np.int32)
    if masked_tail:
        out = np.concatenate([out, np.full((1, tile, tile), -1, np.int32)], axis=0)
    return out


def _cmp_buckets(seq, n_cmp):
    c_end = np.arange(n_cmp)[:, None] * CMP_STRIDE + CMP_LEN - 1
    t = np.arange(seq)[None, :]
    dist = t - c_end
    return np.where(dist >= 0, _np_bucket(dist), -1).astype(np.int32)


def _overlap_t(n_cmp, n_sel):
    c_start = np.arange(n_cmp)[None, :] * CMP_STRIDE
    s_start = np.arange(n_sel)[:, None] * SEL_BLOCK
    ov = np.clip(np.minimum(c_start + CMP_LEN, s_start + SEL_BLOCK) - np.maximum(c_start, s_start), 0, None)
    out = np.zeros((LANES, n_cmp), np.float32)
    out[64:64 + n_sel] = ov.astype(np.float32) / CMP_STRIDE
    return out


def _block_onehot(seq):
    oh = np.zeros((seq, LANES), np.float32)
    blk = np.arange(seq) // SEL_BLOCK
    oh[np.arange(seq), blk] = 1.0
    oh[np.arange(seq), 64 + blk] = 1.0
    return oh


def _gate_expand():
    e = np.zeros((LANES, 3 * NSA_HEADS * HEAD_DIM), np.float32)
    for br in range(3):
        for g in range(NSA_GROUP):
            for ln in range(LANES):
                kv = ln // HEAD_DIM
                e[br * NSA_HEADS + kv * NSA_GROUP + g, br * 512 + g * LANES + ln] = 1.0
    return e


def _mod_kernel(c_ref, w_ref, b_ref, o_ref):
    c = c_ref[...]
    cond = c * (1.0 / (1.0 + jnp.exp(-c)))
    o_ref[...] = jnp.dot(cond, w_ref[...], preferred_element_type=F32,
                         precision=lax.Precision.HIGHEST) + b_ref[...]


def _modulation(c, ada_w, ada_b):
    depth, d, n = ada_w.shape
    b = c.shape[0]
    tn = 1536
    return pl.pallas_call(
        _mod_kernel,
        grid=(depth, n // tn),
        in_specs=[pl.BlockSpec((b, d), lambda l, j: (0, 0)),
                  pl.BlockSpec((None, d, tn), lambda l, j: (l, 0, j)),
                  pl.BlockSpec((None, 1, tn), lambda l, j: (l, 0, j))],
        out_specs=pl.BlockSpec((None, b, tn), lambda l, j: (l, 0, j)),
        out_shape=jax.ShapeDtypeStruct((depth, b, n), F32),
        compiler_params=_cparams(("arbitrary", "arbitrary")),
        name="adaln_modulation",
    )(c, ada_w, ada_b.reshape(depth, 1, n))


def _bias_kernel(tbl_ref, bkt_ref, o_ref, *, head_base):
    h = pl.program_id(0) + head_base
    bkt = bkt_ref[...]
    acc = jnp.full(bkt.shape, MASK_NEG, F32)
    for b in range(NUM_BUCKETS):
        acc = jnp.where(bkt == b, tbl_ref[b, h] * LOG2E, acc)
    o_ref[...] = acc


def _expand_bias(rel_bias, buckets, head_base, n_heads, row_tile):
    rows, cols = buckets.shape
    return pl.pallas_call(
        functools.partial(_bias_kernel, head_base=head_base),
        grid=(n_heads, rows // row_tile),
        in_specs=[pl.BlockSpec(memory_space=pltpu.SMEM),
                  pl.BlockSpec((row_tile, cols), lambda h, i: (i, 0))],
        out_specs=pl.BlockSpec((None, row_tile, cols), lambda h, i: (h, i, 0)),
        out_shape=jax.ShapeDtypeStruct((n_heads, rows, cols), F32),
        compiler_params=_cparams(("arbitrary", "arbitrary")),
        name="bias_expand",
    )(rel_bias, jnp.asarray(buckets))


def _norm_mod(x, g, sc, sh):
    ms = jnp.mean(x * x, axis=-1, keepdims=True)
    y = x * lax.rsqrt(ms + RMS_EPS) * g
    return y * (1.0 + sc) + sh


def _tile_rows(y_ref, tm, d):
    n_sub = d // LANES
    return jnp.concatenate([y_ref[pl.ds(j, tm, stride=n_sub), :] for j in range(n_sub)], axis=1)


def _inproj_kernel(*refs, fused):
    it = iter(refs)
    x_ref = next(it)
    y_ref, g2_ref = (next(it), next(it)) if fused else (None, None)
    sc_ref, sh_ref, g_ref, oh_ref, wq_ref, wkv_ref, wd0_ref, wd1_ref, wd2_ref, wm_ref, wg_ref = (
        next(it) for _ in range(11))
    q_ref, kc_ref, vc_ref, ks_ref, vs_ref, kw_ref, vw_ref, d0_ref, d1_ref, d2_ref, m_ref, gl_ref = (
        next(it) for _ in range(12))
    xo_ref = next(it) if fused else None
    slab_ref = next(it)
    x = x_ref[...]
    tm = x.shape[0]
    if fused:
        x = x + g2_ref[...] * _tile_rows(y_ref, tm, x.shape[1])
        xo_ref[...] = x
    h = _norm_mod(x, g_ref[...], sc_ref[...], sh_ref[...]).astype(BF16)

    def by_residue(w_ref, o_ref):
        dil = o_ref.shape[0]
        res = jnp.dot(h, w_ref[...], preferred_element_type=F32)
        n_slab = res.shape[1] // LANES
        for c in range(n_slab):
            slab_ref[c] = res[:, c * LANES:(c + 1) * LANES]
        for r in range(dil):
            for c in range(n_slab):
                o_ref[r, :, c * LANES:(c + 1) * LANES] = slab_ref[c, pl.ds(r, tm // dil, stride=dil), :].astype(BF16)

    def proj(w_ref):
        return jnp.dot(h, w_ref[...], preferred_element_type=F32)

    q_ref[...] = proj(wq_ref).astype(BF16)
    kv = proj(wkv_ref).astype(BF16)
    for k, ref in ((0, kc_ref), (1, vc_ref), (3, vs_ref), (4, kw_ref), (5, vw_ref)):
        ref[...] = kv[:, k * LANES:(k + 1) * LANES]
    k_sel = kv[:, 2 * LANES:3 * LANES]
    oh = oh_ref[...]
    lo = lax.broadcasted_iota(jnp.int32, k_sel.shape, 1) < HEAD_DIM
    ks_ref[:, :LANES] = jnp.where(lo, k_sel, oh)
    ks_ref[:, LANES:] = jnp.where(lo, oh, k_sel)
    d0_ref[...] = proj(wd0_ref).astype(BF16)
    by_residue(wd1_ref, d1_ref)
    by_residue(wd2_ref, d2_ref)
    m_ref[...] = proj(wm_ref).astype(BF16)
    gl_ref[...] = proj(wg_ref)


def _inproj(x2, sc, sh, g, onehot, weights, seq, pending=None):
    t, d = x2.shape
    tm = TM_PROJ
    per_b = seq // tm
    widths = (512, 128, 128, 256, 128, 128, 128, 768, 768, 768, 2 * d, LANES)
    dtypes = (BF16,) * 11 + (F32,)
    row = lambda i: (i, 0)
    const = lambda i: (0, 0)
    per_batch = pl.BlockSpec((None, 1, d), lambda i: (i // per_b, 0, 0))
    in_specs = [pl.BlockSpec((tm, d), row)]
    args = [x2]
    if pending is not None:
        in_specs += [pl.BlockSpec((tm * (d // LANES), LANES), row), per_batch]
        args += list(pending)
    in_specs += [per_batch, per_batch, pl.BlockSpec((1, d), const),
                 pl.BlockSpec((tm, LANES), lambda i: (i % per_b, 0))]
    in_specs += [pl.BlockSpec(w.shape, const) for w in weights]
    args += [sc, sh, g, onehot, *weights]
    out_specs = [pl.BlockSpec((tm, w), row) for w in widths]
    out_shape = [jax.ShapeDtypeStruct((t, w), dt) for w, dt in zip(widths, dtypes)]
    for k, (_, dil) in zip((8, 9), DIL_PAIRS[1:]):
        out_specs[k] = pl.BlockSpec((None, dil, tm // dil, 768), lambda i: (i // per_b, 0, i % per_b, 0))
        out_shape[k] = jax.ShapeDtypeStruct((t // seq, dil, seq // dil, 768), BF16)
    if pending is not None:
        out_specs.append(pl.BlockSpec((tm, d), row))
        out_shape.append(jax.ShapeDtypeStruct((t, d), F32))
    return pl.pallas_call(
        functools.partial(_inproj_kernel, fused=pending is not None),
        grid=(t // tm,),
        in_specs=in_specs,
        out_specs=out_specs,
        out_shape=out_shape,
        scratch_shapes=[pltpu.VMEM((768 // LANES, tm, LANES), F32)],
        compiler_params=_cparams(("arbitrary",)),
        name="norm_inproj",
    )(*args)


def _gelu_tanh(x):
    return 0.5 * x * (1.0 + jnp.tanh(math.sqrt(2.0 / math.pi) * (x + 0.044715 * (x * x * x))))


def _compress_kernel(r_ref, pt_ref, pb_ref, wt_ref, wb_ref, w2_ref, o_ref, *, transpose_out):
    r = r_ref[...].astype(F32)
    top = jnp.dot((r + pt_ref[...]).astype(BF16), wt_ref[...], preferred_element_type=F32)
    bot = jnp.dot((r + pb_ref[...]).astype(BF16), wb_ref[...], preferred_element_type=F32)
    n = bot.shape[0]
    hid = top + pltpu.roll(bot, n - 1, 0)
    act = _gelu_tanh(hid).astype(BF16)
    if transpose_out:
        o_ref[...] = lax.dot_general(w2_ref[...], act, (((1,), (1,)), ((), ())),
                                     preferred_element_type=F32).astype(BF16)
    else:
        o_ref[...] = jnp.dot(act, w2_ref[...], preferred_element_type=F32).astype(BF16)


def _compress(tok, pos, w1, w2, batch, seq, transpose_out):
    nc = seq // CMP_STRIDE
    half = CMP_LEN // 2
    eye = jnp.eye(NSA_KV_HEADS, dtype=F32)
    w1r = w1.reshape(CMP_LEN, HEAD_DIM, CMP_HIDDEN)
    blk = lambda w: jnp.einsum('ldn,hg->lhdgn', w, eye).reshape(half * LANES, 2 * CMP_HIDDEN).astype(BF16)
    wt, wb = blk(w1r[:half]), blk(w1r[half:])
    posr = lambda p: jnp.broadcast_to(p[:, None, :], (half, NSA_KV_HEADS, HEAD_DIM)).reshape(1, half * LANES)
    pt, pb = posr(pos[:half]), posr(pos[half:])
    w2b = jnp.einsum('nd,hg->hngd', w2, eye).reshape(2 * CMP_HIDDEN, LANES)
    if transpose_out:
        w2b = w2b.T
        out_block, out_shape = (None, LANES, nc), (batch, LANES, nc)
    else:
        out_block, out_shape = (None, nc, LANES), (batch, nc, LANES)
    w2b = w2b.astype(BF16)
    const = lambda b: (0, 0)
    return pl.pallas_call(
        functools.partial(_compress_kernel, transpose_out=transpose_out),
        grid=(batch,),
        in_specs=[pl.BlockSpec((None, nc, half * LANES), lambda b: (b, 0, 0)),
                  pl.BlockSpec(pt.shape, const), pl.BlockSpec(pb.shape, const),
                  pl.BlockSpec(wt.shape, const), pl.BlockSpec(wb.shape, const),
                  pl.BlockSpec(w2b.shape, const)],
        out_specs=pl.BlockSpec(out_block, lambda b: (b, 0, 0)),
        out_shape=jax.ShapeDtypeStruct(out_shape, BF16),
        compiler_params=_cparams(("arbitrary",)),
        name="nsa_compress",
    )(tok.reshape(batch, nc, half * LANES), pt, pb, wt, wb, w2b)


def _cmp_select_kernel(q_ref, kc_ref, vct_ref, bias_ref, ovl_ref, o_ref, selb_ref, *, n_sel, n_top):
    tq = q_ref.shape[0]
    nc = kc_ref.shape[0]
    qs = pl.program_id(1) * tq
    kc = kc_ref[...]
    vct = vct_ref[...]
    lane_q = lax.broadcasted_iota(jnp.int32, (tq, LANES), 1)
    row_o = lax.broadcasted_iota(jnp.int32, (LANES, tq), 0)
    heads = [(g, kv) for g in range(NSA_GROUP) for kv in range(NSA_KV_HEADS)]

    def scores(g, kv):
        qt = q_ref[:, g * LANES:(g + 1) * LANES]
        mine = (lane_q < HEAD_DIM) if kv == 0 else (lane_q >= HEAD_DIM)
        qm = jnp.where(mine, qt, jnp.zeros_like(qt))
        s = lax.dot_general(kc, qm, (((1,), (1,)), ((), ())), preferred_element_type=F32)
        return s + bias_ref[kv * NSA_GROUP + g]

    pc_sum = [jnp.zeros((nc, tq), F32) for _ in range(NSA_KV_HEADS)]
    outs = {}
    ahead = [scores(*hd) for hd in heads[:FLASH_LOOKAHEAD]]
    for n, (g, kv) in enumerate(heads):
        s = ahead.pop(0)
        if n + FLASH_LOOKAHEAD < len(heads):
            ahead.append(scores(*heads[n + FLASH_LOOKAHEAD]))
        m = jnp.max(s, axis=0, keepdims=True)
        m = jnp.where(m < 0.5 * MASK_NEG, 0.0, m)
        p = jnp.exp2(s - m)
        den = jnp.sum(p, axis=0, keepdims=True)
        pc = p * (1.0 / jnp.where(den > 0.0, den, 1.0))
        pc_sum[kv] = pc_sum[kv] + pc
        outs[kv] = jnp.dot(vct, pc.astype(BF16), preferred_element_type=F32)
        if kv == NSA_KV_HEADS - 1:
            o_t = jnp.where(row_o < HEAD_DIM, outs[0], outs[1])
            o_ref[:, g * LANES:(g + 1) * LANES] = o_t.T.astype(BF16)

    ovl = ovl_ref[...]
    imps = []
    for kv in range(NSA_KV_HEADS):
        hi = pc_sum[kv].astype(BF16)
        lo = (pc_sum[kv] - hi.astype(F32)).astype(BF16)
        imps.append((jnp.dot(ovl, hi, preferred_element_type=F32)
                     + jnp.dot(ovl, lo, preferred_element_type=F32))[HEAD_DIM:])
    imp = jnp.concatenate(imps, axis=1)
    wide = (HEAD_DIM, NSA_KV_HEADS * tq)
    rowj = lax.broadcasted_iota(jnp.int32, wide, 0)
    col = lax.broadcasted_iota(jnp.int32, wide, 1)
    t = qs + jnp.where(col >= tq, col - tq, col)
    jq = jnp.right_shift(t, SEL_BLOCK.bit_length() - 1)
    forced = (rowj == 0) | (rowj == jq) | (rowj == jq - 1)
    score = jnp.where(rowj > jq, -1.0, imp)
    rem = jnp.where(forced | (rowj >= n_sel), -3e38, score)
    sel = jnp.where(forced, 1.0, 0.0)
    for _ in range(n_top - 3):
        m = jnp.max(rem, axis=0, keepdims=True)
        idx = jnp.min(jnp.where(rem == m, rowj, HEAD_DIM), axis=0, keepdims=True)
        pick = rowj == idx
        sel = jnp.where(pick, 1.0, sel)
        rem = jnp.where(pick, -3e38, rem)
    sb = jnp.where(sel > 0.5, 0.0, SEL_NEG)
    zero = jnp.zeros((HEAD_DIM, tq), F32)
    selb_ref[:, :LANES] = jnp.concatenate([zero, sb[:, :tq]], axis=0).T.astype(BF16)
    selb_ref[:, LANES:] = jnp.concatenate([sb[:, tq:], zero], axis=0).T.astype(BF16)


def _cmp_select(q, kc, vct, bias_c, ovl_t, batch, seq):
    t = q.shape[0]
    tq = TQ_NSA
    nq = seq // tq
    nc = seq // CMP_STRIDE
    n_sel = seq // SEL_BLOCK
    n_top = min(SEL_TOPN, n_sel)
    return pl.pallas_call(
        functools.partial(_cmp_select_kernel, n_sel=n_sel, n_top=n_top),
        grid=(batch, nq),
        in_specs=[pl.BlockSpec((tq, 512), lambda b, i: (b * nq + i, 0)),
                  pl.BlockSpec((None, nc, LANES), lambda b, i: (b, 0, 0)),
                  pl.BlockSpec((None, LANES, nc), lambda b, i: (b, 0, 0)),
                  pl.BlockSpec((NSA_HEADS, nc, tq), lambda b, i: (0, 0, i)),
                  pl.BlockSpec((LANES, nc), lambda b, i: (0, 0))],
        out_specs=[pl.BlockSpec((tq, 512), lambda b, i: (b * nq + i, 0)),
                   pl.BlockSpec((tq, 2 * LANES), lambda b, i: (b * nq + i, 0))],
        out_shape=[jax.ShapeDtypeStruct((t, 512), BF16),
                   jax.ShapeDtypeStruct((t, 2 * LANES), BF16)],
        compiler_params=_cparams(("arbitrary", "arbitrary")),
        name="nsa_cmp_select",
    )(q, kc, vct, bias_c, ovl_t)


def _flash_kernel(*refs, tile, q_tiles, n_pairs, heads, per_pair_kv, n_back, n_bias, with_sel, with_lse):
    it = iter(refs)
    q_ref, k_ref, v_ref, b_ref = (next(it) for _ in range(4))
    selb_ref = next(it) if with_sel else None
    o_ref = next(it)
    lse_ref = next(it) if with_lse else None
    qa_ref = next(it)
    acc_ref = next(it) if n_back is None else None

    i = pl.program_id(1)
    n_heads = 2 * n_pairs
    lo = lax.broadcasted_iota(jnp.int32, (tile, LANES), 1) < HEAD_DIM
    tr = lambda a: a.astype(F32).T.astype(BF16)
    for u in range(q_tiles):
        rows = slice(u * tile, (u + 1) * tile)
        for p in range(n_pairs):
            q = q_ref[rows, p * LANES:(p + 1) * LANES]
            if with_sel:
                qa_ref[u * n_heads + 2 * p] = tr(jnp.where(lo, q, selb_ref[rows, :LANES]))
                qa_ref[u * n_heads + 2 * p + 1] = tr(jnp.where(lo, selb_ref[rows, LANES:], q))
            else:
                zero = jnp.zeros_like(q)
                qa_ref[u * n_heads + 2 * p] = tr(jnp.where(lo, q, zero))
                qa_ref[u * n_heads + 2 * p + 1] = tr(jnp.where(lo, zero, q))
    nt = (((1,), (1,)), ((), ()))
    tn = (((0,), (0,)), ((), ()))
    top = lax.broadcasted_iota(jnp.int32, (LANES, tile), 0) < HEAD_DIM

    def finish(u, p, acc0, l0, m0, acc1, l1, m1):
        rows = slice(u * tile, (u + 1) * tile)
        o_t = jnp.where(top, acc0 * (1.0 / l0), acc1 * (1.0 / l1))
        o_ref[rows, p * LANES:(p + 1) * LANES] = o_t.T.astype(o_ref.dtype)
        if with_lse:
            lse_t = jnp.where(top, m0 + jnp.log2(l0), m1 + jnp.log2(l1))
            lse_ref[rows, p * LANES:(p + 1) * LANES] = lse_t.T

    def scores(unit, tiles):
        hd = unit % n_heads
        p, half = divmod(hd, 2)
        kc = half if with_sel else (p if per_pair_kv else 0)
        parts = []
        for j, bidx in tiles:
            kt = k_ref[pl.ds(pl.multiple_of(j * tile, tile), tile), kc * LANES:(kc + 1) * LANES]
            s = jnp.dot(kt, qa_ref[unit], preferred_element_type=F32)
            parts.append(s + b_ref[heads[p][half], bidx])
        return parts

    def group(unit_tiles, state):
        res = []
        n_units = len(unit_tiles)
        ahead = [scores(n, unit_tiles[n]) for n in range(min(FLASH_LOOKAHEAD, n_units))]
        for hd in range(n_units):
            tiles = unit_tiles[hd]
            parts = ahead.pop(0)
            if hd + FLASH_LOOKAHEAD < n_units:
                ahead.append(scores(hd + FLASH_LOOKAHEAD, unit_tiles[hd + FLASH_LOOKAHEAD]))
            m_new = functools.reduce(jnp.maximum, [jnp.max(s, axis=0, keepdims=True) for s in parts])
            if state is not None:
                m, l = state[hd]
                m_new = jnp.maximum(m, m_new)
                alpha = jnp.exp2(m - m_new)
            prs = [jnp.exp2(s - m_new) for s in parts]
            l_new = functools.reduce(jnp.add, [jnp.sum(pr, axis=0, keepdims=True) for pr in prs])
            vc = (hd % n_heads // 2) if per_pair_kv else 0
            pv = None
            for (j, _), pr in zip(tiles, prs):
                vt = v_ref[pl.ds(pl.multiple_of(j * tile, tile), tile), vc * LANES:(vc + 1) * LANES]
                d = lax.dot_general(vt, pr.astype(BF16), tn, preferred_element_type=F32)
                pv = d if pv is None else pv + d
            if state is not None:
                l_new = alpha * l + l_new
                pv = alpha * acc_ref[hd] + pv
            res.append((m_new, l_new, pv))
        return res

    if n_back is not None:
        unit_tiles = []
        for u in range(q_tiles):
            iu = i * q_tiles + u
            tiles = [(jnp.maximum(iu - k, 0), jnp.where(iu >= k, k, n_bias - 1)) for k in range(n_back + 1)]
            unit_tiles += [tiles] * n_heads
        res = group(unit_tiles, None)
        for u in range(q_tiles):
            for p in range(n_pairs):
                (m0, l0, a0), (m1, l1, a1) = res[u * n_heads + 2 * p], res[u * n_heads + 2 * p + 1]
                finish(u, p, a0, l0, m0, a1, l1, m1)
        return

    def store(res):
        for hd, (_, _, acc) in enumerate(res):
            acc_ref[hd] = acc
        return tuple(x for m, l, _ in res for x in (m, l))

    def unpack(carry):
        return [(carry[2 * hd], carry[2 * hd + 1]) for hd in range(n_heads)]

    bias_of = lambda back: jnp.minimum(back, n_bias - 1)
    carry = store(group([[(i, 0)]] * n_heads, None))

    def pair_body(step, carry):
        back = 2 * step + 1
        tiles = [(i - back, bias_of(back)), (i - back - 1, bias_of(back + 1))]
        return store(group([tiles] * n_heads, unpack(carry)))

    carry = lax.fori_loop(0, i // 2, pair_body, carry)

    def last_body(_, carry):
        return store(group([[(0, bias_of(i))]] * n_heads, unpack(carry)))

    carry = lax.fori_loop(0, i % 2, last_body, carry)
    fin = unpack(carry)
    for p in range(n_pairs):
        finish(0, p, acc_ref[2 * p], fin[2 * p][1], fin[2 * p][0], acc_ref[2 * p + 1], fin[2 * p + 1][1], fin[2 * p + 1][0])


def _flash(q_arr, k_arr, v_arr, bias, *, batch, length, dil, tile, n_back, n_pairs, q_blk, k_blk, k_width,
           v_blk, v_width, per_pair_kv, heads, selb=None, with_lse=False, q_tiles=1):
    nq = length // tile
    with_sel = selb is not None
    width = n_pairs * LANES

    in_specs = [pl.BlockSpec((None, q_tiles * tile, width), lambda br, i: (br // dil, i, q_blk(br % dil))),
                pl.BlockSpec((None, length, k_width), lambda br, i: (br // dil, 0, k_blk(br % dil))),
                pl.BlockSpec((None, length, v_width), lambda br, i: (br // dil, 0, v_blk(br % dil))),
                pl.BlockSpec(bias.shape, lambda br, i: (0, 0, 0, 0))]
    args = [q_arr, k_arr, v_arr, bias]
    if with_sel:
        in_specs.append(pl.BlockSpec((None, q_tiles * tile, 2 * LANES), lambda br, i: (br, i, 0)))
        args.append(selb)
    o_map = lambda br, i: (br // dil, i, br % dil)
    out_specs = [pl.BlockSpec((None, q_tiles * tile, width), o_map)]
    out_shape = [jax.ShapeDtypeStruct((batch, length, dil * width), BF16)]
    if with_lse:
        out_specs.append(pl.BlockSpec((None, q_tiles * tile, width), o_map))
        out_shape.append(jax.ShapeDtypeStruct((batch, length, dil * width), F32))
    return pl.pallas_call(
        functools.partial(_flash_kernel, tile=tile, q_tiles=q_tiles, n_pairs=n_pairs, heads=heads, per_pair_kv=per_pair_kv,
                          n_back=n_back, n_bias=bias.shape[1], with_sel=with_sel, with_lse=with_lse),
        grid=(batch * dil, nq // q_tiles),
        in_specs=in_specs,
        out_specs=out_specs,
        out_shape=out_shape,
        scratch_shapes=[pltpu.VMEM((q_tiles * 2 * n_pairs, LANES, tile), BF16)]
        + ([pltpu.VMEM((2 * n_pairs, LANES, tile), F32)] if n_back is None else []),
        compiler_params=_cparams(("arbitrary", "arbitrary")),
        name="flash_sel" if with_sel else ("flash_dil" if with_lse else "flash_win"),
    )(*args)


def _mix_kernel(x_ref, g1_ref, oc_ref, os_ref, ow_ref, gl_ref, ge_ref,
                od0_ref, od1_ref, od2_ref, l0_ref, l1_ref, l2_ref, ml_ref,
                wn_ref, wd_ref, wo_ref, o_ref, slab_ref):
    tm, d = x_ref.shape

    def in_token_order(ref):
        if len(ref.shape) == 2:
            return ref[...].astype(F32)
        dil = ref.shape[0]
        n_slab = ref.shape[2] // LANES
        for r in range(dil):
            for c in range(n_slab):
                slab_ref[c, pl.ds(r, tm // dil, stride=dil), :] = ref[r, :, c * LANES:(c + 1) * LANES].astype(F32)
        return jnp.concatenate([slab_ref[c] for c in range(n_slab)], axis=1)

    sig = 1.0 / (1.0 + jnp.exp(-gl_ref[...]))
    hi = sig.astype(BF16)
    lo = (sig - hi.astype(F32)).astype(BF16)
    ge = ge_ref[...]
    gates = jnp.dot(hi, ge, preferred_element_type=F32) + jnp.dot(lo, ge, preferred_element_type=F32)
    o_nsa = (gates[:, 0:512] * oc_ref[...].astype(F32)
             + gates[:, 512:1024] * os_ref[...].astype(F32)
             + gates[:, 1024:1536] * ow_ref[...].astype(F32))
    u_nsa = jnp.dot(o_nsa.astype(BF16), wn_ref[...], preferred_element_type=F32)

    lses = [in_token_order(ref) for ref in (l0_ref, l1_ref, l2_ref)]
    mx = jnp.maximum(jnp.maximum(lses[0], lses[1]), lses[2])
    es = [jnp.exp2(l - mx) for l in lses]
    inv = 1.0 / (es[0] + es[1] + es[2])
    o_dil = (es[0] * in_token_order(od0_ref) + es[1] * in_token_order(od1_ref)
             + es[2] * in_token_order(od2_ref)) * inv
    u_dil = jnp.dot(o_dil.astype(BF16), wd_ref[...], preferred_element_type=F32)

    gm = 1.0 / (1.0 + jnp.exp(-ml_ref[...].astype(F32)))
    merged = gm[:, :d] * u_nsa + gm[:, d:] * u_dil
    y = jnp.dot(merged.astype(BF16), wo_ref[...], preferred_element_type=F32)
    o_ref[...] = x_ref[...] + g1_ref[...] * y


def _mix(x2, g1, oc, osel, ow, gl, ge, od, lse, ml, wn, wd, wo, seq):
    t, d = x2.shape
    tm = TM_PROJ
    per_b = seq // tm
    row = lambda i: (i, 0)
    const = lambda i: (0, 0)
    full = lambda a: pl.BlockSpec(a.shape, const)

    def rows(a):
        if a.ndim == 2:
            return pl.BlockSpec((tm, a.shape[1]), row)
        dil = a.shape[1]
        return pl.BlockSpec((None, dil, tm // dil, a.shape[3]), lambda i: (i // per_b, 0, i % per_b, 0))

    return pl.pallas_call(
        _mix_kernel,
        grid=(t // tm,),
        in_specs=[rows(x2), pl.BlockSpec((None, 1, d), lambda i: (i // per_b, 0, 0)),
                  rows(oc), rows(osel), rows(ow), rows(gl), full(ge),
                  rows(od[0]), rows(od[1]), rows(od[2]), rows(lse[0]), rows(lse[1]), rows(lse[2]), rows(ml),
                  full(wn), full(wd), full(wo)],
        out_specs=pl.BlockSpec((tm, d), row),
        out_shape=jax.ShapeDtypeStruct((t, d), F32),
        scratch_shapes=[pltpu.VMEM((2, tm, LANES), F32)],
        compiler_params=_cparams(("arbitrary",)),
        name="mix_outproj",
    )(x2, g1, oc, osel, ow, gl, ge, *od, *lse, ml, wn, wd, wo)


def _router_kernel(x_ref, sc_ref, sh_ref, g_ref, wr_ref, br_ref, h_ref, eid_ref, wts_ref):
    h = _norm_mod(x_ref[...], g_ref[...], sc_ref[...], sh_ref[...])
    n_sub = h.shape[1] // LANES
    for j in range(n_sub):
        h_ref[pl.ds(j, h.shape[0], stride=n_sub), :] = h[:, j * LANES:(j + 1) * LANES]
    nt = (((1,), (1,)), ((), ()))
    w = wr_ref[...]
    w_hi, h_hi = w.astype(BF16), h.astype(BF16)
    w_lo, h_lo = (w - w_hi.astype(F32)).astype(BF16), (h - h_hi.astype(F32)).astype(BF16)
    logit = (lax.dot_general(w_hi, h_hi, nt, preferred_element_type=F32)
             + lax.dot_general(w_hi, h_lo, nt, preferred_element_type=F32)
             + lax.dot_general(w_lo, h_hi, nt, preferred_element_type=F32)) + br_ref[...]
    grp = jnp.zeros((1, h.shape[0]), jnp.int32)
    best = logit[0:1]
    for k in range(1, N_EXPERT_GROUPS):
        better = logit[k:k + 1] > best
        grp = jnp.where(better, k, grp)
        best = jnp.where(better, logit[k:k + 1], best)
    den = jnp.zeros_like(best)
    for k in range(N_EXPERT_GROUPS):
        den = den + jnp.exp(logit[k:k + 1] - best)
    p_grp = 1.0 / den
    le = logit[SUBLANES:SUBLANES + EXPERTS_PER_GROUP]
    for k in range(1, N_EXPERT_GROUPS):
        lo = SUBLANES + k * EXPERTS_PER_GROUP
        le = jnp.where(grp == k, logit[lo:lo + EXPERTS_PER_GROUP], le)
    rowi = lax.broadcasted_iota(jnp.int32, le.shape, 0)
    v1 = jnp.max(le, axis=0, keepdims=True)
    i1 = jnp.min(jnp.where(le == v1, rowi, EXPERTS_PER_GROUP), axis=0, keepdims=True)
    rest = jnp.where(rowi == i1, -3e38, le)
    v2 = jnp.max(rest, axis=0, keepdims=True)
    i2 = jnp.min(jnp.where(rest == v2, rowi, EXPERTS_PER_GROUP), axis=0, keepdims=True)
    e2 = jnp.exp(v2 - v1)
    inv = p_grp / (1.0 + e2)
    eid_ref[...] = jnp.concatenate([grp * EXPERTS_PER_GROUP + i1, grp * EXPERTS_PER_GROUP + i2], axis=0)
    wts_ref[...] = jnp.concatenate([inv, e2 * inv], axis=0)


def _router(x2, sc, sh, g, wr_t, br, seq):
    t, d = x2.shape
    tm = TM_PROJ
    per_b = seq // tm
    row = lambda i: (i, 0)
    const = lambda i: (0, 0)
    return pl.pallas_call(
        _router_kernel,
        grid=(t // tm,),
        in_specs=[pl.BlockSpec((tm, d), row),
                  pl.BlockSpec((None, 1, d), lambda i: (i // per_b, 0, 0)),
                  pl.BlockSpec((None, 1, d), lambda i: (i // per_b, 0, 0)),
                  pl.BlockSpec((1, d), const),
                  pl.BlockSpec(wr_t.shape, const),
                  pl.BlockSpec(br.shape, const)],
        out_specs=[pl.BlockSpec((tm * (d // LANES), LANES), row),
                   pl.BlockSpec((2, tm), lambda i: (0, i)),
                   pl.BlockSpec((2, tm), lambda i: (0, i))],
        out_shape=[jax.ShapeDtypeStruct((t * (d // LANES), LANES), F32),
                   jax.ShapeDtypeStruct((2, t), jnp.int32),
                   jax.ShapeDtypeStruct((2, t), F32)],
        compiler_params=_cparams(("arbitrary",)),
        name="norm_router",
    )(x2, sc, sh, g, wr_t, br)


def _moe_kernel(cnt_ref, off_ref, tok_ref, wt_ref, h_ref, w1_ref, w3_ref, w2_ref, o_ref,
                xs_ref, os_ref, xb_ref, y3_ref):
    c = pl.program_id(0)
    e = pl.program_id(1)
    n_sub = xb_ref.shape[1] // LANES
    rb = xb_ref.shape[0]

    @pl.when(e == 0)
    def _():
        o_ref[...] = jnp.zeros_like(o_ref)

    n = cnt_ref[c, e]
    off = off_ref[c, e]

    def every_row(base, last, fn):
        def grp(g, _):
            slot0 = base + g * MOE_UNROLL
            row0 = pl.multiple_of(g * (MOE_UNROLL * n_sub), MOE_UNROLL * n_sub)
            for u in range(MOE_UNROLL):
                fn(jnp.minimum(slot0 + u, last), pl.ds(row0 + u * n_sub, n_sub))
            return 0

        lax.fori_loop(0, rb // MOE_UNROLL, grp, 0)

    def token_rows(slot):
        return pl.ds(pl.multiple_of(tok_ref[0, slot], n_sub), n_sub)

    def block(bi, _):
        base = off + bi * rb
        last = off + jnp.minimum(n, (bi + 1) * rb) - 1

        def gather(slot, rows):
            src = token_rows(slot)
            xs_ref[rows, :] = h_ref[src, :]
            os_ref[rows, :] = o_ref[src, :]

        every_row(base, last, gather)
        for j in range(n_sub):
            xb_ref[:, j * LANES:(j + 1) * LANES] = xs_ref[pl.ds(j, rb, stride=n_sub), :].astype(BF16)
        xb = xb_ref[...]
        a = jnp.dot(xb, w1_ref[...], preferred_element_type=F32)
        b = jnp.dot(xb, w3_ref[...], preferred_element_type=F32)
        mid = (a * (1.0 / (1.0 + jnp.exp(-a))) * b).astype(BF16)
        y = jnp.dot(mid, w2_ref[...], preferred_element_type=F32)
        for j in range(n_sub):
            y3_ref[pl.ds(j, rb, stride=n_sub), :] = y[:, j * LANES:(j + 1) * LANES]

        def scatter(slot, rows):
            o_ref[token_rows(slot), :] = os_ref[rows, :] + wt_ref[0, slot] * y3_ref[rows, :]

        every_row(base, last, scatter)
        return 0

    lax.fori_loop(0, (n + rb - 1) // rb, block, 0)


def _moe(h2, eid, wts, w1, w3, w2, layer):
    d = w1.shape[2]
    n_sub = d // LANES
    t = h2.shape[0] // n_sub
    tc = min(MOE_CHUNK, t)
    n_chunks = t // tc
    slots = 2 * tc
    tok = jnp.arange(t, dtype=jnp.int32)
    key = ((tok // tc)[None, :] * N_EXPERTS + eid).reshape(-1)
    order = jnp.argsort(key)
    tok_sorted = (jnp.tile(tok % tc * n_sub, 2)[order]).reshape(n_chunks, slots)
    w_sorted = wts.reshape(-1)[order].reshape(n_chunks, slots)
    counts = jnp.zeros((n_chunks * N_EXPERTS,), jnp.int32).at[key].add(1).reshape(n_chunks, N_EXPERTS)
    starts = jnp.cumsum(counts, axis=1) - counts
    tok_sorted = tok_sorted.reshape(n_chunks, 1, slots)
    w_sorted = w_sorted.reshape(n_chunks, 1, slots)

    grid_spec = pltpu.PrefetchScalarGridSpec(
        num_scalar_prefetch=2,
        grid=(n_chunks, N_EXPERTS),
        in_specs=[pl.BlockSpec((None, 1, slots), lambda c, e, *_: (c, 0, 0), memory_space=pltpu.SMEM),
                  pl.BlockSpec((None, 1, slots), lambda c, e, *_: (c, 0, 0), memory_space=pltpu.SMEM),
                  pl.BlockSpec((tc * n_sub, LANES), lambda c, e, *_: (c, 0), pipeline_mode=pl.Buffered(1)),
                  pl.BlockSpec((None, None, d, D_EXPERT), lambda c, e, *_: (layer, e, 0, 0)),
                  pl.BlockSpec((None, None, d, D_EXPERT), lambda c, e, *_: (layer, e, 0, 0)),
                  pl.BlockSpec((None, None, D_EXPERT, d), lambda c, e, *_: (layer, e, 0, 0))],
        out_specs=pl.BlockSpec((tc * n_sub, LANES), lambda c, e, *_: (c, 0), pipeline_mode=pl.Buffered(1)),
        scratch_shapes=[pltpu.VMEM((MOE_ROWS * n_sub, LANES), F32),
                        pltpu.VMEM((MOE_ROWS * n_sub, LANES), F32),
                        pltpu.VMEM((MOE_ROWS, d), BF16),
                        pltpu.VMEM((MOE_ROWS * n_sub, LANES), F32)],
    )
    out = pl.pallas_call(
        _moe_kernel,
        grid_spec=grid_spec,
        out_shape=jax.ShapeDtypeStruct((t * n_sub, LANES), F32),
        compiler_params=_cparams(("arbitrary", "arbitrary")),
        name="moe_experts",
    )(counts, starts, tok_sorted, w_sorted, h2, w1, w3, w2)
    return out


def _resid_kernel(x_ref, y_ref, g_ref, nf_ref, o_ref):
    tm, d = x_ref.shape
    x = x_ref[...] + g_ref[...] * _tile_rows(y_ref, tm, d)
    ms = jnp.mean(x * x, axis=-1, keepdims=True)
    o_ref[...] = x * lax.rsqrt(ms + RMS_EPS) * nf_ref[...]


def _residual(x2, y2, g2, norm_f, seq):
    t, d = x2.shape
    tm = TM_PROJ
    per_b = seq // tm
    row = lambda i: (i, 0)
    return pl.pallas_call(
        _resid_kernel,
        grid=(t // tm,),
        in_specs=[pl.BlockSpec((tm, d), row), pl.BlockSpec((tm * (d // LANES), LANES), row),
                  pl.BlockSpec((None, 1, d), lambda i: (i // per_b, 0, 0)),
                  pl.BlockSpec((1, d), lambda i: (0, 0))],
        out_specs=pl.BlockSpec((tm, d), row),
        out_shape=jax.ShapeDtypeStruct((t, d), F32),
        compiler_params=_cparams(("arbitrary",)),
        name="residual_final",
    )(x2, y2, g2, norm_f)


def _split_w_in(w_in, d):
    scale = HEAD_DIM ** -0.5 * LOG2E
    nq = NSA_HEADS * HEAD_DIM
    nkv = 3 * 2 * NSA_KV_HEADS * HEAD_DIM
    ngate = 3 * NSA_HEADS
    ndil = 3 * N_DIL_GROUPS * DIL_HEADS_PER_GROUP * HEAD_DIM
    o1, o2, o3 = nq, nq + nkv, nq + nkv + ngate
    o4 = o3 + ndil
    wq = (w_in[:, :o1] * scale).reshape(d, NSA_KV_HEADS, NSA_GROUP, HEAD_DIM)
    wq = wq.transpose(0, 2, 1, 3).reshape(d, nq)
    wkv = w_in[:, o1:o2]
    wg = jnp.pad(w_in[:, o2:o3], ((0, 0), (0, LANES - ngate)))
    gw = DIL_HEADS_PER_GROUP * HEAD_DIM
    per_which = N_DIL_GROUPS * gw
    wds = []
    for grp in range(N_DIL_GROUPS):
        parts = [w_in[:, o3 + which * per_which + grp * gw: o3 + which * per_which + (grp + 1) * gw]
                 for which in range(3)]
        parts[0] = parts[0] * scale
        wds.append(jnp.concatenate(parts, axis=1))
    wm = w_in[:, o4:]
    cast = lambda w: w.astype(BF16)
    return [cast(wq), cast(wkv), cast(wds[0]), cast(wds[1]), cast(wds[2]), cast(wm), cast(wg)]


def kernel(x, c, rel_bias, ada_w, ada_b, norm1, norm2, w_in, cmp_pos, cmp_w1, cmp_w2, w_up_nsa, w_up_dil, w_o,
           router_wg, router_bg, router_we, router_be, exp_w1, exp_w3, exp_w2, norm_f):
    batch, seq, d = x.shape
    depth = ada_w.shape[0]
    t = batch * seq
    n_cmp = seq // CMP_STRIDE
    n_sel = seq // SEL_BLOCK
    assert seq % TQ_NSA == 0 and n_sel <= HEAD_DIM and n_sel >= SEL_TOPN
    assert all(seq % (dil * TQ_DIL) == 0 for _, dil in DIL_PAIRS)

    mod = _modulation(c, ada_w, ada_b)

    nq_nsa = seq // TQ_NSA
    bias_cmp = _expand_bias(rel_bias, _cmp_buckets(seq, n_cmp), 0, NSA_HEADS, 8 * SUBLANES)
    sel_b = _toeplitz_buckets(nq_nsa, TQ_NSA, seq, 1)
    n_sel_bias = nq_nsa
    while n_sel_bias > 1 and (sel_b[n_sel_bias - 2:] == sel_b[n_sel_bias - 1, 0, 0]).all():
        n_sel_bias -= 1
    sel_b = sel_b[:n_sel_bias].reshape(n_sel_bias * TQ_NSA, TQ_NSA)
    bias_sel = _expand_bias(rel_bias, sel_b, 0, NSA_HEADS, TQ_NSA).reshape(NSA_HEADS, n_sel_bias, TQ_NSA, TQ_NSA)
    nb_win = min(-(-NSA_WINDOW // TQ_NSA), nq_nsa - 1)
    win_b = _toeplitz_buckets(nb_win + 1, TQ_NSA, NSA_WINDOW, 1, masked_tail=True).reshape(-1, TQ_NSA)
    bias_win = _expand_bias(rel_bias, win_b, 0, NSA_HEADS, TQ_NSA).reshape(NSA_HEADS, nb_win + 2, TQ_NSA, TQ_NSA)
    bias_dil, nb_dil = [], []
    for grp, (window, dil) in enumerate(DIL_PAIRS):
        nb = min(-(-(window // dil) // TQ_DIL), seq // dil // TQ_DIL - 1)
        bk = _toeplitz_buckets(nb + 1, TQ_DIL, window // dil, dil, masked_tail=True).reshape(-1, TQ_DIL)
        hb = NSA_HEADS + grp * DIL_HEADS_PER_GROUP
        bias_dil.append(_expand_bias(rel_bias, bk, hb, DIL_HEADS_PER_GROUP, TQ_DIL)
                        .reshape(DIL_HEADS_PER_GROUP, nb + 2, TQ_DIL, TQ_DIL))
        nb_dil.append(nb)
    ovl_t = jnp.asarray(_overlap_t(n_cmp, n_sel), BF16)
    onehot = jnp.asarray(_block_onehot(seq), BF16)
    gate_e = jnp.asarray(_gate_expand(), BF16)

    expert_w = [w.astype(BF16) for w in (exp_w1, exp_w3, exp_w2)]
    x2 = x.reshape(t, d)
    pending = None
    for l in range(depth):
        sh1, sc1, g1, sh2, sc2, g2 = [m.reshape(batch, 1, d) for m in jnp.split(mod[l], 6, axis=-1)]
        weights = _split_w_in(w_in[l], d)
        outs = list(_inproj(x2, sc1, sh1, norm1[l].reshape(1, d), onehot, weights, seq, pending))
        if pending is not None:
            x2 = outs.pop()
        (q_n, kc_in, vc_in, k_sel, v_sel, k_win, v_win, qkv_d0, qkv_d1, qkv_d2, merge_l, gate_l) = outs

        kc = _compress(kc_in, cmp_pos[l, 0], cmp_w1[l, 0], cmp_w2[l, 0], batch, seq, transpose_out=False)
        vct = _compress(vc_in, cmp_pos[l, 1], cmp_w1[l, 1], cmp_w2[l, 1], batch, seq, transpose_out=True)
        o_c, selb = _cmp_select(q_n, kc, vct, bias_cmp, ovl_t, batch, seq)
        nsa_common = dict(batch=batch, length=seq, dil=1, tile=TQ_NSA, n_pairs=NSA_GROUP, q_blk=lambda r: 0,
                          k_blk=lambda r: 0, v_blk=lambda r: 0, v_width=LANES, per_pair_kv=False,
                          heads=tuple((g, NSA_GROUP + g) for g in range(NSA_GROUP)))
        q3 = q_n.reshape(batch, seq, 512)
        (o_s,) = _flash(q3, k_sel.reshape(batch, seq, 2 * LANES), v_sel.reshape(batch, seq, LANES), bias_sel,
                        n_back=None, k_width=2 * LANES, selb=selb.reshape(batch, seq, 2 * LANES), **nsa_common)
        (o_w,) = _flash(q3, k_win.reshape(batch, seq, LANES), v_win.reshape(batch, seq, LANES), bias_win,
                        n_back=nb_win, k_width=LANES, q_tiles=WIN_Q_TILES, **nsa_common)

        o_d, lse_d = [], []
        for grp, ((window, dil), qkv) in enumerate(zip(DIL_PAIRS, (qkv_d0, qkv_d1, qkv_d2))):
            length = seq // dil
            view = qkv.reshape(batch * dil, length, 768)
            o_g, lse_g = _flash(view, view, view, bias_dil[grp], batch=batch * dil, length=length, dil=1,
                                tile=TQ_DIL, n_back=nb_dil[grp], n_pairs=2, q_blk=lambda r: 0, k_blk=lambda r: 1,
                                k_width=2 * LANES, v_blk=lambda r: 2, v_width=2 * LANES,
                                per_pair_kv=True, heads=((0, 1), (2, 3)), with_lse=True,
                                q_tiles=math.gcd(DIL_Q_TILES, length // TQ_DIL))
            shape = (t, 256) if dil == 1 else (batch, dil, length, 256)
            o_d.append(o_g.reshape(shape))
            lse_d.append(lse_g.reshape(shape))

        wn = w_up_nsa[l].reshape(NSA_KV_HEADS, NSA_GROUP, HEAD_DIM, d).transpose(1, 0, 2, 3).reshape(512, d)
        x2 = _mix(x2, g1, o_c, o_s.reshape(t, 512), o_w.reshape(t, 512), gate_l, gate_e, o_d, lse_d, merge_l,
                  wn.astype(BF16), w_up_dil[l].astype(BF16), w_o[l].astype(BF16), seq)

        wr_t = jnp.concatenate([jnp.pad(router_wg[l], ((0, 0), (0, SUBLANES - N_EXPERT_GROUPS))),
                                router_we[l]], axis=1).T
        br = jnp.concatenate([jnp.pad(router_bg[l], (0, SUBLANES - N_EXPERT_GROUPS)),
                              router_be[l]]).reshape(-1, 1)
        h2, eid, wts = _router(x2, sc2, sh2, norm2[l].reshape(1, d), wr_t, br, seq)
        y = _moe(h2, eid, wts, *expert_w, layer=l)
        if l == depth - 1:
            x2 = _residual(x2, y, g2, norm_f.reshape(1, d), seq)
        else:
            pending = (y, g2)
    return x2.reshape(batch, seq, d)
```
